```python
import jax, jax.numpy as jnp
from jax import lax
import numpy as np

D_MODEL = 2048
BATCH = 4
SEQ = 2048
DEPTH = 2

N_MIXERS = 2
D_MIX = D_MODEL
F_GROUPS = 4
SGU_HEADS = 8
CHUNK = 128
N_GROUPS = 4
E_PER_GROUP = 8
N_EXPERTS = N_GROUPS * E_PER_GROUP
TOP_K = 2
D_EXPERT = D_MODEL // 4
EXPERT_BLOCK = 128
N_A = (DEPTH + 1) // 2
N_B = DEPTH // 2
EPS = 1e-6

kernel_name = "hybrid_fnet_sgu_hmoe_adaln_encoder"


def rms_norm(x, g):
    xf = x.astype(jnp.float32)
    y = xf * lax.rsqrt(jnp.mean(xf * xf, axis=-1, keepdims=True) + EPS)
    return (y * g.astype(jnp.float32)).astype(x.dtype)


def modulate(h, shift, scale):
    return h * (1 + scale[:, None, :]) + shift[:, None, :]


def fourier_mix(h, w_in, w_out):
    b, s, _ = h.shape
    z = (h @ w_in).reshape(b, s, F_GROUPS, D_MIX // F_GROUPS).astype(jnp.float32)
    f = jnp.fft.fft2(z, axes=(1, 3), norm="ortho").real
    return f.reshape(b, s, D_MIX).astype(h.dtype) @ w_out


def sgu_mix(h, w_in, b_in, g_v, w_s, b_s, w_out):
    b, s, _ = h.shape
    z = jax.nn.gelu(h @ w_in + b_in)
    u, v = jnp.split(z, 2, axis=-1)
    v = rms_norm(v, g_v)
    dh = D_MIX // SGU_HEADS
    vc = v.reshape(b, s // CHUNK, CHUNK, SGU_HEADS, dh)
    mixed = jnp.einsum('hpq,bcqhd->bcphd', w_s, vc) + b_s.T[None, None, :, :, None]
    gated = u.reshape(b, s // CHUNK, CHUNK, SGU_HEADS, dh) * mixed
    return gated.reshape(b, s, D_MIX) @ w_out


def hier_moe(h, w_group, b_group, w_router, b_router, w_gate, w_up, w_down):
    b, s, d = h.shape
    t = b * s
    xt = h.reshape(t, d)
    gl = (xt @ w_group).astype(jnp.float32) + b_group.astype(jnp.float32)
    gp = jax.nn.softmax(gl, axis=-1)
    g_idx = jnp.argmax(gl, axis=-1)
    g_w = jnp.take_along_axis(gp, g_idx[:, None], axis=-1)[:, 0]
    el = (xt @ w_router).astype(jnp.float32) + b_router.astype(jnp.float32)
    el = el.reshape(t, N_GROUPS, E_PER_GROUP)
    el_g = jnp.take_along_axis(el, g_idx[:, None, None], axis=1)[:, 0]
    top_v, top_i = lax.top_k(el_g, TOP_K)
    top_w = jax.nn.softmax(top_v, axis=-1) * g_w[:, None]
    expert_id = (g_idx[:, None] * E_PER_GROUP + top_i).astype(jnp.int32)

    n_assign = t * TOP_K
    e_flat = expert_id.reshape(n_assign)
    w_flat = top_w.reshape(n_assign)
    tok = jnp.repeat(jnp.arange(t, dtype=jnp.int32), TOP_K)
    order = jnp.argsort(e_flat)
    e_s, tok_s, w_s = e_flat[order], tok[order], w_flat[order]
    counts = jnp.zeros((N_EXPERTS,), jnp.int32).at[e_flat].add(1)
    starts = jnp.cumsum(counts) - counts
    padded = (counts + EXPERT_BLOCK - 1) // EXPERT_BLOCK * EXPERT_BLOCK
    pends = jnp.cumsum(padded)
    pstarts = pends - padded
    rank = jnp.arange(n_assign, dtype=jnp.int32) - starts[e_s]
    dest = pstarts[e_s] + rank
    n_blocks = (n_assign + N_EXPERTS * (EXPERT_BLOCK - 1) + EXPERT_BLOCK - 1) // EXPERT_BLOCK
    n_rows = n_blocks * EXPERT_BLOCK
    buf = jnp.zeros((n_rows, d), xt.dtype).at[dest].set(xt[tok_s])
    block_start = jnp.arange(n_blocks, dtype=jnp.int32) * EXPERT_BLOCK
    block_e = jnp.minimum(jnp.searchsorted(pends, block_start, side='right'), N_EXPERTS - 1)

    def expert_block(args):
        xb, e = args
        return (jax.nn.silu(xb @ w_gate[e]) * (xb @ w_up[e])) @ w_down[e]

    yb = lax.map(expert_block, (buf.reshape(n_blocks, EXPERT_BLOCK, d), block_e))
    y_s = yb.reshape(n_rows, d)[dest] * w_s[:, None].astype(xt.dtype)
    out = jnp.zeros((t, d), xt.dtype).at[tok_s].add(y_s)
    return out.reshape(b, s, d)


def setup_inputs(seed: int = 0) -> dict:
    key = jax.random.key(seed)
    ks = jax.random.split(key, 24)
    D, DM, DE, NE = D_MODEL, D_MIX, D_EXPERT, N_EXPERTS

    def nrm(k, shape, scale):
        return jax.random.normal(k, shape, jnp.float32) * scale

    return {
        "x": nrm(ks[0], (BATCH, SEQ, D), 1.0),
        "c": nrm(ks[1], (BATCH, D), 1.0),
        "g_norm1": 1.0 + nrm(ks[2], (DEPTH, D), 0.02),
        "g_norm2": 1.0 + nrm(ks[3], (DEPTH, D), 0.02),
        "w_ada": nrm(ks[4], (DEPTH, D, 6 * D), 0.5 * D ** -0.5),
        "b_ada": nrm(ks[5], (DEPTH, 6 * D), 0.02),
        "fa_w_in": nrm(ks[6], (N_A, D, DM), D ** -0.5),
        "fa_w_out": nrm(ks[7], (N_A, DM, D), DM ** -0.5),
        "sg_w_in": nrm(ks[8], (N_B, D, 2 * DM), D ** -0.5),
        "sg_b_in": nrm(ks[9], (N_B, 2 * DM), 0.02),
        "sg_g_v": 1.0 + nrm(ks[10], (N_B, DM), 0.02),
        "sg_w_s": nrm(ks[11], (N_B, SGU_HEADS, CHUNK, CHUNK), CHUNK ** -0.5),
        "sg_b_s": 1.0 + nrm(ks[12], (N_B, SGU_HEADS, CHUNK), 0.02),
        "sg_w_out": nrm(ks[13], (N_B, DM, D), DM ** -0.5),
        "w_group": nrm(ks[14], (DEPTH, D, N_GROUPS), D ** -0.5),
        "b_group": nrm(ks[15], (DEPTH, N_GROUPS), 0.01),
        "w_router": nrm(ks[16], (DEPTH, D, NE), D ** -0.5),
        "b_router": nrm(ks[17], (DEPTH, NE), 0.01),
        "w_gate": nrm(ks[18], (DEPTH, NE, D, DE), D ** -0.5),
        "w_up": nrm(ks[19], (DEPTH, NE, D, DE), D ** -0.5),
        "w_down": nrm(ks[20], (DEPTH, NE, DE, D), DE ** -0.5),
        "g_final": 1.0 + nrm(ks[21], (D,), 0.02),
    }


def reference(x, c, g_norm1, g_norm2, w_ada, b_ada, fa_w_in, fa_w_out,
              sg_w_in, sg_b_in, sg_g_v, sg_w_s, sg_b_s, sg_w_out,
              w_group, b_group, w_router, b_router, w_gate, w_up, w_down, g_final):
    c_act = jax.nn.silu(c)
    for l in range(DEPTH):
        mod = c_act @ w_ada[l] + b_ada[l]
        sh1, sc1, gt1, sh2, sc2, gt2 = jnp.split(mod, 6, axis=-1)
        h = modulate(rms_norm(x, g_norm1[l]), sh1, sc1)
        j = l // N_MIXERS
        if l % N_MIXERS == 0:
            y = fourier_mix(h, fa_w_in[j], fa_w_out[j])
        else:
            y = sgu_mix(h, sg_w_in[j], sg_b_in[j], sg_g_v[j], sg_w_s[j], sg_b_s[j], sg_w_out[j])
        x = x + gt1[:, None, :] * y
        h = modulate(rms_norm(x, g_norm2[l]), sh2, sc2)
        y = hier_moe(h, w_group[l], b_group[l], w_router[l], b_router[l],
                     w_gate[l], w_up[l], w_down[l])
        x = x + gt2[:, None, :] * y
    return rms_norm(x, g_final)
```

```python
import functools

import numpy as np
import jax
import jax.numpy as jnp
from jax import lax
from jax.experimental import pallas as pl
from jax.experimental.pallas import tpu as pltpu

F32 = jnp.float32
BF16 = jnp.bfloat16

EPS = 1e-6
F_GROUPS = 4
SGU_HEADS = 8
CHUNK = 128
N_GROUPS = 4
E_PER_GROUP = 8
N_EXPERTS = N_GROUPS * E_PER_GROUP
TOP_K = 2
EXPERT_BLOCK = 128

LANES = 128
SUBLANES = 8
VMEM_LIMIT = 56 * 1024 * 1024
NEG_BIG = -1e30


def _cparams(sem, vmem=VMEM_LIMIT):
    return pltpu.CompilerParams(dimension_semantics=sem, vmem_limit_bytes=vmem)


def _rms(x, g):
    return x * lax.rsqrt(jnp.mean(x * x, axis=-1, keepdims=True) + EPS) * g


ADA_TN = 512
ADA_ROWS = 64


def _ada_kernel(cb_ref, w_ref, b_ref, o_ref, *, n_batch):
    d = w_ref.shape[1]
    tn = w_ref.shape[2]

    def body(i, accs):
        r = pl.multiple_of(i * ADA_ROWS, ADA_ROWS)
        wblk = w_ref[0, pl.ds(r, ADA_ROWS), :]
        out = []
        for b in range(n_batch):
            cv = cb_ref[b, pl.ds(r, ADA_ROWS), :]
            sv = cv / (1.0 + jnp.exp(-cv))
            p = wblk * jnp.tile(sv, (1, tn // LANES))
            out.append(accs[b] + p.reshape(ADA_ROWS // SUBLANES, SUBLANES, tn).sum(axis=0))
        return tuple(out)

    init = tuple(jnp.zeros((SUBLANES, tn), F32) for _ in range(n_batch))
    accs = lax.fori_loop(0, d // ADA_ROWS, body, init)
    rows = [jnp.sum(a, axis=0, keepdims=True) for a in accs]
    rows += [jnp.zeros((1, tn), F32)] * (SUBLANES - n_batch)
    o_ref[0] = jnp.concatenate(rows, axis=0) + b_ref[0]


def _ada(c, w_ada, b_ada):
    n_batch, d = c.shape
    depth, _, n6 = w_ada.shape
    cb = jnp.broadcast_to(c[:, :, None], (n_batch, d, LANES))
    return pl.pallas_call(
        functools.partial(_ada_kernel, n_batch=n_batch),
        grid=(depth, n6 // ADA_TN),
        in_specs=[
            pl.BlockSpec((n_batch, d, LANES), lambda l, j: (0, 0, 0)),
            pl.BlockSpec((1, d, ADA_TN), lambda l, j: (l, 0, j)),
            pl.BlockSpec((1, 1, ADA_TN), lambda l, j: (l, 0, j)),
        ],
        out_specs=pl.BlockSpec((1, SUBLANES, ADA_TN), lambda l, j: (l, 0, j)),
        out_shape=jax.ShapeDtypeStruct((depth, SUBLANES, n6), F32),
        compiler_params=_cparams(("arbitrary", "arbitrary")),
        name="ada_mod",
    )(cb, w_ada, b_ada.reshape(depth, 1, n6))


FN_TM = 256


def _fnet_in_kernel(x_ref, g_ref, sh_ref, sc_ref, win_ref, cs_ref, ab_ref):
    d = x_ref.shape[1]
    dg = d // F_GROUPS
    h = _rms(x_ref[...], g_ref[...]) * (1.0 + sc_ref[0]) + sh_ref[0]
    z = jnp.dot(h.astype(BF16), win_ref[...], preferred_element_type=F32)
    for g in range(F_GROUPS):
        zg = z[:, g * dg:(g + 1) * dg].astype(BF16)
        ab = jnp.dot(zg, cs_ref[...], preferred_element_type=F32)
        ab_ref[0, 0, :, g * dg:(g + 1) * dg] = ab[:, :dg].astype(BF16)
        ab_ref[0, 1, :, g * dg:(g + 1) * dg] = ab[:, dg:].astype(BF16)


def _fnet_in(x2, g, sh, sc, w_in_bf, cs_chan, n_batch, seq):
    t, d = x2.shape
    tpb = seq // FN_TM
    dg = d // F_GROUPS
    return pl.pallas_call(
        _fnet_in_kernel,
        grid=(t // FN_TM,),
        in_specs=[
            pl.BlockSpec((FN_TM, d), lambda i: (i, 0)),
            pl.BlockSpec((1, d), lambda i: (0, 0)),
            pl.BlockSpec((1, 1, d), lambda i: (i // tpb, 0, 0)),
            pl.BlockSpec((1, 1, d), lambda i: (i // tpb, 0, 0)),
            pl.BlockSpec((d, d), lambda i: (0, 0), pipeline_mode=pl.Buffered(1)),
            pl.BlockSpec((dg, 2 * dg), lambda i: (0, 0), pipeline_mode=pl.Buffered(1)),
        ],
        out_specs=pl.BlockSpec((1, 2, FN_TM, d), lambda i: (i // tpb, 0, i % tpb, 0)),
        out_shape=jax.ShapeDtypeStruct((n_batch, 2, seq, d), BF16),
        compiler_params=_cparams(("arbitrary",)),
        name="fnet_in",
    )(x2, g, sh, sc, w_in_bf, cs_chan)


FS_TK = 256


def _fnet_seq_kernel(cs_ref, ab_ref, wout_ref, x_ref, gt_ref, o_ref, *, scale):
    two, s, d = ab_ref.shape[1:]
    ab = ab_ref[0].reshape(two * s, d)
    f = jnp.dot(cs_ref[...], ab, preferred_element_type=F32) * scale
    y = jnp.dot(f.astype(BF16), wout_ref[...], preferred_element_type=F32)
    o_ref[...] = x_ref[...] + gt_ref[0] * y


def _fnet_seq(cs_seq, ab, w_out_bf, x2, gt, n_batch, seq):
    t, d = x2.shape
    tpb = seq // FS_TK
    scale = 1.0 / float(np.sqrt(seq * (d // F_GROUPS)))
    return pl.pallas_call(
        functools.partial(_fnet_seq_kernel, scale=scale),
        grid=(n_batch, tpb),
        in_specs=[
            pl.BlockSpec((FS_TK, 2 * seq), lambda b, k: (k, 0)),
            pl.BlockSpec((1, 2, seq, d), lambda b, k: (b, 0, 0, 0), pipeline_mode=pl.Buffered(1)),
            pl.BlockSpec((d, d), lambda b, k: (0, 0), pipeline_mode=pl.Buffered(1)),
            pl.BlockSpec((FS_TK, d), lambda b, k: (b * tpb + k, 0)),
            pl.BlockSpec((1, 1, d), lambda b, k: (b, 0, 0)),
        ],
        out_specs=pl.BlockSpec((FS_TK, d), lambda b, k: (b * tpb + k, 0)),
        out_shape=jax.ShapeDtypeStruct((t, d), F32),
        compiler_params=_cparams(("arbitrary", "arbitrary")),
        name="fnet_seq",
    )(cs_seq, ab, w_out_bf, x2, gt)


SG_TM = 128


def _gelu_tanh(x):
    c = float(np.sqrt(2.0 / np.pi))
    return x * (0.5 * (1.0 + jnp.tanh(c * (x + 0.044715 * (x * x * x)))))


def _sgu_kernel(x_ref, g_ref, sh_ref, sc_ref, gt_ref, win_ref, bin_ref, gv_ref, ws_ref, bs_ref,
                wout_ref, o_ref, gated_ref):
    tm, d = x_ref.shape
    dh = d // SGU_HEADS
    x = x_ref[...]
    h = _rms(x, g_ref[...]) * (1.0 + sc_ref[0]) + sh_ref[0]
    z = jnp.dot(h.astype(BF16), win_ref[...], preferred_element_type=F32) + bin_ref[...]
    z = _gelu_tanh(z)
    u = z[:, :d]
    v = _rms(z[:, d:], gv_ref[...])
    for c in range(tm // CHUNK):
        rows = slice(c * CHUNK, (c + 1) * CHUNK)
        for hd in range(SGU_HEADS):
            cols = slice(hd * dh, (hd + 1) * dh)
            vc = v[rows, cols].astype(BF16)
            m = jnp.dot(ws_ref[hd], vc, preferred_element_type=F32) + bs_ref[:, hd:hd + 1]
            gated_ref[rows, cols] = (u[rows, cols] * m).astype(BF16)
    y = jnp.dot(gated_ref[...], wout_ref[...], preferred_element_type=F32)
    o_ref[...] = x + gt_ref[0] * y


def _sgu(x2, g, sh, sc, gt, w_in_bf, b_in, g_v, w_s_bf, b_s_t, w_out_bf, seq):
    t, d = x2.shape
    tpb = seq // SG_TM
    const2 = lambda i: (0, 0)
    perb = lambda i: (i // tpb, 0, 0)
    return pl.pallas_call(
        _sgu_kernel,
        grid=(t // SG_TM,),
        in_specs=[
            pl.BlockSpec((SG_TM, d), lambda i: (i, 0)),
            pl.BlockSpec((1, d), const2),
            pl.BlockSpec((1, 1, d), perb),
            pl.BlockSpec((1, 1, d), perb),
            pl.BlockSpec((1, 1, d), perb),
            pl.BlockSpec((d, 2 * d), const2, pipeline_mode=pl.Buffered(1)),
            pl.BlockSpec((1, 2 * d), const2),
            pl.BlockSpec((1, d), const2),
            pl.BlockSpec((SGU_HEADS, CHUNK, CHUNK), lambda i: (0, 0, 0)),
            pl.BlockSpec((CHUNK, SGU_HEADS), const2),
            pl.BlockSpec((d, d), const2, pipeline_mode=pl.Buffered(1)),
        ],
        out_specs=pl.BlockSpec((SG_TM, d), lambda i: (i, 0)),
        out_shape=jax.ShapeDtypeStruct((t, d), F32),
        scratch_shapes=[pltpu.VMEM((SG_TM, d), BF16)],
        compiler_params=_cparams(("arbitrary",)),
        name="sgu_mix",
    )(x2, g, sh, sc, gt, w_in_bf, b_in, g_v, w_s_bf, b_s_t, w_out_bf)


RT_TM = 256


def _route_kernel(x_ref, g_ref, sh_ref, sc_ref, whl_ref, br_ref, h_ref, route_ref, cnt_ref, carry_ref):
    tm = x_ref.shape[0]
    i = pl.program_id(0)

    @pl.when(i == 0)
    def _():
        carry_ref[...] = jnp.zeros_like(carry_ref)

    h = _rms(x_ref[...], g_ref[...]) * (1.0 + sc_ref[0]) + sh_ref[0]
    h_ref[...] = h
    h_hi = h.astype(BF16)
    h_lo = (h - h_hi.astype(F32)).astype(BF16)
    whl = whl_ref[...]
    both = jnp.dot(h_hi, whl, preferred_element_type=F32)
    lo = jnp.dot(h_lo, whl[:, :LANES], preferred_element_type=F32)
    lg = both[:, :LANES] + both[:, LANES:] + lo + br_ref[...]

    lane = lax.broadcasted_iota(jnp.int32, (tm, LANES), 1)
    lanef = lane.astype(F32)
    is_g = lane < N_GROUPS
    gl = jnp.where(is_g, lg, NEG_BIG)
    gmax = jnp.max(gl, axis=1, keepdims=True)
    gidx = jnp.min(jnp.where(gl == gmax, lanef, float(LANES)), axis=1, keepdims=True)
    gsum = jnp.sum(jnp.where(is_g, jnp.exp(gl - gmax), 0.0), axis=1, keepdims=True)
    g_w = 1.0 / gsum
    lo_lane = float(N_GROUPS) + gidx * float(E_PER_GROUP)
    in_grp = (lanef >= lo_lane) & (lanef < lo_lane + float(E_PER_GROUP))
    el = jnp.where(in_grp, lg, NEG_BIG)
    v1 = jnp.max(el, axis=1, keepdims=True)
    i1 = jnp.min(jnp.where(el == v1, lanef, float(LANES)), axis=1, keepdims=True)
    el2 = jnp.where(lanef == i1, NEG_BIG, el)
    v2 = jnp.max(el2, axis=1, keepdims=True)
    i2 = jnp.min(jnp.where(el2 == v2, lanef, float(LANES)), axis=1, keepdims=True)
    p = jnp.exp(v2 - v1)
    w0 = g_w / (1.0 + p)
    w1 = g_w * p / (1.0 + p)
    e0 = i1 - float(N_GROUPS)
    e1 = i2 - float(N_GROUPS)

    oh0 = jnp.where(lanef == e0, 1.0, 0.0)
    oh1 = jnp.where(lanef == e1, 1.0, 0.0)
    rr = lax.broadcasted_iota(jnp.int32, (tm, tm), 0)
    cc = lax.broadcasted_iota(jnp.int32, (tm, tm), 1)
    tri = jnp.where(rr > cc, 1.0, 0.0).astype(BF16)
    pre0 = jnp.dot(tri, oh0.astype(BF16), preferred_element_type=F32)
    pre1 = jnp.dot(tri, oh1.astype(BF16), preferred_element_type=F32)
    carry = carry_ref[0:1, :]
    cnt0 = jnp.sum(oh0, axis=0, keepdims=True)
    cnt1 = jnp.sum(oh1, axis=0, keepdims=True)
    rank0 = jnp.sum(oh0 * (pre0 + carry), axis=1, keepdims=True)
    rank1 = jnp.sum(oh1 * (pre1 + carry + cnt0), axis=1, keepdims=True)
    new_carry = carry + cnt0 + cnt1
    carry_ref[...] = jnp.broadcast_to(new_carry, carry_ref.shape)
    cnt_ref[...] = jnp.broadcast_to(new_carry, cnt_ref.shape)

    route = jnp.where(lane == 0, e0, 0.0)
    route = jnp.where(lane == 1, e1, route)
    route = jnp.where(lane == 2, w0, route)
    route = jnp.where(lane == 3, w1, route)
    route = jnp.where(lane == 4, rank0, route)
    route = jnp.where(lane == 5, rank1, route)
    route_ref[...] = route


def _route(x2, g, sh, sc, whl, br, seq):
    t, d = x2.shape
    tpb = seq // RT_TM
    perb = lambda i: (i // tpb, 0, 0)
    return pl.pallas_call(
        _route_kernel,
        grid=(t // RT_TM,),
        in_specs=[
            pl.BlockSpec((RT_TM, d), lambda i: (i, 0)),
            pl.BlockSpec((1, d), lambda i: (0, 0)),
            pl.BlockSpec((1, 1, d), perb),
            pl.BlockSpec((1, 1, d), perb),
            pl.BlockSpec((d, 2 * LANES), lambda i: (0, 0)),
            pl.BlockSpec((1, LANES), lambda i: (0, 0)),
        ],
        out_specs=[
            pl.BlockSpec((RT_TM, d), lambda i: (i, 0)),
            pl.BlockSpec((RT_TM, LANES), lambda i: (i, 0)),
            pl.BlockSpec((SUBLANES, LANES), lambda i: (0, 0)),
        ],
        out_shape=[
            jax.ShapeDtypeStruct((t, d), F32),
            jax.ShapeDtypeStruct((t, LANES), F32),
            jax.ShapeDtypeStruct((SUBLANES, LANES), F32),
        ],
        scratch_shapes=[pltpu.VMEM((SUBLANES, LANES), F32)],
        compiler_params=_cparams(("arbitrary",)),
        name="moe_route",
    )(x2, g, sh, sc, whl, br)


SC_TM = 512


def _scatter_kernel(dest_ref, pad_start_ref, pad_n_ref, nu_ref, h_ref, buf_ref, zrow_ref, sem, zsem, tsem):
    tm = h_ref.shape[0]
    i = pl.program_id(0)
    base = i * (tm * TOP_K)
    n_blocks = buf_ref.shape[0] // EXPERT_BLOCK

    def row_copy(r, k):
        dst = dest_ref[base + r * TOP_K + k]
        return pltpu.make_async_copy(h_ref.at[pl.ds(r, 1), :], buf_ref.at[pl.ds(dst, 1), :], sem)

    def pad_copy(e, j):
        return pltpu.make_async_copy(zrow_ref.at[pl.ds(0, 1), :],
                                     buf_ref.at[pl.ds(pad_start_ref[e] + j, 1), :], zsem)

    def tail_copy(blk):
        row0 = pl.multiple_of(blk * EXPERT_BLOCK, EXPERT_BLOCK)
        return pltpu.make_async_copy(zrow_ref, buf_ref.at[pl.ds(row0, EXPERT_BLOCK), :], tsem)

    @pl.when(i == 0)
    def _():
        zrow_ref[...] = jnp.zeros_like(zrow_ref)

        def per_expert(e, _):
            lax.fori_loop(0, pad_n_ref[e], lambda j, c: (pad_copy(e, j).start(), c)[1], 0)
            return 0
        lax.fori_loop(0, N_EXPERTS, per_expert, 0)
        lax.fori_loop(nu_ref[0], n_blocks, lambda blk, c: (tail_copy(blk).start(), c)[1], 0)

    def issue(r, c):
        for k in range(TOP_K):
            row_copy(r, k).start()
        return c
    lax.fori_loop(0, tm, issue, 0)

    def drain(r, c):
        for k in range(TOP_K):
            row_copy(r, k).wait()
        return c
    lax.fori_loop(0, tm, drain, 0)

    @pl.when(i == 0)
    def _():
        def per_expert(e, _):
            lax.fori_loop(0, pad_n_ref[e], lambda j, c: (pad_copy(e, j).wait(), c)[1], 0)
            return 0
        lax.fori_loop(0, N_EXPERTS, per_expert, 0)
        lax.fori_loop(nu_ref[0], n_blocks, lambda blk, c: (tail_copy(blk).wait(), c)[1], 0)


def _scatter(dest_flat, pad_start, pad_n, n_used, h2, n_rows):
    t, d = h2.shape
    grid_spec = pltpu.PrefetchScalarGridSpec(
        num_scalar_prefetch=4,
        grid=(t // SC_TM,),
        in_specs=[pl.BlockSpec((SC_TM, d), lambda i, *_: (i, 0))],
        out_specs=pl.BlockSpec(memory_space=pl.ANY),
        scratch_shapes=[pltpu.VMEM((EXPERT_BLOCK, d), F32), pltpu.SemaphoreType.DMA, pltpu.SemaphoreType.DMA,
                        pltpu.SemaphoreType.DMA],
    )
    return pl.pallas_call(
        _scatter_kernel,
        grid_spec=grid_spec,
        out_shape=jax.ShapeDtypeStruct((n_rows, d), F32),
        compiler_params=_cparams(("arbitrary",)),
        name="moe_scatter",
    )(dest_flat, pad_start, pad_n, n_used, h2)


def _expert_kernel(be_ref, nu_ref, x_ref, wg_ref, wu_ref, wd_ref, o_ref, wg_bf, wu_bf, wd_bf):
    j = pl.program_id(0)
    used = j < nu_ref[0]

    @pl.when(used)
    def _():
        prev = be_ref[jnp.maximum(j - 1, 0)]
        fresh = (j == 0) | (be_ref[j] != prev)

        @pl.when(fresh)
        def _():
            wg_bf[...] = wg_ref[...].astype(BF16)
            wu_bf[...] = wu_ref[...].astype(BF16)
            wd_bf[...] = wd_ref[...].astype(BF16)

        xb = x_ref[...].astype(BF16)
        gte = jnp.dot(xb, wg_bf[...], preferred_element_type=F32)
        up = jnp.dot(xb, wu_bf[...], preferred_element_type=F32)
        act = gte * (1.0 / (1.0 + jnp.exp(-gte))) * up
        o_ref[...] = jnp.dot(act.astype(BF16), wd_bf[...], preferred_element_type=F32)

    @pl.when(jnp.logical_not(used))
    def _():
        o_ref[...] = jnp.zeros_like(o_ref)


def _experts(block_e, n_used, buf, w_gate, w_up, w_down, layer):
    n_rows, d = buf.shape
    de = w_gate.shape[-1]
    n_blocks = n_rows // EXPERT_BLOCK
    grid_spec = pltpu.PrefetchScalarGridSpec(
        num_scalar_prefetch=2,
        grid=(n_blocks,),
        in_specs=[
            pl.BlockSpec((EXPERT_BLOCK, d), lambda j, be, nu: (jnp.minimum(j, nu[0] - 1), 0)),
            pl.BlockSpec((None, None, d, de), lambda j, be, nu: (layer, be[j], 0, 0)),
            pl.BlockSpec((None, None, d, de), lambda j, be, nu: (layer, be[j], 0, 0)),
            pl.BlockSpec((None, None, de, d), lambda j, be, nu: (layer, be[j], 0, 0)),
        ],
        out_specs=pl.BlockSpec((EXPERT_BLOCK, d), lambda j, be, nu: (j, 0)),
        scratch_shapes=[pltpu.VMEM((d, de), BF16), pltpu.VMEM((d, de), BF16), pltpu.VMEM((de, d), BF16)],
    )
    return pl.pallas_call(
        _expert_kernel,
        grid_spec=grid_spec,
        out_shape=jax.ShapeDtypeStruct((n_rows, d), F32),
        compiler_params=_cparams(("arbitrary",)),
        name="moe_experts",
    )(block_e, n_used, buf, w_gate, w_up, w_down)


CB_TM = 128


def _combine_kernel(dest_ref, x_ref, route_ref, gt_ref, gf_ref, yb_ref, o_ref, ybuf, sems, *, final_norm):
    tm = x_ref.shape[0]
    i = pl.program_id(0)
    n = pl.num_programs(0)

    def row_copy(tile, slot, r, k):
        src = dest_ref[(tile * tm + r) * TOP_K + k]
        return pltpu.make_async_copy(yb_ref.at[pl.ds(src, 1), :], ybuf.at[slot, k, pl.ds(r, 1), :], sems.at[slot])

    def issue(tile, slot):
        def body(r, c):
            for k in range(TOP_K):
                row_copy(tile, slot, r, k).start()
            return c
        lax.fori_loop(0, tm, body, 0)

    @pl.when(i == 0)
    def _():
        issue(0, 0)

    @pl.when(i + 1 < n)
    def _():
        issue(i + 1, (i + 1) % 2)

    slot = i % 2

    def drain(r, c):
        for k in range(TOP_K):
            row_copy(i, slot, r, k).wait()
        return c
    lax.fori_loop(0, tm, drain, 0)

    route = route_ref[...]
    y = ybuf[slot, 0] * route[:, 2:3] + ybuf[slot, 1] * route[:, 3:4]
    xn = x_ref[...] + gt_ref[0] * y
    if final_norm:
        xn = _rms(xn, gf_ref[...])
    o_ref[...] = xn


def _combine(dest_flat, x2, route, gt, g_final, yb, seq, final_norm):
    t, d = x2.shape
    tpb = seq // CB_TM
    grid_spec = pltpu.PrefetchScalarGridSpec(
        num_scalar_prefetch=1,
        grid=(t // CB_TM,),
        in_specs=[
            pl.BlockSpec((CB_TM, d), lambda i, ds: (i, 0)),
            pl.BlockSpec((CB_TM, LANES), lambda i, ds: (i, 0)),
            pl.BlockSpec((1, 1, d), lambda i, ds: (i // tpb, 0, 0)),
            pl.BlockSpec((1, d), lambda i, ds: (0, 0)),
            pl.BlockSpec(memory_space=pl.ANY),
        ],
        out_specs=pl.BlockSpec((CB_TM, d), lambda i, ds: (i, 0)),
        scratch_shapes=[pltpu.VMEM((2, TOP_K, CB_TM, d), F32), pltpu.SemaphoreType.DMA((2,))],
    )
    return pl.pallas_call(
        functools.partial(_combine_kernel, final_norm=final_norm),
        grid_spec=grid_spec,
        out_shape=jax.ShapeDtypeStruct((t, d), F32),
        compiler_params=_cparams(("arbitrary",)),
        name="moe_combine",
    )(dest_flat, x2, route, gt, g_final, yb)


def _dft_cos_sin(n):
    k = np.arange(n, dtype=np.int64)
    ang = (np.outer(k, k) % n).astype(np.float64) * (2.0 * np.pi / n)
    return np.cos(ang), np.sin(ang)


@functools.lru_cache(maxsize=None)
def _dft_constants(seq, dg):
    cc, sc = _dft_cos_sin(dg)
    cs_chan = np.concatenate([cc, sc], axis=1).astype(np.float32)
    cq, sq = _dft_cos_sin(seq)
    cs_seq = np.concatenate([cq, -sq], axis=1).astype(np.float32)
    return cs_chan.astype(BF16), cs_seq.astype(BF16)


def _moe(x2, g2, sh, sc, gt, w_group, b_group, w_router, b_router, w_gate, w_up, w_down, layer,
         g_final, seq, final_norm):
    t, d = x2.shape
    w_all = jnp.concatenate([w_group, w_router], axis=1)
    w_all = jnp.pad(w_all, ((0, 0), (0, LANES - w_all.shape[1])))
    w_hi = w_all.astype(BF16)
    w_lo = (w_all - w_hi.astype(F32)).astype(BF16)
    whl = jnp.concatenate([w_hi, w_lo], axis=1)
    br = jnp.pad(jnp.concatenate([b_group, b_router]), (0, LANES - N_GROUPS - N_EXPERTS)).reshape(1, LANES)

    h2, route, cnt = _route(x2, g2, sh, sc, whl, br, seq)

    counts = cnt[0, :N_EXPERTS].astype(jnp.int32)
    padded = (counts + EXPERT_BLOCK - 1) // EXPERT_BLOCK * EXPERT_BLOCK
    pends = jnp.cumsum(padded)
    pstarts = pends - padded
    n_assign = t * TOP_K
    n_blocks = (n_assign + N_EXPERTS * (EXPERT_BLOCK - 1) + EXPERT_BLOCK - 1) // EXPERT_BLOCK
    n_rows = n_blocks * EXPERT_BLOCK
    e_idx = route[:, 0:TOP_K].astype(jnp.int32)
    rank = route[:, 4:4 + TOP_K].astype(jnp.int32)
    dest_flat = (pstarts[e_idx] + rank).reshape(n_assign)
    block_start = jnp.arange(n_blocks, dtype=jnp.int32) * EXPERT_BLOCK
    block_e = jnp.minimum(jnp.searchsorted(pends, block_start, side='right'), N_EXPERTS - 1).astype(jnp.int32)
    n_used = (pends[-1:] // EXPERT_BLOCK).astype(jnp.int32)

    buf = _scatter(dest_flat, (pstarts + counts).astype(jnp.int32), (padded - counts).astype(jnp.int32), n_used,
                   h2, n_rows)
    yb = _experts(block_e, n_used, buf, w_gate, w_up, w_down, layer)
    return _combine(dest_flat, x2, route, gt, g_final, yb, seq, final_norm)


def kernel(x, c, g_norm1, g_norm2, w_ada, b_ada, fa_w_in, fa_w_out, sg_w_in, sg_b_in, sg_g_v, sg_w_s, sg_b_s, sg_w_out, w_group, b_group, w_router, b_router, w_gate, w_up, w_down, g_final):
    n_batch, seq, d = x.shape
    depth = w_ada.shape[0]
    t = n_batch * seq
    x2 = x.reshape(t, d)

    mod = _ada(c, w_ada, b_ada)
    cs_chan, cs_seq = _dft_constants(seq, d // F_GROUPS)
    gfin = g_final.reshape(1, d)

    for l in range(depth):
        parts = [mod[l, :n_batch, k * d:(k + 1) * d].reshape(n_batch, 1, d) for k in range(6)]
        sh1, sc1, gt1, sh2, sc2, gt2 = parts
        g1 = g_norm1[l].reshape(1, d)
        j = l // 2
        if l % 2 == 0:
            ab = _fnet_in(x2, g1, sh1, sc1, fa_w_in[j].astype(BF16), jnp.asarray(cs_chan), n_batch, seq)
            x2 = _fnet_seq(jnp.asarray(cs_seq), ab, fa_w_out[j].astype(BF16), x2, gt1, n_batch, seq)
        else:
            x2 = _sgu(x2, g1, sh1, sc1, gt1, sg_w_in[j].astype(BF16), sg_b_in[j].reshape(1, 2 * d),
                      sg_g_v[j].reshape(1, d), sg_w_s[j].astype(BF16), sg_b_s[j].T, sg_w_out[j].astype(BF16), seq)
        x2 = _moe(x2, g_norm2[l].reshape(1, d), sh2, sc2, gt2, w_group[l], b_group[l], w_router[l], b_router[l],
                  w_gate, w_up, w_down, l, gfin, seq, final_norm=(l == depth - 1))
    return x2.reshape(n_batch, seq, d)
```

```python
import functools

import numpy as np
import jax
import jax.numpy as jnp
from jax import lax
from jax.experimental import pallas as pl
from jax.experimental.pallas import tpu as pltpu

F32 = jnp.float32
BF16 = jnp.bfloat16

EPS = 1e-6
F_GROUPS = 4
SGU_HEADS = 8
CHUNK = 128
N_GROUPS = 4
E_PER_GROUP = 8
N_EXPERTS = N_GROUPS * E_PER_GROUP
TOP_K = 2
EXPERT_BLOCK = 128

LANES = 128
SUBLANES = 8
VMEM_LIMIT = 56 * 1024 * 1024
NEG_BIG = -1e30


def _cparams(sem, vmem=VMEM_LIMIT):
    return pltpu.CompilerParams(dimension_semantics=sem, vmem_limit_bytes=vmem)


def _rms(x, g):
    return x * lax.rsqrt(jnp.mean(x * x, axis=-1, keepdims=True) + EPS) * g


ADA_TN = 512
ADA_ROWS = 64


def _ada_kernel(cb_ref, w_ref, b_ref, o_ref, *, n_batch):
    d = w_ref.shape[1]
    tn = w_ref.shape[2]

    def body(i, accs):
        r = pl.multiple_of(i * ADA_ROWS, ADA_ROWS)
        wblk = w_ref[0, pl.ds(r, ADA_ROWS), :]
        out = []
        for b in range(n_batch):
            cv = cb_ref[b, pl.ds(r, ADA_ROWS), :]
            sv = cv / (1.0 + jnp.exp(-cv))
            p = wblk * jnp.tile(sv, (1, tn // LANES))
            out.append(accs[b] + p.reshape(ADA_ROWS // SUBLANES, SUBLANES, tn).sum(axis=0))
        return tuple(out)

    init = tuple(jnp.zeros((SUBLANES, tn), F32) for _ in range(n_batch))
    accs = lax.fori_loop(0, d // ADA_ROWS, body, init)
    rows = [jnp.sum(a, axis=0, keepdims=True) for a in accs]
    rows += [jnp.zeros((1, tn), F32)] * (SUBLANES - n_batch)
    o_ref[0] = jnp.concatenate(rows, axis=0) + b_ref[0]


def _ada(c, w_ada, b_ada):
    n_batch, d = c.shape
    depth, _, n6 = w_ada.shape
    cb = jnp.broadcast_to(c[:, :, None], (n_batch, d, LANES))
    return pl.pallas_call(
        functools.partial(_ada_kernel, n_batch=n_batch),
        grid=(depth, n6 // ADA_TN),
        in_specs=[
            pl.BlockSpec((n_batch, d, LANES), lambda l, j: (0, 0, 0)),
            pl.BlockSpec((1, d, ADA_TN), lambda l, j: (l, 0, j)),
            pl.BlockSpec((1, 1, ADA_TN), lambda l, j: (l, 0, j)),
        ],
        out_specs=pl.BlockSpec((1, SUBLANES, ADA_TN), lambda l, j: (l, 0, j)),
        out_shape=jax.ShapeDtypeStruct((depth, SUBLANES, n6), F32),
        compiler_params=_cparams(("arbitrary", "arbitrary")),
        name="ada_mod",
    )(cb, w_ada, b_ada.reshape(depth, 1, n6))


FN_TM = 256


def _fnet_in_kernel(x_ref, g_ref, sh_ref, sc_ref, win_ref, cs_ref, ab_ref):
    d = x_ref.shape[1]
    dg = d // F_GROUPS
    h = _rms(x_ref[...], g_ref[...]) * (1.0 + sc_ref[0]) + sh_ref[0]
    z = jnp.dot(h.astype(BF16), win_ref[...], preferred_element_type=F32)
    for g in range(F_GROUPS):
        zg = z[:, g * dg:(g + 1) * dg].astype(BF16)
        ab = jnp.dot(zg, cs_ref[...], preferred_element_type=F32)
        ab_ref[0, 0, :, g * dg:(g + 1) * dg] = ab[:, :dg].astype(BF16)
        ab_ref[0, 1, :, g * dg:(g + 1) * dg] = ab[:, dg:].astype(BF16)


def _fnet_in(x2, g, sh, sc, w_in_bf, cs_chan, n_batch, seq):
    t, d = x2.shape
    tpb = seq // FN_TM
    dg = d // F_GROUPS
    return pl.pallas_call(
        _fnet_in_kernel,
        grid=(t // FN_TM,),
        in_specs=[
            pl.BlockSpec((FN_TM, d), lambda i: (i, 0)),
            pl.BlockSpec((1, d), lambda i: (0, 0)),
            pl.BlockSpec((1, 1, d), lambda i: (i // tpb, 0, 0)),
            pl.BlockSpec((1, 1, d), lambda i: (i // tpb, 0, 0)),
            pl.BlockSpec((d, d), lambda i: (0, 0), pipeline_mode=pl.Buffered(1)),
            pl.BlockSpec((dg, 2 * dg), lambda i: (0, 0), pipeline_mode=pl.Buffered(1)),
        ],
        out_specs=pl.BlockSpec((1, 2, FN_TM, d), lambda i: (i // tpb, 0, i % tpb, 0)),
        out_shape=jax.ShapeDtypeStruct((n_batch, 2, seq, d), BF16),
        compiler_params=_cparams(("arbitrary",)),
        name="fnet_in",
    )(x2, g, sh, sc, w_in_bf, cs_chan)


FS_TK = 256


def _fnet_seq_kernel(cs_ref, ab_ref, wout_ref, x_ref, gt_ref, o_ref, *, scale):
    two, s, d = ab_ref.shape[1:]
    ab = ab_ref[0].reshape(two * s, d)
    f = jnp.dot(cs_ref[...], ab, preferred_element_type=F32) * scale
    y = jnp.dot(f.astype(BF16), wout_ref[...], preferred_element_type=F32)
    o_ref[...] = x_ref[...] + gt_ref[0] * y


def _fnet_seq(cs_seq, ab, w_out_bf, x2, gt, n_batch, seq):
    t, d = x2.shape
    tpb = seq // FS_TK
    scale = 1.0 / float(np.sqrt(seq * (d // F_GROUPS)))
    return pl.pallas_call(
        functools.partial(_fnet_seq_kernel, scale=scale),
        grid=(n_batch, tpb),
        in_specs=[
            pl.BlockSpec((FS_TK, 2 * seq), lambda b, k: (k, 0)),
            pl.BlockSpec((1, 2, seq, d), lambda b, k: (b, 0, 0, 0), pipeline_mode=pl.Buffered(1)),
            pl.BlockSpec((d, d), lambda b, k: (0, 0), pipeline_mode=pl.Buffered(1)),
            pl.BlockSpec((FS_TK, d), lambda b, k: (b * tpb + k, 0)),
            pl.BlockSpec((1, 1, d), lambda b, k: (b, 0, 0)),
        ],
        out_specs=pl.BlockSpec((FS_TK, d), lambda b, k: (b * tpb + k, 0)),
        out_shape=jax.ShapeDtypeStruct((t, d), F32),
        compiler_params=_cparams(("arbitrary", "arbitrary")),
        name="fnet_seq",
    )(cs_seq, ab, w_out_bf, x2, gt)


SG_TM = 128


def _gelu_tanh(x):
    c = float(np.sqrt(2.0 / np.pi))
    return x * (0.5 * (1.0 + jnp.tanh(c * (x + 0.044715 * (x * x * x)))))


def _sgu_kernel(x_ref, g_ref, sh_ref, sc_ref, gt_ref, win_ref, bin_ref, gv_ref, ws_ref, bs_ref,
                wout_ref, o_ref, gated_ref):
    tm, d = x_ref.shape
    dh = d // SGU_HEADS
    x = x_ref[...]
    h = _rms(x, g_ref[...]) * (1.0 + sc_ref[0]) + sh_ref[0]
    z = jnp.dot(h.astype(BF16), win_ref[...], preferred_element_type=F32) + bin_ref[...]
    z = _gelu_tanh(z)
    u = z[:, :d]
    v = _rms(z[:, d:], gv_ref[...])
    for c in range(tm // CHUNK):
        rows = slice(c * CHUNK, (c + 1) * CHUNK)
        for hd in range(SGU_HEADS):
            cols = slice(hd * dh, (hd + 1) * dh)
            vc = v[rows, cols].astype(BF16)
            m = jnp.dot(ws_ref[hd], vc, preferred_element_type=F32) + bs_ref[:, hd:hd + 1]
            gated_ref[rows, cols] = (u[rows, cols] * m).astype(BF16)
    y = jnp.dot(gated_ref[...], wout_ref[...], preferred_element_type=F32)
    o_ref[...] = x + gt_ref[0] * y


def _sgu(x2, g, sh, sc, gt, w_in_bf, b_in, g_v, w_s_bf, b_s_t, w_out_bf, seq):
    t, d = x2.shape
    tpb = seq // SG_TM
    const2 = lambda i: (0, 0)
    perb = lambda i: (i // tpb, 0, 0)
    return pl.pallas_call(
        _sgu_kernel,
        grid=(t // SG_TM,),
        in_specs=[
            pl.BlockSpec((SG_TM, d), lambda i: (i, 0)),
            pl.BlockSpec((1, d), const2),
            pl.BlockSpec((1, 1, d), perb),
            pl.BlockSpec((1, 1, d), perb),
            pl.BlockSpec((1, 1, d), perb),
            pl.BlockSpec((d, 2 * d), const2, pipeline_mode=pl.Buffered(1)),
            pl.BlockSpec((1, 2 * d), const2),
            pl.BlockSpec((1, d), const2),
            pl.BlockSpec((SGU_HEADS, CHUNK, CHUNK), lambda i: (0, 0, 0)),
            pl.BlockSpec((CHUNK, SGU_HEADS), const2),
            pl.BlockSpec((d, d), const2, pipeline_mode=pl.Buffered(1)),
        ],
        out_specs=pl.BlockSpec((SG_TM, d), lambda i: (i, 0)),
        out_shape=jax.ShapeDtypeStruct((t, d), F32),
        scratch_shapes=[pltpu.VMEM((SG_TM, d), BF16)],
        compiler_params=_cparams(("arbitrary",)),
        name="sgu_mix",
    )(x2, g, sh, sc, gt, w_in_bf, b_in, g_v, w_s_bf, b_s_t, w_out_bf)


RT_TM = 256


def _route_kernel(x_ref, g_ref, sh_ref, sc_ref, whl_ref, br_ref, h_ref, route_ref, cnt_ref, carry_ref):
    tm = x_ref.shape[0]
    i = pl.program_id(0)

    @pl.when(i == 0)
    def _():
        carry_ref[...] = jnp.zeros_like(carry_ref)

    h = _rms(x_ref[...], g_ref[...]) * (1.0 + sc_ref[0]) + sh_ref[0]
    h_ref[...] = h
    h_hi = h.astype(BF16)
    h_lo = (h - h_hi.astype(F32)).astype(BF16)
    whl = whl_ref[...]
    both = jnp.dot(h_hi, whl, preferred_element_type=F32)
    lo = jnp.dot(h_lo, whl[:, :LANES], preferred_element_type=F32)
    lg = both[:, :LANES] + both[:, LANES:] + lo + br_ref[...]

    lane = lax.broadcasted_iota(jnp.int32, (tm, LANES), 1)
    lanef = lane.astype(F32)
    is_g = lane < N_GROUPS
    gl = jnp.where(is_g, lg, NEG_BIG)
    gmax = jnp.max(gl, axis=1, keepdims=True)
    gidx = jnp.min(jnp.where(gl == gmax, lanef, float(LANES)), axis=1, keepdims=True)
    gsum = jnp.sum(jnp.where(is_g, jnp.exp(gl - gmax), 0.0), axis=1, keepdims=True)
    g_w = 1.0 / gsum
    lo_lane = float(N_GROUPS) + gidx * float(E_PER_GROUP)
    in_grp = (lanef >= lo_lane) & (lanef < lo_lane + float(E_PER_GROUP))
    el = jnp.where(in_grp, lg, NEG_BIG)
    v1 = jnp.max(el, axis=1, keepdims=True)
    i1 = jnp.min(jnp.where(el == v1, lanef, float(LANES)), axis=1, keepdims=True)
    el2 = jnp.where(lanef == i1, NEG_BIG, el)
    v2 = jnp.max(el2, axis=1, keepdims=True)
    i2 = jnp.min(jnp.where(el2 == v2, lanef, float(LANES)), axis=1, keepdims=True)
    p = jnp.exp(v2 - v1)
    w0 = g_w / (1.0 + p)
    w1 = g_w * p / (1.0 + p)
    e0 = i1 - float(N_GROUPS)
    e1 = i2 - float(N_GROUPS)

    oh0 = jnp.where(lanef == e0, 1.0, 0.0)
    oh1 = jnp.where(lanef == e1, 1.0, 0.0)
    rr = lax.broadcasted_iota(jnp.int32, (tm, tm), 0)
    cc = lax.broadcasted_iota(jnp.int32, (tm, tm), 1)
    tri = jnp.where(rr > cc, 1.0, 0.0).astype(BF16)
    pre0 = jnp.dot(tri, oh0.astype(BF16), preferred_element_type=F32)
    pre1 = jnp.dot(tri, oh1.astype(BF16), preferred_element_type=F32)
    carry = carry_ref[0:1, :]
    cnt0 = jnp.sum(oh0, axis=0, keepdims=True)
    cnt1 = jnp.sum(oh1, axis=0, keepdims=True)
    rank0 = jnp.sum(oh0 * (pre0 + carry), axis=1, keepdims=True)
    rank1 = jnp.sum(oh1 * (pre1 + carry + cnt0), axis=1, keepdims=True)
    new_carry = carry + cnt0 + cnt1
    carry_ref[...] = jnp.broadcast_to(new_carry, carry_ref.shape)
    cnt_ref[...] = jnp.broadcast_to(new_carry, cnt_ref.shape)

    route = jnp.where(lane == 0, e0, 0.0)
    route = jnp.where(lane == 1, e1, route)
    route = jnp.where(lane == 2, w0, route)
    route = jnp.where(lane == 3, w1, route)
    route = jnp.where(lane == 4, rank0, route)
    route = jnp.where(lane == 5, rank1, route)
    route_ref[...] = route


def _route(x2, g, sh, sc, whl, br, seq):
    t, d = x2.shape
    tpb = seq // RT_TM
    perb = lambda i: (i // tpb, 0, 0)
    return pl.pallas_call(
        _route_kernel,
        grid=(t // RT_TM,),
        in_specs=[
            pl.BlockSpec((RT_TM, d), lambda i: (i, 0)),
            pl.BlockSpec((1, d), lambda i: (0, 0)),
            pl.BlockSpec((1, 1, d), perb),
            pl.BlockSpec((1, 1, d), perb),
            pl.BlockSpec((d, 2 * LANES), lambda i: (0, 0)),
            pl.BlockSpec((1, LANES), lambda i: (0, 0)),
        ],
        out_specs=[
            pl.BlockSpec((RT_TM, d), lambda i: (i, 0)),
            pl.BlockSpec((RT_TM, LANES), lambda i: (i, 0)),
            pl.BlockSpec((SUBLANES, LANES), lambda i: (0, 0)),
        ],
        out_shape=[
            jax.ShapeDtypeStruct((t, d), F32),
            jax.ShapeDtypeStruct((t, LANES), F32),
            jax.ShapeDtypeStruct((SUBLANES, LANES), F32),
        ],
        scratch_shapes=[pltpu.VMEM((SUBLANES, LANES), F32)],
        compiler_params=_cparams(("arbitrary",)),
        name="moe_route",
    )(x2, g, sh, sc, whl, br)


SC_TM = 512


def _scatter_kernel(dest_ref, pad_start_ref, pad_n_ref, nu_ref, h_ref, buf_ref, zrow_ref, sem, zsem, tsem):
    tm = h_ref.shape[0]
    i = pl.program_id(0)
    base = i * (tm * TOP_K)
    n_blocks = buf_ref.shape[0] // EXPERT_BLOCK

    def row_copy(r, k):
        dst = dest_ref[base + r * TOP_K + k]
        return pltpu.make_async_copy(h_ref.at[pl.ds(r, 1), :], buf_ref.at[pl.ds(dst, 1), :], sem)

    def pad_copy(e, j):
        return pltpu.make_async_copy(zrow_ref.at[pl.ds(0, 1), :],
                                     buf_ref.at[pl.ds(pad_start_ref[e] + j, 1), :], zsem)

    def tail_copy(blk):
        row0 = pl.multiple_of(blk * EXPERT_BLOCK, EXPERT_BLOCK)
        return pltpu.make_async_copy(zrow_ref, buf_ref.at[pl.ds(row0, EXPERT_BLOCK), :], tsem)

    @pl.when(i == 0)
    def _():
        zrow_ref[...] = jnp.zeros_like(zrow_ref)

        def per_expert(e, _):
            lax.fori_loop(0, pad_n_ref[e], lambda j, c: (pad_copy(e, j).start(), c)[1], 0)
            return 0
        lax.fori_loop(0, N_EXPERTS, per_expert, 0)
        lax.fori_loop(nu_ref[0], n_blocks, lambda blk, c: (tail_copy(blk).start(), c)[1], 0)

    def issue(r, c):
        for k in range(TOP_K):
            row_copy(r, k).start()
        return c
    lax.fori_loop(0, tm, issue, 0)

    def drain(r, c):
        for k in range(TOP_K):
            row_copy(r, k).wait()
        return c
    lax.fori_loop(0, tm, drain, 0)

    @pl.when(i == 0)
    def _():
        def per_expert(e, _):
            lax.fori_loop(0, pad_n_ref[e], lambda j, c: (pad_copy(e, j).wait(), c)[1], 0)
            return 0
        lax.fori_loop(0, N_EXPERTS, per_expert, 0)
        lax.fori_loop(nu_ref[0], n_blocks, lambda blk, c: (tail_copy(blk).wait(), c)[1], 0)


def _scatter(dest_flat, pad_start, pad_n, n_used, h2, n_rows):
    t, d = h2.shape
    grid_spec = pltpu.PrefetchScalarGridSpec(
        num_scalar_prefetch=4,
        grid=(t // SC_TM,),
        in_specs=[pl.BlockSpec((SC_TM, d), lambda i, *_: (i, 0))],
        out_specs=pl.BlockSpec(memory_space=pl.ANY),
        scratch_shapes=[pltpu.VMEM((EXPERT_BLOCK, d), F32), pltpu.SemaphoreType.DMA, pltpu.SemaphoreType.DMA,
                        pltpu.SemaphoreType.DMA],
    )
    return pl.pallas_call(
        _scatter_kernel,
        grid_spec=grid_spec,
        out_shape=jax.ShapeDtypeStruct((n_rows, d), F32),
        compiler_params=_cparams(("arbitrary",)),
        name="moe_scatter",
    )(dest_flat, pad_start, pad_n, n_used, h2)


def _expert_kernel(pstart_ref, nblk_ref, nu_ref, wg_ref, wu_ref, wd_ref, buf_ref, yb_ref,
                   wg_bf, wu_bf, wd_bf, xbuf, obuf, in_sem, out_sem, tail_sem):
    e = pl.program_id(0)
    nb = nblk_ref[e]
    row0 = pstart_ref[e]
    n_blocks = buf_ref.shape[0] // EXPERT_BLOCK

    def rows(b):
        return pl.ds(pl.multiple_of(row0 + b * EXPERT_BLOCK, EXPERT_BLOCK), EXPERT_BLOCK)

    def x_copy(b, slot):
        return pltpu.make_async_copy(buf_ref.at[rows(b), :], xbuf.at[slot], in_sem.at[slot])

    def o_copy(b, slot):
        return pltpu.make_async_copy(obuf.at[slot], yb_ref.at[rows(b), :], out_sem.at[slot])

    @pl.when(nb > 0)
    def _():
        x_copy(0, 0).start()
        wg_bf[...] = wg_ref[...].astype(BF16)
        wu_bf[...] = wu_ref[...].astype(BF16)
        wd_bf[...] = wd_ref[...].astype(BF16)

        def body(b, c):
            slot = b % 2

            @pl.when(b + 1 < nb)
            def _():
                x_copy(b + 1, 1 - slot).start()

            x_copy(b, slot).wait()

            @pl.when(b >= 2)
            def _():
                o_copy(b - 2, slot).wait()

            xb = xbuf[slot].astype(BF16)
            gte = jnp.dot(xb, wg_bf[...], preferred_element_type=F32)
            up = jnp.dot(xb, wu_bf[...], preferred_element_type=F32)
            act = gte * (1.0 / (1.0 + jnp.exp(-gte))) * up
            obuf[slot] = jnp.dot(act.astype(BF16), wd_bf[...], preferred_element_type=F32)
            o_copy(b, slot).start()
            return c

        lax.fori_loop(0, nb, body, 0)

        @pl.when(nb >= 2)
        def _():
            o_copy(nb - 2, nb % 2).wait()

        o_copy(nb - 1, (nb - 1) % 2).wait()

    @pl.when(e == pl.num_programs(0) - 1)
    def _():
        xbuf[0] = jnp.zeros(xbuf.shape[1:], xbuf.dtype)

        def tail_copy(blk):
            r = pl.ds(pl.multiple_of(blk * EXPERT_BLOCK, EXPERT_BLOCK), EXPERT_BLOCK)
            return pltpu.make_async_copy(xbuf.at[0], yb_ref.at[r, :], tail_sem)

        lax.fori_loop(nu_ref[0], n_blocks, lambda blk, c: (tail_copy(blk).start(), c)[1], 0)
        lax.fori_loop(nu_ref[0], n_blocks, lambda blk, c: (tail_copy(blk).wait(), c)[1], 0)


def _experts(pstarts, nblk, n_used, buf, w_gate, w_up, w_down, layer):
    n_rows, d = buf.shape
    de = w_gate.shape[-1]
    grid_spec = pltpu.PrefetchScalarGridSpec(
        num_scalar_prefetch=3,
        grid=(N_EXPERTS,),
        in_specs=[
            pl.BlockSpec((None, None, d, de), lambda e, *_: (layer, e, 0, 0)),
            pl.BlockSpec((None, None, d, de), lambda e, *_: (layer, e, 0, 0)),
            pl.BlockSpec((None, None, de, d), lambda e, *_: (layer, e, 0, 0)),
            pl.BlockSpec(memory_space=pl.ANY),
        ],
        out_specs=pl.BlockSpec(memory_space=pl.ANY),
        scratch_shapes=[
            pltpu.VMEM((d, de), BF16), pltpu.VMEM((d, de), BF16), pltpu.VMEM((de, d), BF16),
            pltpu.VMEM((2, EXPERT_BLOCK, d), F32), pltpu.VMEM((2, EXPERT_BLOCK, d), F32),
            pltpu.SemaphoreType.DMA((2,)), pltpu.SemaphoreType.DMA((2,)), pltpu.SemaphoreType.DMA,
        ],
    )
    return pl.pallas_call(
        _expert_kernel,
        grid_spec=grid_spec,
        out_shape=jax.ShapeDtypeStruct((n_rows, d), F32),
        compiler_params=_cparams(("arbitrary",)),
        name="moe_experts",
    )(pstarts, nblk, n_used, w_gate, w_up, w_down, buf)


CB_TM = 128


def _combine_kernel(dest_ref, x_ref, route_ref, gt_ref, gf_ref, yb_ref, o_ref, ybuf, sems, *, final_norm):
    tm = x_ref.shape[0]
    i = pl.program_id(0)
    n = pl.num_programs(0)

    def row_copy(tile, slot, r, k):
        src = dest_ref[(tile * tm + r) * TOP_K + k]
        return pltpu.make_async_copy(yb_ref.at[pl.ds(src, 1), :], ybuf.at[slot, k, pl.ds(r, 1), :], sems.at[slot])

    def issue(tile, slot):
        def body(r, c):
            for k in range(TOP_K):
                row_copy(tile, slot, r, k).start()
            return c
        lax.fori_loop(0, tm, body, 0)

    @pl.when(i == 0)
    def _():
        issue(0, 0)

    @pl.when(i + 1 < n)
    def _():
        issue(i + 1, (i + 1) % 2)

    slot = i % 2

    def drain(r, c):
        for k in range(TOP_K):
            row_copy(i, slot, r, k).wait()
        return c
    lax.fori_loop(0, tm, drain, 0)

    route = route_ref[...]
    y = ybuf[slot, 0] * route[:, 2:3] + ybuf[slot, 1] * route[:, 3:4]
    xn = x_ref[...] + gt_ref[0] * y
    if final_norm:
        xn = _rms(xn, gf_ref[...])
    o_ref[...] = xn


def _combine(dest_flat, x2, route, gt, g_final, yb, seq, final_norm):
    t, d = x2.shape
    tpb = seq // CB_TM
    grid_spec = pltpu.PrefetchScalarGridSpec(
        num_scalar_prefetch=1,
        grid=(t // CB_TM,),
        in_specs=[
            pl.BlockSpec((CB_TM, d), lambda i, ds: (i, 0)),
            pl.BlockSpec((CB_TM, LANES), lambda i, ds: (i, 0)),
            pl.BlockSpec((1, 1, d), lambda i, ds: (i // tpb, 0, 0)),
            pl.BlockSpec((1, d), lambda i, ds: (0, 0)),
            pl.BlockSpec(memory_space=pl.ANY),
        ],
        out_specs=pl.BlockSpec((CB_TM, d), lambda i, ds: (i, 0)),
        scratch_shapes=[pltpu.VMEM((2, TOP_K, CB_TM, d), F32), pltpu.SemaphoreType.DMA((2,))],
    )
    return pl.pallas_call(
        functools.partial(_combine_kernel, final_norm=final_norm),
        grid_spec=grid_spec,
        out_shape=jax.ShapeDtypeStruct((t, d), F32),
        compiler_params=_cparams(("arbitrary",)),
        name="moe_combine",
    )(dest_flat, x2, route, gt, g_final, yb)


def _dft_cos_sin(n):
    k = np.arange(n, dtype=np.int64)
    ang = (np.outer(k, k) % n).astype(np.float64) * (2.0 * np.pi / n)
    return np.cos(ang), np.sin(ang)


@functools.lru_cache(maxsize=None)
def _dft_constants(seq, dg):
    cc, sc = _dft_cos_sin(dg)
    cs_chan = np.concatenate([cc, sc], axis=1).astype(np.float32)
    cq, sq = _dft_cos_sin(seq)
    cs_seq = np.concatenate([cq, -sq], axis=1).astype(np.float32)
    return cs_chan.astype(BF16), cs_seq.astype(BF16)


def _moe(x2, g2, sh, sc, gt, w_group, b_group, w_router, b_router, w_gate, w_up, w_down, layer,
         g_final, seq, final_norm):
    t, d = x2.shape
    w_all = jnp.concatenate([w_group, w_router], axis=1)
    w_all = jnp.pad(w_all, ((0, 0), (0, LANES - w_all.shape[1])))
    w_hi = w_all.astype(BF16)
    w_lo = (w_all - w_hi.astype(F32)).astype(BF16)
    whl = jnp.concatenate([w_hi, w_lo], axis=1)
    br = jnp.pad(jnp.concatenate([b_group, b_router]), (0, LANES - N_GROUPS - N_EXPERTS)).reshape(1, LANES)

    h2, route, cnt = _route(x2, g2, sh, sc, whl, br, seq)

    counts = cnt[0, :N_EXPERTS].astype(jnp.int32)
    padded = (counts + EXPERT_BLOCK - 1) // EXPERT_BLOCK * EXPERT_BLOCK
    pends = jnp.cumsum(padded)
    pstarts = pends - padded
    n_assign = t * TOP_K
    n_blocks = (n_assign + N_EXPERTS * (EXPERT_BLOCK - 1) + EXPERT_BLOCK - 1) // EXPERT_BLOCK
    n_rows = n_blocks * EXPERT_BLOCK
    e_idx = route[:, 0:TOP_K].astype(jnp.int32)
    rank = route[:, 4:4 + TOP_K].astype(jnp.int32)
    is_e = e_idx[:, :, None] == jnp.arange(N_EXPERTS, dtype=jnp.int32)
    dest_flat = (jnp.sum(jnp.where(is_e, pstarts, 0), axis=-1) + rank).reshape(n_assign)
    n_used = (pends[-1:] // EXPERT_BLOCK).astype(jnp.int32)

    buf = _scatter(dest_flat, (pstarts + counts).astype(jnp.int32), (padded - counts).astype(jnp.int32), n_used,
                   h2, n_rows)
    yb = _experts(pstarts.astype(jnp.int32), (padded // EXPERT_BLOCK).astype(jnp.int32), n_used, buf,
                  w_gate, w_up, w_down, layer)
    return _combine(dest_flat, x2, route, gt, g_final, yb, seq, final_norm)


def kernel(x, c, g_norm1, g_norm2, w_ada, b_ada, fa_w_in, fa_w_out, sg_w_in, sg_b_in, sg_g_v, sg_w_s, sg_b_s, sg_w_out, w_group, b_group, w_router, b_router, w_gate, w_up, w_down, g_final):
    n_batch, seq, d = x.shape
    depth = w_ada.shape[0]
    t = n_batch * seq
    x2 = x.reshape(t, d)

    mod = _ada(c, w_ada, b_ada)
    cs_chan, cs_seq = _dft_constants(seq, d // F_GROUPS)
    gfin = g_final.reshape(1, d)

    for l in range(depth):
        parts = [mod[l, :n_batch, k * d:(k + 1) * d].reshape(n_batch, 1, d) for k in range(6)]
        sh1, sc1, gt1, sh2, sc2, gt2 = parts
        g1 = g_norm1[l].reshape(1, d)
        j = l // 2
        if l % 2 == 0:
            ab = _fnet_in(x2, g1, sh1, sc1, fa_w_in[j].astype(BF16), jnp.asarray(cs_chan), n_batch, seq)
            x2 = _fnet_seq(jnp.asarray(cs_seq), ab, fa_w_out[j].astype(BF16), x2, gt1, n_batch, seq)
        else:
            x2 = _sgu(x2, g1, sh1, sc1, gt1, sg_w_in[j].astype(BF16), sg_b_in[j].reshape(1, 2 * d),
                      sg_g_v[j].reshape(1, d), sg_w_s[j].astype(BF16), sg_b_s[j].T, sg_w_out[j].astype(BF16), seq)
        x2 = _moe(x2, g_norm2[l].reshape(1, d), sh2, sc2, gt2, w_group[l], b_group[l], w_router[l], b_router[l],
                  w_gate, w_up, w_down, l, gfin, seq, final_norm=(l == depth - 1))
    return x2.reshape(n_batch, seq, d)
```

```python
import functools

import numpy as np
import jax
import jax.numpy as jnp
from jax import lax
from jax.experimental import pallas as pl
from jax.experimental.pallas import tpu as pltpu

F32 = jnp.float32
BF16 = jnp.bfloat16

EPS = 1e-6
F_GROUPS = 4
SGU_HEADS = 8
CHUNK = 128
N_GROUPS = 4
E_PER_GROUP = 8
N_EXPERTS = N_GROUPS * E_PER_GROUP
TOP_K = 2
EXPERT_BLOCK = 128

LANES = 128
SUBLANES = 8
VMEM_LIMIT = 56 * 1024 * 1024
NEG_BIG = -1e30


def _cparams(sem, vmem=VMEM_LIMIT):
    return pltpu.CompilerParams(dimension_semantics=sem, vmem_limit_bytes=vmem)


def _rms(x, g):
    return x * lax.rsqrt(jnp.mean(x * x, axis=-1, keepdims=True) + EPS) * g


ADA_TN = 512
ADA_ROWS = 64


def _ada_kernel(cb_ref, w_ref, b_ref, o_ref, *, n_batch):
    d = w_ref.shape[1]
    tn = w_ref.shape[2]

    def body(i, accs):
        r = pl.multiple_of(i * ADA_ROWS, ADA_ROWS)
        wblk = w_ref[0, pl.ds(r, ADA_ROWS), :]
        out = []
        for b in range(n_batch):
            cv = cb_ref[b, pl.ds(r, ADA_ROWS), :]
            sv = cv / (1.0 + jnp.exp(-cv))
            p = wblk * jnp.tile(sv, (1, tn // LANES))
            out.append(accs[b] + p.reshape(ADA_ROWS // SUBLANES, SUBLANES, tn).sum(axis=0))
        return tuple(out)

    init = tuple(jnp.zeros((SUBLANES, tn), F32) for _ in range(n_batch))
    accs = lax.fori_loop(0, d // ADA_ROWS, body, init)
    rows = [jnp.sum(a, axis=0, keepdims=True) for a in accs]
    rows += [jnp.zeros((1, tn), F32)] * (SUBLANES - n_batch)
    o_ref[0] = jnp.concatenate(rows, axis=0) + b_ref[0]


def _ada(c, w_ada, b_ada):
    n_batch, d = c.shape
    depth, _, n6 = w_ada.shape
    cb = jnp.broadcast_to(c[:, :, None], (n_batch, d, LANES))
    return pl.pallas_call(
        functools.partial(_ada_kernel, n_batch=n_batch),
        grid=(depth, n6 // ADA_TN),
        in_specs=[
            pl.BlockSpec((n_batch, d, LANES), lambda l, j: (0, 0, 0)),
            pl.BlockSpec((1, d, ADA_TN), lambda l, j: (l, 0, j)),
            pl.BlockSpec((1, 1, ADA_TN), lambda l, j: (l, 0, j)),
        ],
        out_specs=pl.BlockSpec((1, SUBLANES, ADA_TN), lambda l, j: (l, 0, j)),
        out_shape=jax.ShapeDtypeStruct((depth, SUBLANES, n6), F32),
        compiler_params=_cparams(("arbitrary", "arbitrary")),
        name="ada_mod",
    )(cb, w_ada, b_ada.reshape(depth, 1, n6))


FN_TM = 256


def _fnet_in_kernel(x_ref, g_ref, sh_ref, sc_ref, win_ref, cs_ref, ab_ref):
    d = x_ref.shape[1]
    dg = d // F_GROUPS
    h = _rms(x_ref[...], g_ref[...]) * (1.0 + sc_ref[0]) + sh_ref[0]
    z = jnp.dot(h.astype(BF16), win_ref[...], preferred_element_type=F32)
    for g in range(F_GROUPS):
        zg = z[:, g * dg:(g + 1) * dg].astype(BF16)
        ab = jnp.dot(zg, cs_ref[...], preferred_element_type=F32)
        ab_ref[0, 0, :, g * dg:(g + 1) * dg] = ab[:, :dg].astype(BF16)
        ab_ref[0, 1, :, g * dg:(g + 1) * dg] = ab[:, dg:].astype(BF16)


def _fnet_in(x2, g, sh, sc, w_in_bf, cs_chan, n_batch, seq):
    t, d = x2.shape
    tpb = seq // FN_TM
    dg = d // F_GROUPS
    return pl.pallas_call(
        _fnet_in_kernel,
        grid=(t // FN_TM,),
        in_specs=[
            pl.BlockSpec((FN_TM, d), lambda i: (i, 0)),
            pl.BlockSpec((1, d), lambda i: (0, 0)),
            pl.BlockSpec((1, 1, d), lambda i: (i // tpb, 0, 0)),
            pl.BlockSpec((1, 1, d), lambda i: (i // tpb, 0, 0)),
            pl.BlockSpec((d, d), lambda i: (0, 0), pipeline_mode=pl.Buffered(1)),
            pl.BlockSpec((dg, 2 * dg), lambda i: (0, 0), pipeline_mode=pl.Buffered(1)),
        ],
        out_specs=pl.BlockSpec((1, 2, FN_TM, d), lambda i: (i // tpb, 0, i % tpb, 0)),
        out_shape=jax.ShapeDtypeStruct((n_batch, 2, seq, d), BF16),
        compiler_params=_cparams(("arbitrary",)),
        name="fnet_in",
    )(x2, g, sh, sc, w_in_bf, cs_chan)


FS_TK = 256


def _fnet_seq_kernel(cs_ref, ab_ref, wout_ref, x_ref, gt_ref, o_ref, *, scale):
    two, s, d = ab_ref.shape[1:]
    ab = ab_ref[0].reshape(two * s, d)
    f = jnp.dot(cs_ref[...], ab, preferred_element_type=F32) * scale
    y = jnp.dot(f.astype(BF16), wout_ref[...], preferred_element_type=F32)
    o_ref[...] = x_ref[...] + gt_ref[0] * y


def _fnet_seq(cs_seq, ab, w_out_bf, x2, gt, n_batch, seq):
    t, d = x2.shape
    tpb = seq // FS_TK
    scale = 1.0 / float(np.sqrt(seq * (d // F_GROUPS)))
    return pl.pallas_call(
        functools.partial(_fnet_seq_kernel, scale=scale),
        grid=(n_batch, tpb),
        in_specs=[
            pl.BlockSpec((FS_TK, 2 * seq), lambda b, k: (k, 0)),
            pl.BlockSpec((1, 2, seq, d), lambda b, k: (b, 0, 0, 0), pipeline_mode=pl.Buffered(1)),
            pl.BlockSpec((d, d), lambda b, k: (0, 0), pipeline_mode=pl.Buffered(1)),
            pl.BlockSpec((FS_TK, d), lambda b, k: (b * tpb + k, 0)),
            pl.BlockSpec((1, 1, d), lambda b, k: (b, 0, 0)),
        ],
        out_specs=pl.BlockSpec((FS_TK, d), lambda b, k: (b * tpb + k, 0)),
        out_shape=jax.ShapeDtypeStruct((t, d), F32),
        compiler_params=_cparams(("arbitrary", "arbitrary")),
        name="fnet_seq",
    )(cs_seq, ab, w_out_bf, x2, gt)


SG_TM = 128


def _gelu_tanh(x):
    c = float(np.sqrt(2.0 / np.pi))
    return x * (0.5 * (1.0 + jnp.tanh(c * (x + 0.044715 * (x * x * x)))))


def _sgu_kernel(x_ref, g_ref, sh_ref, sc_ref, gt_ref, win_ref, bin_ref, gv_ref, ws_ref, bs_ref,
                wout_ref, o_ref, gated_ref):
    tm, d = x_ref.shape
    dh = d // SGU_HEADS
    x = x_ref[...]
    h = _rms(x, g_ref[...]) * (1.0 + sc_ref[0]) + sh_ref[0]
    z = jnp.dot(h.astype(BF16), win_ref[...], preferred_element_type=F32) + bin_ref[...]
    z = _gelu_tanh(z)
    u = z[:, :d]
    v = _rms(z[:, d:], gv_ref[...])
    for c in range(tm // CHUNK):
        rows = slice(c * CHUNK, (c + 1) * CHUNK)
        for hd in range(SGU_HEADS):
            cols = slice(hd * dh, (hd + 1) * dh)
            vc = v[rows, cols].astype(BF16)
            m = jnp.dot(ws_ref[hd], vc, preferred_element_type=F32) + bs_ref[:, hd:hd + 1]
            gated_ref[rows, cols] = (u[rows, cols] * m).astype(BF16)
    y = jnp.dot(gated_ref[...], wout_ref[...], preferred_element_type=F32)
    o_ref[...] = x + gt_ref[0] * y


def _sgu(x2, g, sh, sc, gt, w_in_bf, b_in, g_v, w_s_bf, b_s_t, w_out_bf, seq):
    t, d = x2.shape
    tpb = seq // SG_TM
    const2 = lambda i: (0, 0)
    perb = lambda i: (i // tpb, 0, 0)
    return pl.pallas_call(
        _sgu_kernel,
        grid=(t // SG_TM,),
        in_specs=[
            pl.BlockSpec((SG_TM, d), lambda i: (i, 0)),
            pl.BlockSpec((1, d), const2),
            pl.BlockSpec((1, 1, d), perb),
            pl.BlockSpec((1, 1, d), perb),
            pl.BlockSpec((1, 1, d), perb),
            pl.BlockSpec((d, 2 * d), const2, pipeline_mode=pl.Buffered(1)),
            pl.BlockSpec((1, 2 * d), const2),
            pl.BlockSpec((1, d), const2),
            pl.BlockSpec((SGU_HEADS, CHUNK, CHUNK), lambda i: (0, 0, 0)),
            pl.BlockSpec((CHUNK, SGU_HEADS), const2),
            pl.BlockSpec((d, d), const2, pipeline_mode=pl.Buffered(1)),
        ],
        out_specs=pl.BlockSpec((SG_TM, d), lambda i: (i, 0)),
        out_shape=jax.ShapeDtypeStruct((t, d), F32),
        scratch_shapes=[pltpu.VMEM((SG_TM, d), BF16)],
        compiler_params=_cparams(("arbitrary",)),
        name="sgu_mix",
    )(x2, g, sh, sc, gt, w_in_bf, b_in, g_v, w_s_bf, b_s_t, w_out_bf)


RT_TM = 256


def _route_kernel(x_ref, g_ref, sh_ref, sc_ref, whl_ref, br_ref, h_ref, route_ref, cnt_ref, carry_ref):
    tm = x_ref.shape[0]
    i = pl.program_id(0)

    @pl.when(i == 0)
    def _():
        carry_ref[...] = jnp.zeros_like(carry_ref)

    h = _rms(x_ref[...], g_ref[...]) * (1.0 + sc_ref[0]) + sh_ref[0]
    h_hi = h.astype(BF16)
    pk = h.shape[1] // (2 * LANES)
    bits = pltpu.bitcast(h_hi.astype(F32), jnp.uint32)
    for s in range(pk):
        low = bits[:, s * LANES:(s + 1) * LANES]
        high = bits[:, (s + pk) * LANES:(s + pk + 1) * LANES]
        h_ref[pl.ds(s, tm, stride=pk), :] = (high & jnp.uint32(0xFFFF0000)) | (low >> jnp.uint32(16))
    h_lo = (h - h_hi.astype(F32)).astype(BF16)
    whl = whl_ref[...]
    both = jnp.dot(h_hi, whl, preferred_element_type=F32)
    lo = jnp.dot(h_lo, whl[:, :LANES], preferred_element_type=F32)
    lg = both[:, :LANES] + both[:, LANES:] + lo + br_ref[...]

    lane = lax.broadcasted_iota(jnp.int32, (tm, LANES), 1)
    lanef = lane.astype(F32)
    is_g = lane < N_GROUPS
    gl = jnp.where(is_g, lg, NEG_BIG)
    gmax = jnp.max(gl, axis=1, keepdims=True)
    gidx = jnp.min(jnp.where(gl == gmax, lanef, float(LANES)), axis=1, keepdims=True)
    gsum = jnp.sum(jnp.where(is_g, jnp.exp(gl - gmax), 0.0), axis=1, keepdims=True)
    g_w = 1.0 / gsum
    lo_lane = float(N_GROUPS) + gidx * float(E_PER_GROUP)
    in_grp = (lanef >= lo_lane) & (lanef < lo_lane + float(E_PER_GROUP))
    el = jnp.where(in_grp, lg, NEG_BIG)
    v1 = jnp.max(el, axis=1, keepdims=True)
    i1 = jnp.min(jnp.where(el == v1, lanef, float(LANES)), axis=1, keepdims=True)
    el2 = jnp.where(lanef == i1, NEG_BIG, el)
    v2 = jnp.max(el2, axis=1, keepdims=True)
    i2 = jnp.min(jnp.where(el2 == v2, lanef, float(LANES)), axis=1, keepdims=True)
    p = jnp.exp(v2 - v1)
    w0 = g_w / (1.0 + p)
    w1 = g_w * p / (1.0 + p)
    e0 = i1 - float(N_GROUPS)
    e1 = i2 - float(N_GROUPS)

    oh0 = jnp.where(lanef == e0, 1.0, 0.0)
    oh1 = jnp.where(lanef == e1, 1.0, 0.0)
    rr = lax.broadcasted_iota(jnp.int32, (tm, tm), 0)
    cc = lax.broadcasted_iota(jnp.int32, (tm, tm), 1)
    tri = jnp.where(rr > cc, 1.0, 0.0).astype(BF16)
    pre0 = jnp.dot(tri, oh0.astype(BF16), preferred_element_type=F32)
    pre1 = jnp.dot(tri, oh1.astype(BF16), preferred_element_type=F32)
    carry = carry_ref[0:1, :]
    cnt0 = jnp.sum(oh0, axis=0, keepdims=True)
    cnt1 = jnp.sum(oh1, axis=0, keepdims=True)
    rank0 = jnp.sum(oh0 * (pre0 + carry), axis=1, keepdims=True)
    rank1 = jnp.sum(oh1 * (pre1 + carry + cnt0), axis=1, keepdims=True)
    new_carry = carry + cnt0 + cnt1
    carry_ref[...] = jnp.broadcast_to(new_carry, carry_ref.shape)
    cnt_ref[...] = jnp.broadcast_to(new_carry, cnt_ref.shape)

    route = jnp.where(lane == 0, e0, 0.0)
    route = jnp.where(lane == 1, e1, route)
    route = jnp.where(lane == 2, w0, route)
    route = jnp.where(lane == 3, w1, route)
    route = jnp.where(lane == 4, rank0, route)
    route = jnp.where(lane == 5, rank1, route)
    route_ref[...] = route


def _route(x2, g, sh, sc, whl, br, seq):
    t, d = x2.shape
    tpb = seq // RT_TM
    pk = d // (2 * LANES)
    perb = lambda i: (i // tpb, 0, 0)
    return pl.pallas_call(
        _route_kernel,
        grid=(t // RT_TM,),
        in_specs=[
            pl.BlockSpec((RT_TM, d), lambda i: (i, 0)),
            pl.BlockSpec((1, d), lambda i: (0, 0)),
            pl.BlockSpec((1, 1, d), perb),
            pl.BlockSpec((1, 1, d), perb),
            pl.BlockSpec((d, 2 * LANES), lambda i: (0, 0)),
            pl.BlockSpec((1, LANES), lambda i: (0, 0)),
        ],
        out_specs=[
            pl.BlockSpec((RT_TM * pk, LANES), lambda i: (i, 0)),
            pl.BlockSpec((RT_TM, LANES), lambda i: (i, 0)),
            pl.BlockSpec((SUBLANES, LANES), lambda i: (0, 0)),
        ],
        out_shape=[
            jax.ShapeDtypeStruct((t * pk, LANES), jnp.uint32),
            jax.ShapeDtypeStruct((t, LANES), F32),
            jax.ShapeDtypeStruct((SUBLANES, LANES), F32),
        ],
        scratch_shapes=[pltpu.VMEM((SUBLANES, LANES), F32)],
        compiler_params=_cparams(("arbitrary",)),
        name="moe_route",
    )(x2, g, sh, sc, whl, br)


SC_TM = 512


SC_UNROLL = 8


def _scatter_kernel(dest_ref, pad_start_ref, pad_n_ref, nu_ref, h_ref, buf_ref, zrow_ref, sem, zsem, tsem, *, pk):
    tm = h_ref.shape[0] // pk
    i = pl.program_id(0)
    base = i * (tm * TOP_K)
    blk_rows = EXPERT_BLOCK * pk
    n_blocks = buf_ref.shape[0] // blk_rows

    def tok_rows(tok):
        return pl.ds(pl.multiple_of(tok * pk, pk), pk)

    def row_copy(r, k):
        dst = dest_ref[base + r * TOP_K + k]
        return pltpu.make_async_copy(h_ref.at[tok_rows(r), :], buf_ref.at[tok_rows(dst), :], sem)

    def pad_copy(e, j):
        return pltpu.make_async_copy(zrow_ref.at[pl.ds(0, pk), :],
                                     buf_ref.at[tok_rows(pad_start_ref[e] + j), :], zsem)

    def tail_copy(blk):
        row0 = pl.multiple_of(blk * blk_rows, blk_rows)
        return pltpu.make_async_copy(zrow_ref, buf_ref.at[pl.ds(row0, blk_rows), :], tsem)

    @pl.when(i == 0)
    def _():
        zrow_ref[...] = jnp.zeros_like(zrow_ref)

        def per_expert(e, _):
            lax.fori_loop(0, pad_n_ref[e], lambda j, c: (pad_copy(e, j).start(), c)[1], 0)
            return 0
        lax.fori_loop(0, N_EXPERTS, per_expert, 0)
        lax.fori_loop(nu_ref[0], n_blocks, lambda blk, c: (tail_copy(blk).start(), c)[1], 0)

    def issue(q, c):
        for u in range(SC_UNROLL):
            for k in range(TOP_K):
                row_copy(q * SC_UNROLL + u, k).start()
        return c
    lax.fori_loop(0, tm // SC_UNROLL, issue, 0)

    def drain(q, c):
        for u in range(SC_UNROLL):
            for k in range(TOP_K):
                row_copy(q * SC_UNROLL + u, k).wait()
        return c
    lax.fori_loop(0, tm // SC_UNROLL, drain, 0)

    @pl.when(i == 0)
    def _():
        def per_expert(e, _):
            lax.fori_loop(0, pad_n_ref[e], lambda j, c: (pad_copy(e, j).wait(), c)[1], 0)
            return 0
        lax.fori_loop(0, N_EXPERTS, per_expert, 0)
        lax.fori_loop(nu_ref[0], n_blocks, lambda blk, c: (tail_copy(blk).wait(), c)[1], 0)


def _scatter(dest_flat, pad_start, pad_n, n_used, hp, n_rows, pk):
    t = hp.shape[0] // pk
    grid_spec = pltpu.PrefetchScalarGridSpec(
        num_scalar_prefetch=4,
        grid=(t // SC_TM,),
        in_specs=[pl.BlockSpec((SC_TM * pk, LANES), lambda i, *_: (i, 0))],
        out_specs=pl.BlockSpec(memory_space=pl.ANY),
        scratch_shapes=[pltpu.VMEM((EXPERT_BLOCK * pk, LANES), hp.dtype), pltpu.SemaphoreType.DMA,
                        pltpu.SemaphoreType.DMA, pltpu.SemaphoreType.DMA],
    )
    return pl.pallas_call(
        functools.partial(_scatter_kernel, pk=pk),
        grid_spec=grid_spec,
        out_shape=jax.ShapeDtypeStruct((n_rows * pk, LANES), hp.dtype),
        compiler_params=_cparams(("arbitrary",)),
        name="moe_scatter",
    )(dest_flat, pad_start, pad_n, n_used, hp)


EX_RING = 4


def _expert_kernel(pstart_ref, nblk_ref, nu_ref, wg_ref, wu_ref, wd_ref, buf_ref, yb_ref,
                   wg_bf, wu_bf, wd_bf, xbuf, obuf, in_sem, out_sem, tail_sem, *, pk, fs):
    e = pl.program_id(0)
    nb = nblk_ref[e]
    g0 = pstart_ref[e]
    nu = nu_ref[0]
    xrows = EXPERT_BLOCK * pk
    orows = EXPERT_BLOCK * fs
    n_blocks = yb_ref.shape[0] // orows

    def x_copy(g):
        slot = g % EX_RING
        src = buf_ref.at[pl.ds(pl.multiple_of(g * xrows, xrows), xrows), :]
        return pltpu.make_async_copy(src, xbuf.at[slot], in_sem.at[slot])

    def o_copy(g):
        slot = g % EX_RING
        dst = yb_ref.at[pl.ds(pl.multiple_of(g * orows, orows), orows), :]
        return pltpu.make_async_copy(obuf.at[slot], dst, out_sem.at[slot])

    @pl.when(e == 0)
    def _():
        for k in range(EX_RING - 1):
            @pl.when(k < nu)
            def _():
                x_copy(k).start()

    @pl.when(nb > 0)
    def _():
        wg_bf[...] = wg_ref[...].astype(BF16)
        wu_bf[...] = wu_ref[...].astype(BF16)
        wd_bf[...] = wd_ref[...].astype(BF16)

        def body(g, c):
            @pl.when(g + (EX_RING - 1) < nu)
            def _():
                x_copy(g + (EX_RING - 1)).start()

            x_copy(g).wait()

            @pl.when(g >= EX_RING)
            def _():
                o_copy(g - EX_RING).wait()

            slot = g % EX_RING
            xs = xbuf.at[slot]
            lows, highs = [], []
            for s in range(pk):
                w = xs[pl.ds(s, EXPERT_BLOCK, stride=pk), :]
                lows.append(pltpu.bitcast(w << jnp.uint32(16), F32).astype(BF16))
                highs.append(pltpu.bitcast(w & jnp.uint32(0xFFFF0000), F32).astype(BF16))
            xb = jnp.concatenate(lows + highs, axis=1)
            gte = jnp.dot(xb, wg_bf[...], preferred_element_type=F32)
            up = jnp.dot(xb, wu_bf[...], preferred_element_type=F32)
            act = gte * (1.0 / (1.0 + jnp.exp(-gte))) * up
            y = jnp.dot(act.astype(BF16), wd_bf[...], preferred_element_type=F32)
            os_ = obuf.at[slot]
            for s in range(fs):
                os_[pl.ds(s, EXPERT_BLOCK, stride=fs), :] = y[:, s * LANES:(s + 1) * LANES]
            o_copy(g).start()
            return c

        lax.fori_loop(g0, g0 + nb, body, 0)

    @pl.when(e == pl.num_programs(0) - 1)
    def _():
        lax.fori_loop(jnp.maximum(nu - EX_RING, 0), nu, lambda g, c: (o_copy(g).wait(), c)[1], 0)
        obuf[0] = jnp.zeros(obuf.shape[1:], obuf.dtype)

        def tail_copy(blk):
            r = pl.ds(pl.multiple_of(blk * orows, orows), orows)
            return pltpu.make_async_copy(obuf.at[0], yb_ref.at[r, :], tail_sem)

        lax.fori_loop(nu, n_blocks, lambda blk, c: (tail_copy(blk).start(), c)[1], 0)
        lax.fori_loop(nu, n_blocks, lambda blk, c: (tail_copy(blk).wait(), c)[1], 0)


def _experts(gstarts, nblk, n_used, buf, w_gate, w_up, w_down, layer, n_rows, pk):
    d, de = w_gate.shape[-2:]
    fs = d // LANES
    grid_spec = pltpu.PrefetchScalarGridSpec(
        num_scalar_prefetch=3,
        grid=(N_EXPERTS,),
        in_specs=[
            pl.BlockSpec((None, None, d, de), lambda e, *_: (layer, e, 0, 0)),
            pl.BlockSpec((None, None, d, de), lambda e, *_: (layer, e, 0, 0)),
            pl.BlockSpec((None, None, de, d), lambda e, *_: (layer, e, 0, 0)),
            pl.BlockSpec(memory_space=pl.ANY),
        ],
        out_specs=pl.BlockSpec(memory_space=pl.ANY),
        scratch_shapes=[
            pltpu.VMEM((d, de), BF16), pltpu.VMEM((d, de), BF16), pltpu.VMEM((de, d), BF16),
            pltpu.VMEM((EX_RING, EXPERT_BLOCK * pk, LANES), buf.dtype),
            pltpu.VMEM((EX_RING, EXPERT_BLOCK * fs, LANES), F32),
            pltpu.SemaphoreType.DMA((EX_RING,)), pltpu.SemaphoreType.DMA((EX_RING,)), pltpu.SemaphoreType.DMA,
        ],
    )
    return pl.pallas_call(
        functools.partial(_expert_kernel, pk=pk, fs=fs),
        grid_spec=grid_spec,
        out_shape=jax.ShapeDtypeStruct((n_rows * fs, LANES), F32),
        compiler_params=_cparams(("arbitrary",)),
        name="moe_experts",
    )(gstarts, nblk, n_used, w_gate, w_up, w_down, buf)


CB_TM = 128


def _combine_kernel(dest_ref, x_ref, route_ref, gt_ref, gf_ref, yb_ref, o_ref, ybuf, sems, *, final_norm):
    tm, d = x_ref.shape
    fs = d // LANES
    i = pl.program_id(0)
    n = pl.num_programs(0)

    def row_copy(tile, slot, r, k):
        src = dest_ref[(tile * tm + r) * TOP_K + k]
        return pltpu.make_async_copy(yb_ref.at[pl.ds(pl.multiple_of(src * fs, fs), fs), :],
                                     ybuf.at[slot, k, pl.ds(pl.multiple_of(r * fs, fs), fs), :], sems.at[slot])

    def issue(tile, slot):
        def body(q, c):
            for u in range(SC_UNROLL):
                for k in range(TOP_K):
                    row_copy(tile, slot, q * SC_UNROLL + u, k).start()
            return c
        lax.fori_loop(0, tm // SC_UNROLL, body, 0)

    @pl.when(i == 0)
    def _():
        issue(0, 0)

    @pl.when(i + 1 < n)
    def _():
        issue(i + 1, (i + 1) % 2)

    slot = i % 2

    def drain(q, c):
        for u in range(SC_UNROLL):
            for k in range(TOP_K):
                row_copy(i, slot, q * SC_UNROLL + u, k).wait()
        return c
    lax.fori_loop(0, tm // SC_UNROLL, drain, 0)

    route = route_ref[...]
    ys = []
    for k in range(TOP_K):
        yk = ybuf.at[slot, k]
        ys.append(jnp.concatenate([yk[pl.ds(s, tm, stride=fs), :] for s in range(fs)], axis=1))
    y = ys[0] * route[:, 2:3] + ys[1] * route[:, 3:4]
    xn = x_ref[...] + gt_ref[0] * y
    if final_norm:
        xn = _rms(xn, gf_ref[...])
    o_ref[...] = xn


def _combine(dest_flat, x2, route, gt, g_final, yb, seq, final_norm):
    t, d = x2.shape
    tpb = seq // CB_TM
    grid_spec = pltpu.PrefetchScalarGridSpec(
        num_scalar_prefetch=1,
        grid=(t // CB_TM,),
        in_specs=[
            pl.BlockSpec((CB_TM, d), lambda i, ds: (i, 0)),
            pl.BlockSpec((CB_TM, LANES), lambda i, ds: (i, 0)),
            pl.BlockSpec((1, 1, d), lambda i, ds: (i // tpb, 0, 0)),
            pl.BlockSpec((1, d), lambda i, ds: (0, 0)),
            pl.BlockSpec(memory_space=pl.ANY),
        ],
        out_specs=pl.BlockSpec((CB_TM, d), lambda i, ds: (i, 0)),
        scratch_shapes=[pltpu.VMEM((2, TOP_K, CB_TM * (d // LANES), LANES), F32), pltpu.SemaphoreType.DMA((2,))],
    )
    return pl.pallas_call(
        functools.partial(_combine_kernel, final_norm=final_norm),
        grid_spec=grid_spec,
        out_shape=jax.ShapeDtypeStruct((t, d), F32),
        compiler_params=_cparams(("arbitrary",)),
        name="moe_combine",
    )(dest_flat, x2, route, gt, g_final, yb)


def _dft_cos_sin(n):
    k = np.arange(n, dtype=np.int64)
    ang = (np.outer(k, k) % n).astype(np.float64) * (2.0 * np.pi / n)
    return np.cos(ang), np.sin(ang)


@functools.lru_cache(maxsize=None)
def _dft_constants(seq, dg):
    cc, sc = _dft_cos_sin(dg)
    cs_chan = np.concatenate([cc, sc], axis=1).astype(np.float32)
    cq, sq = _dft_cos_sin(seq)
    cs_seq = np.concatenate([cq, -sq], axis=1).astype(np.float32)
    return cs_chan.astype(BF16), cs_seq.astype(BF16)


def _moe(x2, g2, sh, sc, gt, w_group, b_group, w_router, b_router, w_gate, w_up, w_down, layer,
         g_final, seq, final_norm):
    t, d = x2.shape
    w_all = jnp.concatenate([w_group, w_router], axis=1)
    w_all = jnp.pad(w_all, ((0, 0), (0, LANES - w_all.shape[1])))
    w_hi = w_all.astype(BF16)
    w_lo = (w_all - w_hi.astype(F32)).astype(BF16)
    whl = jnp.concatenate([w_hi, w_lo], axis=1)
    br = jnp.pad(jnp.concatenate([b_group, b_router]), (0, LANES - N_GROUPS - N_EXPERTS)).reshape(1, LANES)

    pk = d // (2 * LANES)
    hp, route, cnt = _route(x2, g2, sh, sc, whl, br, seq)

    counts = cnt[0, :N_EXPERTS].astype(jnp.int32)
    padded = (counts + EXPERT_BLOCK - 1) // EXPERT_BLOCK * EXPERT_BLOCK
    pends = jnp.cumsum(padded)
    pstarts = pends - padded
    n_assign = t * TOP_K
    n_blocks = (n_assign + N_EXPERTS * (EXPERT_BLOCK - 1) + EXPERT_BLOCK - 1) // EXPERT_BLOCK
    n_rows = n_blocks * EXPERT_BLOCK
    e_idx = route[:, 0:TOP_K].astype(jnp.int32)
    rank = route[:, 4:4 + TOP_K].astype(jnp.int32)
    is_e = e_idx[:, :, None] == jnp.arange(N_EXPERTS, dtype=jnp.int32)
    dest_flat = (jnp.sum(jnp.where(is_e, pstarts, 0), axis=-1) + rank).reshape(n_assign)
    n_used = (pends[-1:] // EXPERT_BLOCK).astype(jnp.int32)

    buf = _scatter(dest_flat, (pstarts + counts).astype(jnp.int32), (padded - counts).astype(jnp.int32), n_used,
                   hp, n_rows, pk)
    yb = _experts((pstarts // EXPERT_BLOCK).astype(jnp.int32), (padded // EXPERT_BLOCK).astype(jnp.int32), n_used,
                  buf, w_gate, w_up, w_down, layer, n_rows, pk)
    return _combine(dest_flat, x2, route, gt, g_final, yb, seq, final_norm)


def kernel(x, c, g_norm1, g_norm2, w_ada, b_ada, fa_w_in, fa_w_out, sg_w_in, sg_b_in, sg_g_v, sg_w_s, sg_b_s, sg_w_out, w_group, b_group, w_router, b_router, w_gate, w_up, w_down, g_final):
    n_batch, seq, d = x.shape
    depth = w_ada.shape[0]
    t = n_batch * seq
    x2 = x.reshape(t, d)

    mod = _ada(c, w_ada, b_ada)
    cs_chan, cs_seq = _dft_constants(seq, d // F_GROUPS)
    gfin = g_final.reshape(1, d)

    for l in range(depth):
        parts = [mod[l, :n_batch, k * d:(k + 1) * d].reshape(n_batch, 1, d) for k in range(6)]
        sh1, sc1, gt1, sh2, sc2, gt2 = parts
        g1 = g_norm1[l].reshape(1, d)
        j = l // 2
        if l % 2 == 0:
            ab = _fnet_in(x2, g1, sh1, sc1, fa_w_in[j].astype(BF16), jnp.asarray(cs_chan), n_batch, seq)
            x2 = _fnet_seq(jnp.asarray(cs_seq), ab, fa_w_out[j].astype(BF16), x2, gt1, n_batch, seq)
        else:
            x2 = _sgu(x2, g1, sh1, sc1, gt1, sg_w_in[j].astype(BF16), sg_b_in[j].reshape(1, 2 * d),
                      sg_g_v[j].reshape(1, d), sg_w_s[j].astype(BF16), sg_b_s[j].T, sg_w_out[j].astype(BF16), seq)
        x2 = _moe(x2, g_norm2[l].reshape(1, d), sh2, sc2, gt2, w_group[l], b_group[l], w_router[l], b_router[l],
                  w_gate, w_up, w_down, l, gfin, seq, final_norm=(l == depth - 1))
    return x2.reshape(n_batch, seq, d)
```

```python
import functools

import numpy as np
import jax
import jax.numpy as jnp
from jax import lax
from jax.experimental import pallas as pl
from jax.experimental.pallas import tpu as pltpu

F32 = jnp.float32
BF16 = jnp.bfloat16

EPS = 1e-6
F_GROUPS = 4
SGU_HEADS = 8
CHUNK = 128
N_GROUPS = 4
E_PER_GROUP = 8
N_EXPERTS = N_GROUPS * E_PER_GROUP
TOP_K = 2
EXPERT_BLOCK = 128

LANES = 128
SUBLANES = 8
VMEM_LIMIT = 56 * 1024 * 1024
NEG_BIG = -1e30


def _cparams(sem, vmem=VMEM_LIMIT):
    return pltpu.CompilerParams(dimension_semantics=sem, vmem_limit_bytes=vmem)


def _rms(x, g):
    return x * lax.rsqrt(jnp.mean(x * x, axis=-1, keepdims=True) + EPS) * g


ADA_TN = 512
ADA_ROWS = 64


def _ada_kernel(cb_ref, w_ref, b_ref, o_ref, silu_ref, *, n_batch):
    d = w_ref.shape[1]
    tn = w_ref.shape[2]

    @pl.when((pl.program_id(0) == 0) & (pl.program_id(1) == 0))
    def _():
        cv = cb_ref[...]
        silu_ref[...] = cv / (1.0 + jnp.exp(-cv))

    def body(i, accs):
        r = pl.multiple_of(i * ADA_ROWS, ADA_ROWS)
        wblk = w_ref[0, pl.ds(r, ADA_ROWS), :]
        out = []
        for b in range(n_batch):
            sv = silu_ref[b, pl.ds(r, ADA_ROWS), :]
            p = wblk * jnp.tile(sv, (1, tn // LANES))
            out.append(accs[b] + p.reshape(ADA_ROWS // SUBLANES, SUBLANES, tn).sum(axis=0))
        return tuple(out)

    init = tuple(jnp.zeros((SUBLANES, tn), F32) for _ in range(n_batch))
    accs = lax.fori_loop(0, d // ADA_ROWS, body, init)
    rows = [jnp.sum(a, axis=0, keepdims=True) for a in accs]
    rows += [jnp.zeros((1, tn), F32)] * (SUBLANES - n_batch)
    o_ref[0] = jnp.concatenate(rows, axis=0) + b_ref[0]


def _ada(c, w_ada, b_ada):
    n_batch, d = c.shape
    depth, _, n6 = w_ada.shape
    cb = jnp.broadcast_to(c[:, :, None], (n_batch, d, LANES))
    return pl.pallas_call(
        functools.partial(_ada_kernel, n_batch=n_batch),
        grid=(depth, n6 // ADA_TN),
        in_specs=[
            pl.BlockSpec((n_batch, d, LANES), lambda l, j: (0, 0, 0)),
            pl.BlockSpec((1, d, ADA_TN), lambda l, j: (l, 0, j)),
            pl.BlockSpec((1, 1, ADA_TN), lambda l, j: (l, 0, j)),
        ],
        out_specs=pl.BlockSpec((1, SUBLANES, ADA_TN), lambda l, j: (l, 0, j)),
        out_shape=jax.ShapeDtypeStruct((depth, SUBLANES, n6), F32),
        scratch_shapes=[pltpu.VMEM((n_batch, d, LANES), F32)],
        compiler_params=_cparams(("arbitrary", "arbitrary")),
        name="ada_mod",
    )(cb, w_ada, b_ada.reshape(depth, 1, n6))


FN_TM = 256


def _fnet_in_kernel(x_ref, g_ref, sh_ref, sc_ref, win_ref, cs_ref, ab_ref, xs_ref):
    tm, d = x_ref.shape
    dg = d // F_GROUPS
    half = tm // 2
    nch = d // LANES
    for c in range(nch):
        xs_ref[c] = x_ref[:, c * LANES:(c + 1) * LANES]
    x = jnp.concatenate(
        [jnp.concatenate([xs_ref[c, pl.ds(par, half, stride=2), :] for c in range(nch)], axis=1)
         for par in range(2)], axis=0)
    h = _rms(x, g_ref[...]) * (1.0 + sc_ref[0]) + sh_ref[0]
    z = jnp.dot(h.astype(BF16), win_ref[...], preferred_element_type=F32)
    for g in range(F_GROUPS):
        zg = z[:, g * dg:(g + 1) * dg].astype(BF16)
        ab = jnp.dot(zg, cs_ref[...], preferred_element_type=F32)
        for par in range(2):
            rows = slice(par * half, (par + 1) * half)
            ab_ref[0, par, 0, :, g * dg:(g + 1) * dg] = ab[rows, :dg].astype(BF16)
            ab_ref[0, par, 1, :, g * dg:(g + 1) * dg] = ab[rows, dg:].astype(BF16)


def _fnet_in(x2, g, sh, sc, w_in_bf, cs_chan, n_batch, seq):
    t, d = x2.shape
    tpb = seq // FN_TM
    dg = d // F_GROUPS
    return pl.pallas_call(
        _fnet_in_kernel,
        grid=(t // FN_TM,),
        in_specs=[
            pl.BlockSpec((FN_TM, d), lambda i: (i, 0)),
            pl.BlockSpec((1, d), lambda i: (0, 0)),
            pl.BlockSpec((1, 1, d), lambda i: (i // tpb, 0, 0)),
            pl.BlockSpec((1, 1, d), lambda i: (i // tpb, 0, 0)),
            pl.BlockSpec((d, d), lambda i: (0, 0), pipeline_mode=pl.Buffered(1)),
            pl.BlockSpec((dg, 2 * dg), lambda i: (0, 0), pipeline_mode=pl.Buffered(1)),
        ],
        out_specs=pl.BlockSpec((1, 2, 2, FN_TM // 2, d), lambda i: (i // tpb, 0, 0, i % tpb, 0)),
        out_shape=jax.ShapeDtypeStruct((n_batch, 2, 2, seq // 2, d), BF16),
        scratch_shapes=[pltpu.VMEM((d // LANES, FN_TM, LANES), F32)],
        compiler_params=_cparams(("arbitrary",)),
        name="fnet_in",
    )(x2, g, sh, sc, w_in_bf, cs_chan)


FS_TK = 128


def _fnet_seq_kernel(cs_ref, ab_ref, wout_ref, x_ref, gt_ref, o_ref, *, scale):
    tk = cs_ref.shape[1]
    _, two, hs, d = ab_ref.shape[1:]
    ye = jnp.dot(cs_ref[0], ab_ref[0, 0].reshape(two * hs, d), preferred_element_type=F32)
    yo = jnp.dot(cs_ref[1], ab_ref[0, 1].reshape(two * hs, d), preferred_element_type=F32)
    f = jnp.concatenate([(ye + yo) * scale, (ye - yo) * scale], axis=0).astype(BF16)
    y = jnp.dot(f, wout_ref[...], preferred_element_type=F32)
    o_ref[0, 0] = x_ref[0, 0] + gt_ref[0] * y[:tk]
    o_ref[0, 1] = x_ref[0, 1] + gt_ref[0] * y[tk:]


def _fnet_seq(cs_seq, ab, w_out_bf, x2, gt, n_batch, seq):
    t, d = x2.shape
    hs = seq // 2
    scale = 1.0 / float(np.sqrt(seq * (d // F_GROUPS)))
    x4 = x2.reshape(n_batch, 2, hs, d)
    out = pl.pallas_call(
        functools.partial(_fnet_seq_kernel, scale=scale),
        grid=(n_batch, hs // FS_TK),
        in_specs=[
            pl.BlockSpec((2, FS_TK, seq), lambda b, k: (0, k, 0)),
            pl.BlockSpec((1, 2, 2, hs, d), lambda b, k: (b, 0, 0, 0, 0), pipeline_mode=pl.Buffered(1)),
            pl.BlockSpec((d, d), lambda b, k: (0, 0), pipeline_mode=pl.Buffered(1)),
            pl.BlockSpec((1, 2, FS_TK, d), lambda b, k: (b, 0, k, 0)),
            pl.BlockSpec((1, 1, d), lambda b, k: (b, 0, 0)),
        ],
        out_specs=pl.BlockSpec((1, 2, FS_TK, d), lambda b, k: (b, 0, k, 0)),
        out_shape=jax.ShapeDtypeStruct((n_batch, 2, hs, d), F32),
        compiler_params=_cparams(("arbitrary", "arbitrary")),
        name="fnet_seq",
    )(cs_seq, ab, w_out_bf, x4, gt)
    return out.reshape(t, d)


SG_TM = 128


def _gelu_tanh(x):
    c = float(np.sqrt(2.0 / np.pi))
    return x * (0.5 * (1.0 + jnp.tanh(c * (x + 0.044715 * (x * x * x)))))


def _sgu_kernel(x_ref, g_ref, sh_ref, sc_ref, gt_ref, win_ref, bin_ref, gv_ref, ws_ref, bs_ref,
                wout_ref, o_ref, gated_ref):
    tm, d = x_ref.shape
    dh = d // SGU_HEADS
    x = x_ref[...]
    h = _rms(x, g_ref[...]) * (1.0 + sc_ref[0]) + sh_ref[0]
    z = jnp.dot(h.astype(BF16), win_ref[...], preferred_element_type=F32) + bin_ref[...]
    z = _gelu_tanh(z)
    u = z[:, :d]
    v = _rms(z[:, d:], gv_ref[...])
    for c in range(tm // CHUNK):
        rows = slice(c * CHUNK, (c + 1) * CHUNK)
        for hd in range(SGU_HEADS):
            cols = slice(hd * dh, (hd + 1) * dh)
            vc = v[rows, cols].astype(BF16)
            m = jnp.dot(ws_ref[hd], vc, preferred_element_type=F32) + bs_ref[:, hd:hd + 1]
            gated_ref[rows, cols] = (u[rows, cols] * m).astype(BF16)
    y = jnp.dot(gated_ref[...], wout_ref[...], preferred_element_type=F32)
    o_ref[...] = x + gt_ref[0] * y


def _sgu(x2, g, sh, sc, gt, w_in_bf, b_in, g_v, w_s_bf, b_s_t, w_out_bf, seq):
    t, d = x2.shape
    tpb = seq // SG_TM
    const2 = lambda i: (0, 0)
    perb = lambda i: (i // tpb, 0, 0)
    return pl.pallas_call(
        _sgu_kernel,
        grid=(t // SG_TM,),
        in_specs=[
            pl.BlockSpec((SG_TM, d), lambda i: (i, 0)),
            pl.BlockSpec((1, d), const2),
            pl.BlockSpec((1, 1, d), perb),
            pl.BlockSpec((1, 1, d), perb),
            pl.BlockSpec((1, 1, d), perb),
            pl.BlockSpec((d, 2 * d), const2, pipeline_mode=pl.Buffered(1)),
            pl.BlockSpec((1, 2 * d), const2),
            pl.BlockSpec((1, d), const2),
            pl.BlockSpec((SGU_HEADS, CHUNK, CHUNK), lambda i: (0, 0, 0)),
            pl.BlockSpec((CHUNK, SGU_HEADS), const2),
            pl.BlockSpec((d, d), const2, pipeline_mode=pl.Buffered(1)),
        ],
        out_specs=pl.BlockSpec((SG_TM, d), lambda i: (i, 0)),
        out_shape=jax.ShapeDtypeStruct((t, d), F32),
        scratch_shapes=[pltpu.VMEM((SG_TM, d), BF16)],
        compiler_params=_cparams(("arbitrary",)),
        name="sgu_mix",
    )(x2, g, sh, sc, gt, w_in_bf, b_in, g_v, w_s_bf, b_s_t, w_out_bf)


RT_TM = 256


def _route_kernel(x_ref, g_ref, sh_ref, sc_ref, whl_ref, br_ref, h_ref, route_ref, cnt_ref, carry_ref):
    tm = x_ref.shape[0]
    i = pl.program_id(0)

    @pl.when(i == 0)
    def _():
        carry_ref[...] = jnp.zeros_like(carry_ref)

    h = _rms(x_ref[...], g_ref[...]) * (1.0 + sc_ref[0]) + sh_ref[0]
    h_hi = h.astype(BF16)
    pk = h.shape[1] // (2 * LANES)
    bits = pltpu.bitcast(h_hi.astype(F32), jnp.uint32)
    for s in range(pk):
        low = bits[:, s * LANES:(s + 1) * LANES]
        high = bits[:, (s + pk) * LANES:(s + pk + 1) * LANES]
        h_ref[pl.ds(s, tm, stride=pk), :] = (high & jnp.uint32(0xFFFF0000)) | (low >> jnp.uint32(16))
    h_lo = (h - h_hi.astype(F32)).astype(BF16)
    whl = whl_ref[...]
    both = jnp.dot(h_hi, whl, preferred_element_type=F32)
    lo = jnp.dot(h_lo, whl[:, :LANES], preferred_element_type=F32)
    lg = both[:, :LANES] + both[:, LANES:] + lo + br_ref[...]

    lane = lax.broadcasted_iota(jnp.int32, (tm, LANES), 1)
    lanef = lane.astype(F32)
    is_g = lane < N_GROUPS
    gl = jnp.where(is_g, lg, NEG_BIG)
    gmax = jnp.max(gl, axis=1, keepdims=True)
    gidx = jnp.min(jnp.where(gl == gmax, lanef, float(LANES)), axis=1, keepdims=True)
    gsum = jnp.sum(jnp.where(is_g, jnp.exp(gl - gmax), 0.0), axis=1, keepdims=True)
    g_w = 1.0 / gsum
    lo_lane = float(N_GROUPS) + gidx * float(E_PER_GROUP)
    in_grp = (lanef >= lo_lane) & (lanef < lo_lane + float(E_PER_GROUP))
    el = jnp.where(in_grp, lg, NEG_BIG)
    v1 = jnp.max(el, axis=1, keepdims=True)
    i1 = jnp.min(jnp.where(el == v1, lanef, float(LANES)), axis=1, keepdims=True)
    el2 = jnp.where(lanef == i1, NEG_BIG, el)
    v2 = jnp.max(el2, axis=1, keepdims=True)
    i2 = jnp.min(jnp.where(el2 == v2, lanef, float(LANES)), axis=1, keepdims=True)
    p = jnp.exp(v2 - v1)
    w0 = g_w / (1.0 + p)
    w1 = g_w * p / (1.0 + p)
    e0 = i1 - float(N_GROUPS)
    e1 = i2 - float(N_GROUPS)

    oh0 = jnp.where(lanef == e0, 1.0, 0.0)
    oh1 = jnp.where(lanef == e1, 1.0, 0.0)
    rr = lax.broadcasted_iota(jnp.int32, (tm, tm), 0)
    cc = lax.broadcasted_iota(jnp.int32, (tm, tm), 1)
    tri = jnp.where(rr > cc, 1.0, 0.0).astype(BF16)
    pre0 = jnp.dot(tri, oh0.astype(BF16), preferred_element_type=F32)
    pre1 = jnp.dot(tri, oh1.astype(BF16), preferred_element_type=F32)
    carry = carry_ref[0:1, :]
    cnt0 = jnp.sum(oh0, axis=0, keepdims=True)
    cnt1 = jnp.sum(oh1, axis=0, keepdims=True)
    rank0 = jnp.sum(oh0 * (pre0 + carry), axis=1, keepdims=True)
    rank1 = jnp.sum(oh1 * (pre1 + carry + cnt0), axis=1, keepdims=True)
    new_carry = carry + cnt0 + cnt1
    carry_ref[...] = jnp.broadcast_to(new_carry, carry_ref.shape)
    cnt_ref[...] = jnp.broadcast_to(new_carry, cnt_ref.shape)

    route = jnp.where(lane == 0, e0, 0.0)
    route = jnp.where(lane == 1, e1, route)
    route = jnp.where(lane == 2, w0, route)
    route = jnp.where(lane == 3, w1, route)
    route = jnp.where(lane == 4, rank0, route)
    route = jnp.where(lane == 5, rank1, route)
    route_ref[...] = route


def _route(x2, g, sh, sc, whl, br, seq):
    t, d = x2.shape
    tpb = seq // RT_TM
    pk = d // (2 * LANES)
    perb = lambda i: (i // tpb, 0, 0)
    return pl.pallas_call(
        _route_kernel,
        grid=(t // RT_TM,),
        in_specs=[
            pl.BlockSpec((RT_TM, d), lambda i: (i, 0)),
            pl.BlockSpec((1, d), lambda i: (0, 0)),
            pl.BlockSpec((1, 1, d), perb),
            pl.BlockSpec((1, 1, d), perb),
            pl.BlockSpec((d, 2 * LANES), lambda i: (0, 0)),
            pl.BlockSpec((1, LANES), lambda i: (0, 0)),
        ],
        out_specs=[
            pl.BlockSpec((RT_TM * pk, LANES), lambda i: (i, 0)),
            pl.BlockSpec((RT_TM, LANES), lambda i: (i, 0)),
            pl.BlockSpec((SUBLANES, LANES), lambda i: (0, 0)),
        ],
        out_shape=[
            jax.ShapeDtypeStruct((t * pk, LANES), jnp.uint32),
            jax.ShapeDtypeStruct((t, LANES), F32),
            jax.ShapeDtypeStruct((SUBLANES, LANES), F32),
        ],
        scratch_shapes=[pltpu.VMEM((SUBLANES, LANES), F32)],
        compiler_params=_cparams(("arbitrary",)),
        name="moe_route",
    )(x2, g, sh, sc, whl, br)


SC_TM = 512


SC_UNROLL = 8


def _scatter_kernel(dest_ref, pad_start_ref, pad_n_ref, nu_ref, h_ref, buf_ref, zrow_ref, sem, zsem, tsem, *, pk):
    tm = h_ref.shape[0] // pk
    i = pl.program_id(0)
    base = i * (tm * TOP_K)
    blk_rows = EXPERT_BLOCK * pk
    n_blocks = buf_ref.shape[0] // blk_rows

    def tok_rows(tok):
        return pl.ds(pl.multiple_of(tok * pk, pk), pk)

    def row_copy(r, k):
        dst = dest_ref[base + r * TOP_K + k]
        return pltpu.make_async_copy(h_ref.at[tok_rows(r), :], buf_ref.at[tok_rows(dst), :], sem)

    def pad_copy(e, j):
        return pltpu.make_async_copy(zrow_ref.at[pl.ds(0, pk), :],
                                     buf_ref.at[tok_rows(pad_start_ref[e] + j), :], zsem)

    def tail_copy(blk):
        row0 = pl.multiple_of(blk * blk_rows, blk_rows)
        return pltpu.make_async_copy(zrow_ref, buf_ref.at[pl.ds(row0, blk_rows), :], tsem)

    @pl.when(i == 0)
    def _():
        zrow_ref[...] = jnp.zeros_like(zrow_ref)

        def per_expert(e, _):
            lax.fori_loop(0, pad_n_ref[e], lambda j, c: (pad_copy(e, j).start(), c)[1], 0)
            return 0
        lax.fori_loop(0, N_EXPERTS, per_expert, 0)
        lax.fori_loop(nu_ref[0], n_blocks, lambda blk, c: (tail_copy(blk).start(), c)[1], 0)

    def issue(q, c):
        for u in range(SC_UNROLL):
            for k in range(TOP_K):
                row_copy(q * SC_UNROLL + u, k).start(priority=k)
        return c
    lax.fori_loop(0, tm // SC_UNROLL, issue, 0)

    def drain(q, c):
        for u in range(SC_UNROLL):
            for k in range(TOP_K):
                row_copy(q * SC_UNROLL + u, k).wait()
        return c
    lax.fori_loop(0, tm // SC_UNROLL, drain, 0)

    @pl.when(i == 0)
    def _():
        def per_expert(e, _):
            lax.fori_loop(0, pad_n_ref[e], lambda j, c: (pad_copy(e, j).wait(), c)[1], 0)
            return 0
        lax.fori_loop(0, N_EXPERTS, per_expert, 0)
        lax.fori_loop(nu_ref[0], n_blocks, lambda blk, c: (tail_copy(blk).wait(), c)[1], 0)


def _scatter(dest_flat, pad_start, pad_n, n_used, hp, n_rows, pk):
    t = hp.shape[0] // pk
    grid_spec = pltpu.PrefetchScalarGridSpec(
        num_scalar_prefetch=4,
        grid=(t // SC_TM,),
        in_specs=[pl.BlockSpec((SC_TM * pk, LANES), lambda i, *_: (i, 0))],
        out_specs=pl.BlockSpec(memory_space=pl.ANY),
        scratch_shapes=[pltpu.VMEM((EXPERT_BLOCK * pk, LANES), hp.dtype), pltpu.SemaphoreType.DMA,
                        pltpu.SemaphoreType.DMA, pltpu.SemaphoreType.DMA],
    )
    return pl.pallas_call(
        functools.partial(_scatter_kernel, pk=pk),
        grid_spec=grid_spec,
        out_shape=jax.ShapeDtypeStruct((n_rows * pk, LANES), hp.dtype),
        compiler_params=_cparams(("arbitrary",)),
        name="moe_scatter",
    )(dest_flat, pad_start, pad_n, n_used, hp)


EX_RING = 4


def _expert_kernel(pstart_ref, nblk_ref, nu_ref, wg_ref, wu_ref, wd_ref, buf_ref, yb_ref,
                   wg_bf, wu_bf, wd_bf, xbuf, obuf, in_sem, out_sem, tail_sem, *, pk, fs):
    e = pl.program_id(0)
    nb = nblk_ref[e]
    g0 = pstart_ref[e]
    nu = nu_ref[0]
    xrows = EXPERT_BLOCK * pk
    orows = EXPERT_BLOCK * fs
    n_blocks = yb_ref.shape[0] // orows

    def x_copy(g):
        slot = g % EX_RING
        src = buf_ref.at[pl.ds(pl.multiple_of(g * xrows, xrows), xrows), :]
        return pltpu.make_async_copy(src, xbuf.at[slot], in_sem.at[slot])

    def o_copy(g):
        slot = g % EX_RING
        dst = yb_ref.at[pl.ds(pl.multiple_of(g * orows, orows), orows), :]
        return pltpu.make_async_copy(obuf.at[slot], dst, out_sem.at[slot])

    @pl.when(e == 0)
    def _():
        for k in range(EX_RING - 1):
            @pl.when(k < nu)
            def _():
                x_copy(k).start()

    @pl.when(nb > 0)
    def _():
        wg_bf[...] = wg_ref[...].astype(BF16)
        wu_bf[...] = wu_ref[...].astype(BF16)
        wd_bf[...] = wd_ref[...].astype(BF16)

        def body(g, c):
            @pl.when(g + (EX_RING - 1) < nu)
            def _():
                x_copy(g + (EX_RING - 1)).start()

            x_copy(g).wait()

            @pl.when(g >= EX_RING)
            def _():
                o_copy(g - EX_RING).wait()

            slot = g % EX_RING
            xs = xbuf.at[slot]
            lows, highs = [], []
            for s in range(pk):
                w = xs[pl.ds(s, EXPERT_BLOCK, stride=pk), :]
                lows.append(pltpu.bitcast(w << jnp.uint32(16), F32).astype(BF16))
                highs.append(pltpu.bitcast(w & jnp.uint32(0xFFFF0000), F32).astype(BF16))
            xb = jnp.concatenate(lows + highs, axis=1)
            gte = jnp.dot(xb, wg_bf[...], preferred_element_type=F32)
            up = jnp.dot(xb, wu_bf[...], preferred_element_type=F32)
            act = gte * (1.0 / (1.0 + jnp.exp(-gte))) * up
            y = jnp.dot(act.astype(BF16), wd_bf[...], preferred_element_type=F32)
            os_ = obuf.at[slot]
            for s in range(fs):
                os_[pl.ds(s, EXPERT_BLOCK, stride=fs), :] = y[:, s * LANES:(s + 1) * LANES]
            o_copy(g).start()
            return c

        lax.fori_loop(g0, g0 + nb, body, 0)

    @pl.when(e == pl.num_programs(0) - 1)
    def _():
        lax.fori_loop(jnp.maximum(nu - EX_RING, 0), nu, lambda g, c: (o_copy(g).wait(), c)[1], 0)
        obuf[0] = jnp.zeros(obuf.shape[1:], obuf.dtype)

        def tail_copy(blk):
            r = pl.ds(pl.multiple_of(blk * orows, orows), orows)
            return pltpu.make_async_copy(obuf.at[0], yb_ref.at[r, :], tail_sem)

        lax.fori_loop(nu, n_blocks, lambda blk, c: (tail_copy(blk).start(), c)[1], 0)
        lax.fori_loop(nu, n_blocks, lambda blk, c: (tail_copy(blk).wait(), c)[1], 0)


def _experts(gstarts, nblk, n_used, buf, w_gate, w_up, w_down, layer, n_rows, pk):
    d, de = w_gate.shape[-2:]
    fs = d // LANES
    grid_spec = pltpu.PrefetchScalarGridSpec(
        num_scalar_prefetch=3,
        grid=(N_EXPERTS,),
        in_specs=[
            pl.BlockSpec((None, None, d, de), lambda e, *_: (layer, e, 0, 0)),
            pl.BlockSpec((None, None, d, de), lambda e, *_: (layer, e, 0, 0)),
            pl.BlockSpec((None, None, de, d), lambda e, *_: (layer, e, 0, 0)),
            pl.BlockSpec(memory_space=pl.ANY),
        ],
        out_specs=pl.BlockSpec(memory_space=pl.ANY),
        scratch_shapes=[
            pltpu.VMEM((d, de), BF16), pltpu.VMEM((d, de), BF16), pltpu.VMEM((de, d), BF16),
            pltpu.VMEM((EX_RING, EXPERT_BLOCK * pk, LANES), buf.dtype),
            pltpu.VMEM((EX_RING, EXPERT_BLOCK * fs, LANES), F32),
            pltpu.SemaphoreType.DMA((EX_RING,)), pltpu.SemaphoreType.DMA((EX_RING,)), pltpu.SemaphoreType.DMA,
        ],
    )
    return pl.pallas_call(
        functools.partial(_expert_kernel, pk=pk, fs=fs),
        grid_spec=grid_spec,
        out_shape=jax.ShapeDtypeStruct((n_rows * fs, LANES), F32),
        compiler_params=_cparams(("arbitrary",)),
        name="moe_experts",
    )(gstarts, nblk, n_used, w_gate, w_up, w_down, buf)


CB_TM = 128


def _combine_kernel(dest_ref, x_ref, route_ref, gt_ref, gf_ref, yb_ref, o_ref, ybuf, sems, *, final_norm):
    tm, d = x_ref.shape
    fs = d // LANES
    i = pl.program_id(0)
    n = pl.num_programs(0)

    def row_copy(tile, slot, r, k):
        src = dest_ref[(tile * tm + r) * TOP_K + k]
        return pltpu.make_async_copy(yb_ref.at[pl.ds(pl.multiple_of(src * fs, fs), fs), :],
                                     ybuf.at[slot, k, pl.ds(pl.multiple_of(r * fs, fs), fs), :], sems.at[slot])

    def issue(tile, slot):
        def body(q, c):
            for u in range(SC_UNROLL):
                for k in range(TOP_K):
                    row_copy(tile, slot, q * SC_UNROLL + u, k).start(priority=k)
            return c
        lax.fori_loop(0, tm // SC_UNROLL, body, 0)

    @pl.when(i == 0)
    def _():
        issue(0, 0)

    @pl.when(i + 1 < n)
    def _():
        issue(i + 1, (i + 1) % 2)

    slot = i % 2

    def drain(q, c):
        for u in range(SC_UNROLL):
            for k in range(TOP_K):
                row_copy(i, slot, q * SC_UNROLL + u, k).wait()
        return c
    lax.fori_loop(0, tm // SC_UNROLL, drain, 0)

    route = route_ref[...]
    ys = []
    for k in range(TOP_K):
        yk = ybuf.at[slot, k]
        ys.append(jnp.concatenate([yk[pl.ds(s, tm, stride=fs), :] for s in range(fs)], axis=1))
    y = ys[0] * route[:, 2:3] + ys[1] * route[:, 3:4]
    xn = x_ref[...] + gt_ref[0] * y
    if final_norm:
        xn = _rms(xn, gf_ref[...])
    o_ref[...] = xn


def _combine(dest_flat, x2, route, gt, g_final, yb, seq, final_norm):
    t, d = x2.shape
    tpb = seq // CB_TM
    grid_spec = pltpu.PrefetchScalarGridSpec(
        num_scalar_prefetch=1,
        grid=(t // CB_TM,),
        in_specs=[
            pl.BlockSpec((CB_TM, d), lambda i, ds: (i, 0)),
            pl.BlockSpec((CB_TM, LANES), lambda i, ds: (i, 0)),
            pl.BlockSpec((1, 1, d), lambda i, ds: (i // tpb, 0, 0)),
            pl.BlockSpec((1, d), lambda i, ds: (0, 0)),
            pl.BlockSpec(memory_space=pl.ANY),
        ],
        out_specs=pl.BlockSpec((CB_TM, d), lambda i, ds: (i, 0)),
        scratch_shapes=[pltpu.VMEM((2, TOP_K, CB_TM * (d // LANES), LANES), F32), pltpu.SemaphoreType.DMA((2,))],
    )
    return pl.pallas_call(
        functools.partial(_combine_kernel, final_norm=final_norm),
        grid_spec=grid_spec,
        out_shape=jax.ShapeDtypeStruct((t, d), F32),
        compiler_params=_cparams(("arbitrary",)),
        name="moe_combine",
    )(dest_flat, x2, route, gt, g_final, yb)


def _dft_cos_sin(n):
    k = np.arange(n, dtype=np.int64)
    ang = (np.outer(k, k) % n).astype(np.float64) * (2.0 * np.pi / n)
    return np.cos(ang), np.sin(ang)


@functools.lru_cache(maxsize=None)
def _dft_constants(seq, dg):
    cc, sc = _dft_cos_sin(dg)
    cs_chan = np.concatenate([cc, sc], axis=1).astype(np.float32)
    k = np.arange(seq // 2, dtype=np.int64)[:, None]
    parts = []
    for par in range(2):
        n = 2 * np.arange(seq // 2, dtype=np.int64)[None, :] + par
        ang = ((k * n) % seq).astype(np.float64) * (2.0 * np.pi / seq)
        parts.append(np.concatenate([np.cos(ang), -np.sin(ang)], axis=1))
    cs_seq = np.stack(parts, axis=0).astype(np.float32)
    return cs_chan.astype(BF16), cs_seq.astype(BF16)


def _moe(x2, g2, sh, sc, gt, w_group, b_group, w_router, b_router, w_gate, w_up, w_down, layer,
         g_final, seq, final_norm):
    t, d = x2.shape
    w_all = jnp.concatenate([w_group, w_router], axis=1)
    w_all = jnp.pad(w_all, ((0, 0), (0, LANES - w_all.shape[1])))
    w_hi = w_all.astype(BF16)
    w_lo = (w_all - w_hi.astype(F32)).astype(BF16)
    whl = jnp.concatenate([w_hi, w_lo], axis=1)
    br = jnp.pad(jnp.concatenate([b_group, b_router]), (0, LANES - N_GROUPS - N_EXPERTS)).reshape(1, LANES)

    pk = d // (2 * LANES)
    hp, route, cnt = _route(x2, g2, sh, sc, whl, br, seq)

    counts = cnt[0, :N_EXPERTS].astype(jnp.int32)
    padded = (counts + EXPERT_BLOCK - 1) // EXPERT_BLOCK * EXPERT_BLOCK
    pends = jnp.cumsum(padded)
    pstarts = pends - padded
    n_assign = t * TOP_K
    n_blocks = (n_assign + N_EXPERTS * (EXPERT_BLOCK - 1) + EXPERT_BLOCK - 1) // EXPERT_BLOCK
    n_rows = n_blocks * EXPERT_BLOCK
    e_idx = route[:, 0:TOP_K].astype(jnp.int32)
    rank = route[:, 4:4 + TOP_K].astype(jnp.int32)
    is_e = e_idx[:, :, None] == jnp.arange(N_EXPERTS, dtype=jnp.int32)
    dest_flat = (jnp.sum(jnp.where(is_e, pstarts, 0), axis=-1) + rank).reshape(n_assign)
    n_used = (pends[-1:] // EXPERT_BLOCK).astype(jnp.int32)

    buf = _scatter(dest_flat, (pstarts + counts).astype(jnp.int32), (padded - counts).astype(jnp.int32), n_used,
                   hp, n_rows, pk)
    yb = _experts((pstarts // EXPERT_BLOCK).astype(jnp.int32), (padded // EXPERT_BLOCK).astype(jnp.int32), n_used,
                  buf, w_gate, w_up, w_down, layer, n_rows, pk)
    return _combine(dest_flat, x2, route, gt, g_final, yb, seq, final_norm)


def kernel(x, c, g_norm1, g_norm2, w_ada, b_ada, fa_w_in, fa_w_out, sg_w_in, sg_b_in, sg_g_v, sg_w_s, sg_b_s, sg_w_out, w_group, b_group, w_router, b_router, w_gate, w_up, w_down, g_final):
    n_batch, seq, d = x.shape
    depth = w_ada.shape[0]
    t = n_batch * seq
    x2 = x.reshape(t, d)

    mod = _ada(c, w_ada, b_ada)
    cs_chan, cs_seq = _dft_constants(seq, d // F_GROUPS)
    gfin = g_final.reshape(1, d)

    for l in range(depth):
        parts = [mod[l, :n_batch, k * d:(k + 1) * d].reshape(n_batch, 1, d) for k in range(6)]
        sh1, sc1, gt1, sh2, sc2, gt2 = parts
        g1 = g_norm1[l].reshape(1, d)
        j = l // 2
        if l % 2 == 0:
            ab = _fnet_in(x2, g1, sh1, sc1, fa_w_in[j].astype(BF16), jnp.asarray(cs_chan), n_batch, seq)
            x2 = _fnet_seq(jnp.asarray(cs_seq), ab, fa_w_out[j].astype(BF16), x2, gt1, n_batch, seq)
        else:
            x2 = _sgu(x2, g1, sh1, sc1, gt1, sg_w_in[j].astype(BF16), sg_b_in[j].reshape(1, 2 * d),
                      sg_g_v[j].reshape(1, d), sg_w_s[j].astype(BF16), sg_b_s[j].T, sg_w_out[j].astype(BF16), seq)
        x2 = _moe(x2, g_norm2[l].reshape(1, d), sh2, sc2, gt2, w_group[l], b_group[l], w_router[l], b_router[l],
                  w_gate, w_up, w_down, l, gfin, seq, final_norm=(l == depth - 1))
    return x2.reshape(n_batch, seq, d)
```

```python
import functools

import numpy as np
import jax
import jax.numpy as jnp
from jax import lax
from jax.experimental import pallas as pl
from jax.experimental.pallas import tpu as pltpu

F32 = jnp.float32
BF16 = jnp.bfloat16

EPS = 1e-6
F_GROUPS = 4
SGU_HEADS = 8
CHUNK = 128
N_GROUPS = 4
E_PER_GROUP = 8
N_EXPERTS = N_GROUPS * E_PER_GROUP
TOP_K = 2
EXPERT_BLOCK = 128

LANES = 128
SUBLANES = 8
VMEM_LIMIT = 56 * 1024 * 1024
NEG_BIG = -1e30


def _cparams(sem, vmem=VMEM_LIMIT):
    return pltpu.CompilerParams(dimension_semantics=sem, vmem_limit_bytes=vmem)


def _rms(x, g):
    return x * lax.rsqrt(jnp.mean(x * x, axis=-1, keepdims=True) + EPS) * g


def _store_packed(ref, vals_bf16, n_tok):
    pk = vals_bf16.shape[1] // (2 * LANES)
    bits = pltpu.bitcast(vals_bf16.astype(F32), jnp.uint32)
    for s in range(pk):
        low = bits[:, s * LANES:(s + 1) * LANES]
        high = bits[:, (s + pk) * LANES:(s + pk + 1) * LANES]
        ref[pl.ds(s, n_tok, stride=pk), :] = (high & jnp.uint32(0xFFFF0000)) | (low >> jnp.uint32(16))


def _load_packed(ref, n_tok, pk, dtype):
    lows, highs = [], []
    for s in range(pk):
        w = ref[pl.ds(s, n_tok, stride=pk), :]
        lows.append(pltpu.bitcast(w << jnp.uint32(16), F32).astype(dtype))
        highs.append(pltpu.bitcast(w & jnp.uint32(0xFFFF0000), F32).astype(dtype))
    return jnp.concatenate(lows + highs, axis=1)


ADA_TN = 512
ADA_ROWS = 64


def _ada_kernel(cb_ref, w_ref, b_ref, o_ref, silu_ref, *, n_batch):
    d = w_ref.shape[1]
    tn = w_ref.shape[2]

    @pl.when((pl.program_id(0) == 0) & (pl.program_id(1) == 0))
    def _():
        cv = cb_ref[...]
        silu_ref[...] = cv / (1.0 + jnp.exp(-cv))

    def body(i, accs):
        r = pl.multiple_of(i * ADA_ROWS, ADA_ROWS)
        wblk = w_ref[0, pl.ds(r, ADA_ROWS), :]
        out = []
        for b in range(n_batch):
            sv = silu_ref[b, pl.ds(r, ADA_ROWS), :]
            p = wblk * jnp.tile(sv, (1, tn // LANES))
            out.append(accs[b] + p.reshape(ADA_ROWS // SUBLANES, SUBLANES, tn).sum(axis=0))
        return tuple(out)

    init = tuple(jnp.zeros((SUBLANES, tn), F32) for _ in range(n_batch))
    accs = lax.fori_loop(0, d // ADA_ROWS, body, init)
    rows = [jnp.sum(a, axis=0, keepdims=True) for a in accs]
    rows += [jnp.zeros((1, tn), F32)] * (SUBLANES - n_batch)
    o_ref[0] = jnp.concatenate(rows, axis=0) + b_ref[0]


def _ada(c, w_ada, b_ada):
    n_batch, d = c.shape
    depth, _, n6 = w_ada.shape
    cb = jnp.broadcast_to(c[:, :, None], (n_batch, d, LANES))
    return pl.pallas_call(
        functools.partial(_ada_kernel, n_batch=n_batch),
        grid=(depth, n6 // ADA_TN),
        in_specs=[
            pl.BlockSpec((n_batch, d, LANES), lambda l, j: (0, 0, 0)),
            pl.BlockSpec((1, d, ADA_TN), lambda l, j: (l, 0, j)),
            pl.BlockSpec((1, 1, ADA_TN), lambda l, j: (l, 0, j)),
        ],
        out_specs=pl.BlockSpec((1, SUBLANES, ADA_TN), lambda l, j: (l, 0, j)),
        out_shape=jax.ShapeDtypeStruct((depth, SUBLANES, n6), F32),
        scratch_shapes=[pltpu.VMEM((n_batch, d, LANES), F32)],
        compiler_params=_cparams(("arbitrary", "arbitrary")),
        name="ada_mod",
    )(cb, w_ada, b_ada.reshape(depth, 1, n6))


FN_TM = 256


def _fnet_in_kernel(x_ref, g_ref, sh_ref, sc_ref, win_ref, cs_ref, ab_ref, xs_ref):
    tm, d = x_ref.shape
    dg = d // F_GROUPS
    half = tm // 2
    nch = d // LANES
    for c in range(nch):
        xs_ref[c] = x_ref[:, c * LANES:(c + 1) * LANES]
    x = jnp.concatenate(
        [jnp.concatenate([xs_ref[c, pl.ds(par, half, stride=2), :] for c in range(nch)], axis=1)
         for par in range(2)], axis=0)
    h = _rms(x, g_ref[...]) * (1.0 + sc_ref[0]) + sh_ref[0]
    z = jnp.dot(h.astype(BF16), win_ref[...], preferred_element_type=F32)
    for g in range(F_GROUPS):
        zg = z[:, g * dg:(g + 1) * dg].astype(BF16)
        ab = jnp.dot(zg, cs_ref[...], preferred_element_type=F32)
        for par in range(2):
            rows = slice(par * half, (par + 1) * half)
            ab_ref[0, par, 0, :, g * dg:(g + 1) * dg] = ab[rows, :dg].astype(BF16)
            ab_ref[0, par, 1, :, g * dg:(g + 1) * dg] = ab[rows, dg:].astype(BF16)


def _fnet_in(x2, g, sh, sc, w_in_bf, cs_chan, n_batch, seq):
    t, d = x2.shape
    tpb = seq // FN_TM
    dg = d // F_GROUPS
    return pl.pallas_call(
        _fnet_in_kernel,
        grid=(t // FN_TM,),
        in_specs=[
            pl.BlockSpec((FN_TM, d), lambda i: (i, 0)),
            pl.BlockSpec((1, d), lambda i: (0, 0)),
            pl.BlockSpec((1, 1, d), lambda i: (i // tpb, 0, 0)),
            pl.BlockSpec((1, 1, d), lambda i: (i // tpb, 0, 0)),
            pl.BlockSpec((d, d), lambda i: (0, 0), pipeline_mode=pl.Buffered(1)),
            pl.BlockSpec((dg, 2 * dg), lambda i: (0, 0), pipeline_mode=pl.Buffered(1)),
        ],
        out_specs=pl.BlockSpec((1, 2, 2, FN_TM // 2, d), lambda i: (i // tpb, 0, 0, i % tpb, 0)),
        out_shape=jax.ShapeDtypeStruct((n_batch, 2, 2, seq // 2, d), BF16),
        scratch_shapes=[pltpu.VMEM((d // LANES, FN_TM, LANES), F32)],
        compiler_params=_cparams(("arbitrary",)),
        name="fnet_in",
    )(x2, g, sh, sc, w_in_bf, cs_chan)


FS_TK = 128


def _fnet_seq_kernel(cs_ref, ab_ref, wout_ref, x_ref, gt_ref, o_ref, *, scale):
    tk = cs_ref.shape[1]
    _, two, hs, d = ab_ref.shape[1:]
    ye = jnp.dot(cs_ref[0], ab_ref[0, 0].reshape(two * hs, d), preferred_element_type=F32)
    yo = jnp.dot(cs_ref[1], ab_ref[0, 1].reshape(two * hs, d), preferred_element_type=F32)
    f = jnp.concatenate([(ye + yo) * scale, (ye - yo) * scale], axis=0).astype(BF16)
    y = jnp.dot(f, wout_ref[...], preferred_element_type=F32)
    o_ref[0, 0] = x_ref[0, 0] + gt_ref[0] * y[:tk]
    o_ref[0, 1] = x_ref[0, 1] + gt_ref[0] * y[tk:]


def _fnet_seq(cs_seq, ab, w_out_bf, x2, gt, n_batch, seq):
    t, d = x2.shape
    hs = seq // 2
    scale = 1.0 / float(np.sqrt(seq * (d // F_GROUPS)))
    x4 = x2.reshape(n_batch, 2, hs, d)
    out = pl.pallas_call(
        functools.partial(_fnet_seq_kernel, scale=scale),
        grid=(n_batch, hs // FS_TK),
        in_specs=[
            pl.BlockSpec((2, FS_TK, seq), lambda b, k: (0, k, 0)),
            pl.BlockSpec((1, 2, 2, hs, d), lambda b, k: (b, 0, 0, 0, 0), pipeline_mode=pl.Buffered(1)),
            pl.BlockSpec((d, d), lambda b, k: (0, 0), pipeline_mode=pl.Buffered(1)),
            pl.BlockSpec((1, 2, FS_TK, d), lambda b, k: (b, 0, k, 0)),
            pl.BlockSpec((1, 1, d), lambda b, k: (b, 0, 0)),
        ],
        out_specs=pl.BlockSpec((1, 2, FS_TK, d), lambda b, k: (b, 0, k, 0)),
        out_shape=jax.ShapeDtypeStruct((n_batch, 2, hs, d), F32),
        compiler_params=_cparams(("arbitrary", "arbitrary")),
        name="fnet_seq",
    )(cs_seq, ab, w_out_bf, x4, gt)
    return out.reshape(t, d)


SG_TM = 128


def _gelu_tanh(x):
    c = float(np.sqrt(2.0 / np.pi))
    return x * (0.5 * (1.0 + jnp.tanh(c * (x + 0.044715 * (x * x * x)))))


def _sgu_kernel(x_ref, g_ref, sh_ref, sc_ref, gt_ref, win_ref, bin_ref, gv_ref, ws_ref, bs_ref,
                wout_ref, o_ref, gated_ref):
    tm, d = x_ref.shape
    dh = d // SGU_HEADS
    x = x_ref[...]
    h = _rms(x, g_ref[...]) * (1.0 + sc_ref[0]) + sh_ref[0]
    z = jnp.dot(h.astype(BF16), win_ref[...], preferred_element_type=F32) + bin_ref[...]
    z = _gelu_tanh(z)
    u = z[:, :d]
    v = _rms(z[:, d:], gv_ref[...])
    for c in range(tm // CHUNK):
        rows = slice(c * CHUNK, (c + 1) * CHUNK)
        for hd in range(SGU_HEADS):
            cols = slice(hd * dh, (hd + 1) * dh)
            vc = v[rows, cols].astype(BF16)
            m = jnp.dot(ws_ref[hd], vc, preferred_element_type=F32) + bs_ref[:, hd:hd + 1]
            gated_ref[rows, cols] = (u[rows, cols] * m).astype(BF16)
    y = jnp.dot(gated_ref[...], wout_ref[...], preferred_element_type=F32)
    o_ref[...] = x + gt_ref[0] * y


def _sgu(x2, g, sh, sc, gt, w_in_bf, b_in, g_v, w_s_bf, b_s_t, w_out_bf, seq):
    t, d = x2.shape
    tpb = seq // SG_TM
    const2 = lambda i: (0, 0)
    perb = lambda i: (i // tpb, 0, 0)
    return pl.pallas_call(
        _sgu_kernel,
        grid=(t // SG_TM,),
        in_specs=[
            pl.BlockSpec((SG_TM, d), lambda i: (i, 0)),
            pl.BlockSpec((1, d), const2),
            pl.BlockSpec((1, 1, d), perb),
            pl.BlockSpec((1, 1, d), perb),
            pl.BlockSpec((1, 1, d), perb),
            pl.BlockSpec((d, 2 * d), const2, pipeline_mode=pl.Buffered(1)),
            pl.BlockSpec((1, 2 * d), const2),
            pl.BlockSpec((1, d), const2),
            pl.BlockSpec((SGU_HEADS, CHUNK, CHUNK), lambda i: (0, 0, 0)),
            pl.BlockSpec((CHUNK, SGU_HEADS), const2),
            pl.BlockSpec((d, d), const2, pipeline_mode=pl.Buffered(1)),
        ],
        out_specs=pl.BlockSpec((SG_TM, d), lambda i: (i, 0)),
        out_shape=jax.ShapeDtypeStruct((t, d), F32),
        scratch_shapes=[pltpu.VMEM((SG_TM, d), BF16)],
        compiler_params=_cparams(("arbitrary",)),
        name="sgu_mix",
    )(x2, g, sh, sc, gt, w_in_bf, b_in, g_v, w_s_bf, b_s_t, w_out_bf)


RT_TM = 256


def _route_kernel(x_ref, g_ref, sh_ref, sc_ref, whl_ref, br_ref, h_ref, route_ref, cnt_ref, carry_ref):
    tm = x_ref.shape[0]
    i = pl.program_id(0)

    @pl.when(i == 0)
    def _():
        carry_ref[...] = jnp.zeros_like(carry_ref)

    h = _rms(x_ref[...], g_ref[...]) * (1.0 + sc_ref[0]) + sh_ref[0]
    h_hi = h.astype(BF16)
    _store_packed(h_ref, h_hi, tm)
    h_lo = (h - h_hi.astype(F32)).astype(BF16)
    whl = whl_ref[...]
    both = jnp.dot(h_hi, whl, preferred_element_type=F32)
    lo = jnp.dot(h_lo, whl[:, :LANES], preferred_element_type=F32)
    lg = both[:, :LANES] + both[:, LANES:] + lo + br_ref[...]

    lane = lax.broadcasted_iota(jnp.int32, (tm, LANES), 1)
    lanef = lane.astype(F32)
    is_g = lane < N_GROUPS
    gl = jnp.where(is_g, lg, NEG_BIG)
    gmax = jnp.max(gl, axis=1, keepdims=True)
    gidx = jnp.min(jnp.where(gl == gmax, lanef, float(LANES)), axis=1, keepdims=True)
    gsum = jnp.sum(jnp.where(is_g, jnp.exp(gl - gmax), 0.0), axis=1, keepdims=True)
    g_w = 1.0 / gsum
    lo_lane = float(N_GROUPS) + gidx * float(E_PER_GROUP)
    in_grp = (lanef >= lo_lane) & (lanef < lo_lane + float(E_PER_GROUP))
    el = jnp.where(in_grp, lg, NEG_BIG)
    v1 = jnp.max(el, axis=1, keepdims=True)
    i1 = jnp.min(jnp.where(el == v1, lanef, float(LANES)), axis=1, keepdims=True)
    el2 = jnp.where(lanef == i1, NEG_BIG, el)
    v2 = jnp.max(el2, axis=1, keepdims=True)
    i2 = jnp.min(jnp.where(el2 == v2, lanef, float(LANES)), axis=1, keepdims=True)
    p = jnp.exp(v2 - v1)
    w0 = g_w / (1.0 + p)
    w1 = g_w * p / (1.0 + p)
    e0 = i1 - float(N_GROUPS)
    e1 = i2 - float(N_GROUPS)

    oh0 = jnp.where(lanef == e0, 1.0, 0.0)
    oh1 = jnp.where(lanef == e1, 1.0, 0.0)
    rr = lax.broadcasted_iota(jnp.int32, (tm, tm), 0)
    cc = lax.broadcasted_iota(jnp.int32, (tm, tm), 1)
    tri = jnp.where(rr > cc, 1.0, 0.0).astype(BF16)
    pre0 = jnp.dot(tri, oh0.astype(BF16), preferred_element_type=F32)
    pre1 = jnp.dot(tri, oh1.astype(BF16), preferred_element_type=F32)
    carry = carry_ref[0:1, :]
    cnt0 = jnp.sum(oh0, axis=0, keepdims=True)
    cnt1 = jnp.sum(oh1, axis=0, keepdims=True)
    rank0 = jnp.sum(oh0 * (pre0 + carry), axis=1, keepdims=True)
    rank1 = jnp.sum(oh1 * (pre1 + carry + cnt0), axis=1, keepdims=True)
    new_carry = carry + cnt0 + cnt1
    carry_ref[...] = jnp.broadcast_to(new_carry, carry_ref.shape)
    cnt_ref[...] = jnp.broadcast_to(new_carry, cnt_ref.shape)

    route = jnp.where(lane == 0, e0, 0.0)
    route = jnp.where(lane == 1, e1, route)
    route = jnp.where(lane == 2, w0, route)
    route = jnp.where(lane == 3, w1, route)
    route = jnp.where(lane == 4, rank0, route)
    route = jnp.where(lane == 5, rank1, route)
    route_ref[...] = route


def _route(x2, g, sh, sc, whl, br, seq):
    t, d = x2.shape
    tpb = seq // RT_TM
    pk = d // (2 * LANES)
    perb = lambda i: (i // tpb, 0, 0)
    return pl.pallas_call(
        _route_kernel,
        grid=(t // RT_TM,),
        in_specs=[
            pl.BlockSpec((RT_TM, d), lambda i: (i, 0)),
            pl.BlockSpec((1, d), lambda i: (0, 0)),
            pl.BlockSpec((1, 1, d), perb),
            pl.BlockSpec((1, 1, d), perb),
            pl.BlockSpec((d, 2 * LANES), lambda i: (0, 0)),
            pl.BlockSpec((1, LANES), lambda i: (0, 0)),
        ],
        out_specs=[
            pl.BlockSpec((RT_TM * pk, LANES), lambda i: (i, 0)),
            pl.BlockSpec((RT_TM, LANES), lambda i: (i, 0)),
            pl.BlockSpec((SUBLANES, LANES), lambda i: (0, 0)),
        ],
        out_shape=[
            jax.ShapeDtypeStruct((t * pk, LANES), jnp.uint32),
            jax.ShapeDtypeStruct((t, LANES), F32),
            jax.ShapeDtypeStruct((SUBLANES, LANES), F32),
        ],
        scratch_shapes=[pltpu.VMEM((SUBLANES, LANES), F32)],
        compiler_params=_cparams(("arbitrary",)),
        name="moe_route",
    )(x2, g, sh, sc, whl, br)


SC_TM = 512


SC_UNROLL = 8


def _scatter_kernel(dest_ref, pad_start_ref, pad_n_ref, nu_ref, h_ref, buf_ref, zrow_ref, sem, zsem, tsem, *, pk):
    tm = h_ref.shape[0] // pk
    i = pl.program_id(0)
    base = i * (tm * TOP_K)
    blk_rows = EXPERT_BLOCK * pk
    n_blocks = buf_ref.shape[0] // blk_rows

    def tok_rows(tok):
        return pl.ds(pl.multiple_of(tok * pk, pk), pk)

    def row_copy(r, k):
        dst = dest_ref[base + r * TOP_K + k]
        return pltpu.make_async_copy(h_ref.at[tok_rows(r), :], buf_ref.at[tok_rows(dst), :], sem)

    def pad_copy(e, j):
        return pltpu.make_async_copy(zrow_ref.at[pl.ds(0, pk), :],
                                     buf_ref.at[tok_rows(pad_start_ref[e] + j), :], zsem)

    def tail_copy(blk):
        row0 = pl.multiple_of(blk * blk_rows, blk_rows)
        return pltpu.make_async_copy(zrow_ref, buf_ref.at[pl.ds(row0, blk_rows), :], tsem)

    @pl.when(i == 0)
    def _():
        zrow_ref[...] = jnp.zeros_like(zrow_ref)

        def per_expert(e, _):
            lax.fori_loop(0, pad_n_ref[e], lambda j, c: (pad_copy(e, j).start(), c)[1], 0)
            return 0
        lax.fori_loop(0, N_EXPERTS, per_expert, 0)
        lax.fori_loop(nu_ref[0], n_blocks, lambda blk, c: (tail_copy(blk).start(), c)[1], 0)

    def issue(q, c):
        for u in range(SC_UNROLL):
            for k in range(TOP_K):
                row_copy(q * SC_UNROLL + u, k).start(priority=k)
        return c
    lax.fori_loop(0, tm // SC_UNROLL, issue, 0)

    def drain(q, c):
        for u in range(SC_UNROLL):
            for k in range(TOP_K):
                row_copy(q * SC_UNROLL + u, k).wait()
        return c
    lax.fori_loop(0, tm // SC_UNROLL, drain, 0)

    @pl.when(i == 0)
    def _():
        def per_expert(e, _):
            lax.fori_loop(0, pad_n_ref[e], lambda j, c: (pad_copy(e, j).wait(), c)[1], 0)
            return 0
        lax.fori_loop(0, N_EXPERTS, per_expert, 0)
        lax.fori_loop(nu_ref[0], n_blocks, lambda blk, c: (tail_copy(blk).wait(), c)[1], 0)


def _scatter(dest_flat, pad_start, pad_n, n_used, hp, n_rows, pk):
    t = hp.shape[0] // pk
    grid_spec = pltpu.PrefetchScalarGridSpec(
        num_scalar_prefetch=4,
        grid=(t // SC_TM,),
        in_specs=[pl.BlockSpec((SC_TM * pk, LANES), lambda i, *_: (i, 0))],
        out_specs=pl.BlockSpec(memory_space=pl.ANY),
        scratch_shapes=[pltpu.VMEM((EXPERT_BLOCK * pk, LANES), hp.dtype), pltpu.SemaphoreType.DMA,
                        pltpu.SemaphoreType.DMA, pltpu.SemaphoreType.DMA],
    )
    return pl.pallas_call(
        functools.partial(_scatter_kernel, pk=pk),
        grid_spec=grid_spec,
        out_shape=jax.ShapeDtypeStruct((n_rows * pk, LANES), hp.dtype),
        compiler_params=_cparams(("arbitrary",)),
        name="moe_scatter",
    )(dest_flat, pad_start, pad_n, n_used, hp)


EX_RING = 4
W_PARTS = 4


def _expert_kernel(pstart_ref, nblk_ref, nu_ref, wg_hbm, wu_hbm, wd_hbm, buf_ref, yb_ref,
                   wg_st, wu_st, wd_st, wg_bf, wu_bf, wd_bf, xbuf, obuf, w_sem, in_sem, out_sem, tail_sem,
                   *, layer, pk):
    e = pl.program_id(0)
    ne = pl.num_programs(0)
    nb = nblk_ref[e]
    g0 = pstart_ref[e]
    nu = nu_ref[0]
    xrows = EXPERT_BLOCK * pk
    orows = EXPERT_BLOCK * pk
    n_blocks = yb_ref.shape[0] // orows
    cur = e % 2
    nxt = (e + 1) % 2

    def w_copies(ex, slot):
        return [pltpu.make_async_copy(hbm.at[layer, ex], st.at[slot], w_sem.at[slot])
                for hbm, st in ((wg_hbm, wg_st), (wu_hbm, wu_st), (wd_hbm, wd_st))]

    def cast_part(slot, q):
        for st, bf in ((wg_st, wg_bf), (wu_st, wu_bf), (wd_st, wd_bf)):
            n = st.shape[1] // W_PARTS
            r = pl.ds(pl.multiple_of(q * n, n), n)
            bf[slot, r, :] = st[slot, r, :].astype(BF16)

    @pl.when(e == 0)
    def _():
        for c in w_copies(0, 0):
            c.start()
        for c in w_copies(1, 1):
            c.start()
        for c in w_copies(0, 0):
            c.wait()
        for q in range(W_PARTS):
            cast_part(0, q)
        for c in w_copies(2, 0):
            c.start()

    @pl.when((e >= 1) & (e + 2 < ne))
    def _():
        for c in w_copies(e + 2, cur):
            c.start()

    @pl.when(e + 1 < ne)
    def _():
        for c in w_copies(e + 1, nxt):
            c.wait()

    def x_copy(g):
        slot = g % EX_RING
        src = buf_ref.at[pl.ds(pl.multiple_of(g * xrows, xrows), xrows), :]
        return pltpu.make_async_copy(src, xbuf.at[slot], in_sem.at[slot])

    def o_copy(g):
        slot = g % EX_RING
        dst = yb_ref.at[pl.ds(pl.multiple_of(g * orows, orows), orows), :]
        return pltpu.make_async_copy(obuf.at[slot], dst, out_sem.at[slot])

    @pl.when(e == 0)
    def _():
        for k in range(EX_RING - 1):
            @pl.when(k < nu)
            def _():
                x_copy(k).start()

    def body(g, c):
        @pl.when(g + (EX_RING - 1) < nu)
        def _():
            x_copy(g + (EX_RING - 1)).start()

        x_copy(g).wait()

        @pl.when(g >= EX_RING)
        def _():
            o_copy(g - EX_RING).wait()

        slot = g % EX_RING
        xb = _load_packed(xbuf.at[slot], EXPERT_BLOCK, pk, BF16)
        gte = jnp.dot(xb, wg_bf[cur], preferred_element_type=F32)
        up = jnp.dot(xb, wu_bf[cur], preferred_element_type=F32)
        act = gte * (1.0 / (1.0 + jnp.exp(-gte))) * up
        y = jnp.dot(act.astype(BF16), wd_bf[cur], preferred_element_type=F32)
        _store_packed(obuf.at[slot], y.astype(BF16), EXPERT_BLOCK)
        o_copy(g).start()
        cast_part(nxt, jnp.minimum(g - g0, W_PARTS - 1))
        return c

    lax.fori_loop(g0, g0 + nb, body, 0)
    lax.fori_loop(jnp.minimum(nb, W_PARTS), W_PARTS, lambda q, c: (cast_part(nxt, q), c)[1], 0)

    @pl.when(e == ne - 1)
    def _():
        lax.fori_loop(jnp.maximum(nu - EX_RING, 0), nu, lambda g, c: (o_copy(g).wait(), c)[1], 0)
        obuf[0] = jnp.zeros(obuf.shape[1:], obuf.dtype)

        def tail_copy(blk):
            r = pl.ds(pl.multiple_of(blk * orows, orows), orows)
            return pltpu.make_async_copy(obuf.at[0], yb_ref.at[r, :], tail_sem)

        lax.fori_loop(nu, n_blocks, lambda blk, c: (tail_copy(blk).start(), c)[1], 0)
        lax.fori_loop(nu, n_blocks, lambda blk, c: (tail_copy(blk).wait(), c)[1], 0)


def _experts(gstarts, nblk, n_used, buf, w_gate, w_up, w_down, layer, n_rows, pk):
    d, de = w_gate.shape[-2:]
    assert N_EXPERTS >= 3 and d % W_PARTS == 0 and de % W_PARTS == 0
    hbm = pl.BlockSpec(memory_space=pl.ANY)
    grid_spec = pltpu.PrefetchScalarGridSpec(
        num_scalar_prefetch=3,
        grid=(N_EXPERTS,),
        in_specs=[hbm, hbm, hbm, hbm],
        out_specs=hbm,
        scratch_shapes=[
            pltpu.VMEM((2, d, de), F32), pltpu.VMEM((2, d, de), F32), pltpu.VMEM((2, de, d), F32),
            pltpu.VMEM((2, d, de), BF16), pltpu.VMEM((2, d, de), BF16), pltpu.VMEM((2, de, d), BF16),
            pltpu.VMEM((EX_RING, EXPERT_BLOCK * pk, LANES), buf.dtype),
            pltpu.VMEM((EX_RING, EXPERT_BLOCK * pk, LANES), buf.dtype),
            pltpu.SemaphoreType.DMA((2,)), pltpu.SemaphoreType.DMA((EX_RING,)), pltpu.SemaphoreType.DMA((EX_RING,)),
            pltpu.SemaphoreType.DMA,
        ],
    )
    return pl.pallas_call(
        functools.partial(_expert_kernel, layer=layer, pk=pk),
        grid_spec=grid_spec,
        out_shape=jax.ShapeDtypeStruct((n_rows * pk, LANES), buf.dtype),
        compiler_params=_cparams(("arbitrary",)),
        name="moe_experts",
    )(gstarts, nblk, n_used, w_gate, w_up, w_down, buf)


CB_TM = 128


def _combine_kernel(dest_ref, x_ref, route_ref, gt_ref, gf_ref, yb_ref, o_ref, ybuf, sems, *, final_norm):
    tm, d = x_ref.shape
    fs = d // (2 * LANES)
    i = pl.program_id(0)
    n = pl.num_programs(0)

    def row_copy(tile, slot, r, k):
        src = dest_ref[(tile * tm + r) * TOP_K + k]
        return pltpu.make_async_copy(yb_ref.at[pl.ds(pl.multiple_of(src * fs, fs), fs), :],
                                     ybuf.at[slot, k, pl.ds(pl.multiple_of(r * fs, fs), fs), :], sems.at[slot])

    def issue(tile, slot):
        def body(q, c):
            for u in range(SC_UNROLL):
                for k in range(TOP_K):
                    row_copy(tile, slot, q * SC_UNROLL + u, k).start(priority=k)
            return c
        lax.fori_loop(0, tm // SC_UNROLL, body, 0)

    @pl.when(i == 0)
    def _():
        issue(0, 0)

    @pl.when(i + 1 < n)
    def _():
        issue(i + 1, (i + 1) % 2)

    slot = i % 2

    def drain(q, c):
        for u in range(SC_UNROLL):
            for k in range(TOP_K):
                row_copy(i, slot, q * SC_UNROLL + u, k).wait()
        return c
    lax.fori_loop(0, tm // SC_UNROLL, drain, 0)

    route = route_ref[...]
    ys = [_load_packed(ybuf.at[slot, k], tm, fs, F32) for k in range(TOP_K)]
    y = ys[0] * route[:, 2:3] + ys[1] * route[:, 3:4]
    xn = x_ref[...] + gt_ref[0] * y
    if final_norm:
        xn = _rms(xn, gf_ref[...])
    o_ref[...] = xn


def _combine(dest_flat, x2, route, gt, g_final, yb, seq, final_norm):
    t, d = x2.shape
    tpb = seq // CB_TM
    grid_spec = pltpu.PrefetchScalarGridSpec(
        num_scalar_prefetch=1,
        grid=(t // CB_TM,),
        in_specs=[
            pl.BlockSpec((CB_TM, d), lambda i, ds: (i, 0)),
            pl.BlockSpec((CB_TM, LANES), lambda i, ds: (i, 0)),
            pl.BlockSpec((1, 1, d), lambda i, ds: (i // tpb, 0, 0)),
            pl.BlockSpec((1, d), lambda i, ds: (0, 0)),
            pl.BlockSpec(memory_space=pl.ANY),
        ],
        out_specs=pl.BlockSpec((CB_TM, d), lambda i, ds: (i, 0)),
        scratch_shapes=[pltpu.VMEM((2, TOP_K, CB_TM * (d // (2 * LANES)), LANES), yb.dtype),
                        pltpu.SemaphoreType.DMA((2,))],
    )
    return pl.pallas_call(
        functools.partial(_combine_kernel, final_norm=final_norm),
        grid_spec=grid_spec,
        out_shape=jax.ShapeDtypeStruct((t, d), F32),
        compiler_params=_cparams(("arbitrary",)),
        name="moe_combine",
    )(dest_flat, x2, route, gt, g_final, yb)


def _dft_cos_sin(n):
    k = np.arange(n, dtype=np.int64)
    ang = (np.outer(k, k) % n).astype(np.float64) * (2.0 * np.pi / n)
    return np.cos(ang), np.sin(ang)


@functools.lru_cache(maxsize=None)
def _dft_constants(seq, dg):
    cc, sc = _dft_cos_sin(dg)
    cs_chan = np.concatenate([cc, sc], axis=1).astype(np.float32)
    k = np.arange(seq // 2, dtype=np.int64)[:, None]
    parts = []
    for par in range(2):
        n = 2 * np.arange(seq // 2, dtype=np.int64)[None, :] + par
        ang = ((k * n) % seq).astype(np.float64) * (2.0 * np.pi / seq)
        parts.append(np.concatenate([np.cos(ang), -np.sin(ang)], axis=1))
    cs_seq = np.stack(parts, axis=0).astype(np.float32)
    return cs_chan.astype(BF16), cs_seq.astype(BF16)


def _moe(x2, g2, sh, sc, gt, w_group, b_group, w_router, b_router, w_gate, w_up, w_down, layer,
         g_final, seq, final_norm):
    t, d = x2.shape
    w_all = jnp.concatenate([w_group, w_router], axis=1)
    w_all = jnp.pad(w_all, ((0, 0), (0, LANES - w_all.shape[1])))
    w_hi = w_all.astype(BF16)
    w_lo = (w_all - w_hi.astype(F32)).astype(BF16)
    whl = jnp.concatenate([w_hi, w_lo], axis=1)
    br = jnp.pad(jnp.concatenate([b_group, b_router]), (0, LANES - N_GROUPS - N_EXPERTS)).reshape(1, LANES)

    pk = d // (2 * LANES)
    hp, route, cnt = _route(x2, g2, sh, sc, whl, br, seq)

    counts = cnt[0, :N_EXPERTS].astype(jnp.int32)
    padded = (counts + EXPERT_BLOCK - 1) // EXPERT_BLOCK * EXPERT_BLOCK
    pends = jnp.cumsum(padded)
    pstarts = pends - padded
    n_assign = t * TOP_K
    n_blocks = (n_assign + N_EXPERTS * (EXPERT_BLOCK - 1) + EXPERT_BLOCK - 1) // EXPERT_BLOCK
    n_rows = n_blocks * EXPERT_BLOCK
    e_idx = route[:, 0:TOP_K].astype(jnp.int32)
    rank = route[:, 4:4 + TOP_K].astype(jnp.int32)
    is_e = e_idx[:, :, None] == jnp.arange(N_EXPERTS, dtype=jnp.int32)
    dest_flat = (jnp.sum(jnp.where(is_e, pstarts, 0), axis=-1) + rank).reshape(n_assign)
    n_used = (pends[-1:] // EXPERT_BLOCK).astype(jnp.int32)

    buf = _scatter(dest_flat, (pstarts + counts).astype(jnp.int32), (padded - counts).astype(jnp.int32), n_used,
                   hp, n_rows, pk)
    yb = _experts((pstarts // EXPERT_BLOCK).astype(jnp.int32), (padded // EXPERT_BLOCK).astype(jnp.int32), n_used,
                  buf, w_gate, w_up, w_down, layer, n_rows, pk)
    return _combine(dest_flat, x2, route, gt, g_final, yb, seq, final_norm)


def kernel(x, c, g_norm1, g_norm2, w_ada, b_ada, fa_w_in, fa_w_out, sg_w_in, sg_b_in, sg_g_v, sg_w_s, sg_b_s, sg_w_out, w_group, b_group, w_router, b_router, w_gate, w_up, w_down, g_final):
    n_batch, seq, d = x.shape
    depth = w_ada.shape[0]
    t = n_batch * seq
    x2 = x.reshape(t, d)

    mod = _ada(c, w_ada, b_ada)
    cs_chan, cs_seq = _dft_constants(seq, d // F_GROUPS)
    gfin = g_final.reshape(1, d)

    for l in range(depth):
        parts = [mod[l, :n_batch, k * d:(k + 1) * d].reshape(n_batch, 1, d) for k in range(6)]
        sh1, sc1, gt1, sh2, sc2, gt2 = parts
        g1 = g_norm1[l].reshape(1, d)
        j = l // 2
        if l % 2 == 0:
            ab = _fnet_in(x2, g1, sh1, sc1, fa_w_in[j].astype(BF16), jnp.asarray(cs_chan), n_batch, seq)
            x2 = _fnet_seq(jnp.asarray(cs_seq), ab, fa_w_out[j].astype(BF16), x2, gt1, n_batch, seq)
        else:
            x2 = _sgu(x2, g1, sh1, sc1, gt1, sg_w_in[j].astype(BF16), sg_b_in[j].reshape(1, 2 * d),
                      sg_g_v[j].reshape(1, d), sg_w_s[j].astype(BF16), sg_b_s[j].T, sg_w_out[j].astype(BF16), seq)
        x2 = _moe(x2, g_norm2[l].reshape(1, d), sh2, sc2, gt2, w_group[l], b_group[l], w_router[l], b_router[l],
                  w_gate, w_up, w_down, l, gfin, seq, final_norm=(l == depth - 1))
    return x2.reshape(n_batch, seq, d)
```

```python
import functools

import numpy as np
import jax
import jax.numpy as jnp
from jax import lax
from jax.experimental import pallas as pl
from jax.experimental.pallas import tpu as pltpu

F32 = jnp.float32
BF16 = jnp.bfloat16

EPS = 1e-6
F_GROUPS = 4
SGU_HEADS = 8
CHUNK = 128
N_GROUPS = 4
E_PER_GROUP = 8
N_EXPERTS = N_GROUPS * E_PER_GROUP
TOP_K = 2
EXPERT_BLOCK = 256

LANES = 128
SUBLANES = 8
VMEM_LIMIT = 56 * 1024 * 1024
NEG_BIG = -1e30


def _cparams(sem, vmem=VMEM_LIMIT):
    return pltpu.CompilerParams(dimension_semantics=sem, vmem_limit_bytes=vmem)


def _rms(x, g):
    return x * lax.rsqrt(jnp.mean(x * x, axis=-1, keepdims=True) + EPS) * g


def _store_packed(ref, vals_bf16, n_tok):
    pk = vals_bf16.shape[1] // (2 * LANES)
    bits = pltpu.bitcast(vals_bf16.astype(F32), jnp.uint32)
    for s in range(pk):
        low = bits[:, s * LANES:(s + 1) * LANES]
        high = bits[:, (s + pk) * LANES:(s + pk + 1) * LANES]
        ref[pl.ds(s, n_tok, stride=pk), :] = (high & jnp.uint32(0xFFFF0000)) | (low >> jnp.uint32(16))


def _load_packed(ref, n_tok, pk, dtype):
    lows, highs = [], []
    for s in range(pk):
        w = ref[pl.ds(s, n_tok, stride=pk), :]
        lows.append(pltpu.bitcast(w << jnp.uint32(16), F32).astype(dtype))
        highs.append(pltpu.bitcast(w & jnp.uint32(0xFFFF0000), F32).astype(dtype))
    return jnp.concatenate(lows + highs, axis=1)


ADA_TN = 512
ADA_ROWS = 64


def _ada_kernel(cb_ref, w_ref, b_ref, o_ref, silu_ref, *, n_batch):
    d = w_ref.shape[1]
    tn = w_ref.shape[2]

    @pl.when((pl.program_id(0) == 0) & (pl.program_id(1) == 0))
    def _():
        cv = cb_ref[...]
        silu_ref[...] = cv / (1.0 + jnp.exp(-cv))

    def body(i, accs):
        r = pl.multiple_of(i * ADA_ROWS, ADA_ROWS)
        wblk = w_ref[0, pl.ds(r, ADA_ROWS), :]
        out = []
        for b in range(n_batch):
            sv = silu_ref[b, pl.ds(r, ADA_ROWS), :]
            p = wblk * jnp.tile(sv, (1, tn // LANES))
            out.append(accs[b] + p.reshape(ADA_ROWS // SUBLANES, SUBLANES, tn).sum(axis=0))
        return tuple(out)

    init = tuple(jnp.zeros((SUBLANES, tn), F32) for _ in range(n_batch))
    accs = lax.fori_loop(0, d // ADA_ROWS, body, init)
    rows = [jnp.sum(a, axis=0, keepdims=True) for a in accs]
    rows += [jnp.zeros((1, tn), F32)] * (SUBLANES - n_batch)
    o_ref[0] = jnp.concatenate(rows, axis=0) + b_ref[0]


def _ada(c, w_ada, b_ada):
    n_batch, d = c.shape
    depth, _, n6 = w_ada.shape
    cb = jnp.broadcast_to(c[:, :, None], (n_batch, d, LANES))
    return pl.pallas_call(
        functools.partial(_ada_kernel, n_batch=n_batch),
        grid=(depth, n6 // ADA_TN),
        in_specs=[
            pl.BlockSpec((n_batch, d, LANES), lambda l, j: (0, 0, 0)),
            pl.BlockSpec((1, d, ADA_TN), lambda l, j: (l, 0, j)),
            pl.BlockSpec((1, 1, ADA_TN), lambda l, j: (l, 0, j)),
        ],
        out_specs=pl.BlockSpec((1, SUBLANES, ADA_TN), lambda l, j: (l, 0, j)),
        out_shape=jax.ShapeDtypeStruct((depth, SUBLANES, n6), F32),
        scratch_shapes=[pltpu.VMEM((n_batch, d, LANES), F32)],
        compiler_params=_cparams(("arbitrary", "arbitrary")),
        name="ada_mod",
    )(cb, w_ada, b_ada.reshape(depth, 1, n6))


FN_TM = 256


def _fnet_in_kernel(x_ref, g_ref, sh_ref, sc_ref, win_ref, cs_ref, ab_ref, xs_ref):
    tm, d = x_ref.shape
    dg = d // F_GROUPS
    half = tm // 2
    nch = d // LANES
    for c in range(nch):
        xs_ref[c] = x_ref[:, c * LANES:(c + 1) * LANES]
    x = jnp.concatenate(
        [jnp.concatenate([xs_ref[c, pl.ds(par, half, stride=2), :] for c in range(nch)], axis=1)
         for par in range(2)], axis=0)
    h = _rms(x, g_ref[...]) * (1.0 + sc_ref[0]) + sh_ref[0]
    z = jnp.dot(h.astype(BF16), win_ref[...], preferred_element_type=F32)
    for g in range(F_GROUPS):
        zg = z[:, g * dg:(g + 1) * dg].astype(BF16)
        ab = jnp.dot(zg, cs_ref[...], preferred_element_type=F32)
        for par in range(2):
            rows = slice(par * half, (par + 1) * half)
            ab_ref[0, par, 0, :, g * dg:(g + 1) * dg] = ab[rows, :dg].astype(BF16)
            ab_ref[0, par, 1, :, g * dg:(g + 1) * dg] = ab[rows, dg:].astype(BF16)


def _fnet_in(x2, g, sh, sc, w_in_bf, cs_chan, n_batch, seq):
    t, d = x2.shape
    tpb = seq // FN_TM
    dg = d // F_GROUPS
    return pl.pallas_call(
        _fnet_in_kernel,
        grid=(t // FN_TM,),
        in_specs=[
            pl.BlockSpec((FN_TM, d), lambda i: (i, 0)),
            pl.BlockSpec((1, d), lambda i: (0, 0)),
            pl.BlockSpec((1, 1, d), lambda i: (i // tpb, 0, 0)),
            pl.BlockSpec((1, 1, d), lambda i: (i // tpb, 0, 0)),
            pl.BlockSpec((d, d), lambda i: (0, 0), pipeline_mode=pl.Buffered(1)),
            pl.BlockSpec((dg, 2 * dg), lambda i: (0, 0), pipeline_mode=pl.Buffered(1)),
        ],
        out_specs=pl.BlockSpec((1, 2, 2, FN_TM // 2, d), lambda i: (i // tpb, 0, 0, i % tpb, 0)),
        out_shape=jax.ShapeDtypeStruct((n_batch, 2, 2, seq // 2, d), BF16),
        scratch_shapes=[pltpu.VMEM((d // LANES, FN_TM, LANES), F32)],
        compiler_params=_cparams(("arbitrary",)),
        name="fnet_in",
    )(x2, g, sh, sc, w_in_bf, cs_chan)


FS_TK = 128


def _fnet_seq_kernel(cs_ref, ab_ref, wout_ref, x_ref, gt_ref, o_ref, *, scale):
    tk = cs_ref.shape[1]
    _, two, hs, d = ab_ref.shape[1:]
    ye = jnp.dot(cs_ref[0], ab_ref[0, 0].reshape(two * hs, d), preferred_element_type=F32)
    yo = jnp.dot(cs_ref[1], ab_ref[0, 1].reshape(two * hs, d), preferred_element_type=F32)
    f = jnp.concatenate([(ye + yo) * scale, (ye - yo) * scale], axis=0).astype(BF16)
    y = jnp.dot(f, wout_ref[...], preferred_element_type=F32)
    o_ref[0, 0] = x_ref[0, 0] + gt_ref[0] * y[:tk]
    o_ref[0, 1] = x_ref[0, 1] + gt_ref[0] * y[tk:]


def _fnet_seq(cs_seq, ab, w_out_bf, x2, gt, n_batch, seq):
    t, d = x2.shape
    hs = seq // 2
    scale = 1.0 / float(np.sqrt(seq * (d // F_GROUPS)))
    x4 = x2.reshape(n_batch, 2, hs, d)
    out = pl.pallas_call(
        functools.partial(_fnet_seq_kernel, scale=scale),
        grid=(n_batch, hs // FS_TK),
        in_specs=[
            pl.BlockSpec((2, FS_TK, seq), lambda b, k: (0, k, 0)),
            pl.BlockSpec((1, 2, 2, hs, d), lambda b, k: (b, 0, 0, 0, 0), pipeline_mode=pl.Buffered(1)),
            pl.BlockSpec((d, d), lambda b, k: (0, 0), pipeline_mode=pl.Buffered(1)),
            pl.BlockSpec((1, 2, FS_TK, d), lambda b, k: (b, 0, k, 0)),
            pl.BlockSpec((1, 1, d), lambda b, k: (b, 0, 0)),
        ],
        out_specs=pl.BlockSpec((1, 2, FS_TK, d), lambda b, k: (b, 0, k, 0)),
        out_shape=jax.ShapeDtypeStruct((n_batch, 2, hs, d), F32),
        compiler_params=_cparams(("arbitrary", "arbitrary")),
        name="fnet_seq",
    )(cs_seq, ab, w_out_bf, x4, gt)
    return out.reshape(t, d)


SG_TM = 128


def _gelu_tanh(x):
    c = float(np.sqrt(2.0 / np.pi))
    return x * (0.5 * (1.0 + jnp.tanh(c * (x + 0.044715 * (x * x * x)))))


def _sgu_kernel(x_ref, g_ref, sh_ref, sc_ref, gt_ref, win_ref, bin_ref, gv_ref, ws_ref, bs_ref,
                wout_ref, o_ref, gated_ref):
    tm, d = x_ref.shape
    dh = d // SGU_HEADS
    x = x_ref[...]
    h = _rms(x, g_ref[...]) * (1.0 + sc_ref[0]) + sh_ref[0]
    z = jnp.dot(h.astype(BF16), win_ref[...], preferred_element_type=F32) + bin_ref[...]
    z = _gelu_tanh(z)
    u = z[:, :d]
    v = _rms(z[:, d:], gv_ref[...])
    for c in range(tm // CHUNK):
        rows = slice(c * CHUNK, (c + 1) * CHUNK)
        for hd in range(SGU_HEADS):
            cols = slice(hd * dh, (hd + 1) * dh)
            vc = v[rows, cols].astype(BF16)
            m = jnp.dot(ws_ref[hd], vc, preferred_element_type=F32) + bs_ref[:, hd:hd + 1]
            gated_ref[rows, cols] = (u[rows, cols] * m).astype(BF16)
    y = jnp.dot(gated_ref[...], wout_ref[...], preferred_element_type=F32)
    o_ref[...] = x + gt_ref[0] * y


def _sgu(x2, g, sh, sc, gt, w_in_bf, b_in, g_v, w_s_bf, b_s_t, w_out_bf, seq):
    t, d = x2.shape
    tpb = seq // SG_TM
    const2 = lambda i: (0, 0)
    perb = lambda i: (i // tpb, 0, 0)
    return pl.pallas_call(
        _sgu_kernel,
        grid=(t // SG_TM,),
        in_specs=[
            pl.BlockSpec((SG_TM, d), lambda i: (i, 0)),
            pl.BlockSpec((1, d), const2),
            pl.BlockSpec((1, 1, d), perb),
            pl.BlockSpec((1, 1, d), perb),
            pl.BlockSpec((1, 1, d), perb),
            pl.BlockSpec((d, 2 * d), const2, pipeline_mode=pl.Buffered(1)),
            pl.BlockSpec((1, 2 * d), const2),
            pl.BlockSpec((1, d), const2),
            pl.BlockSpec((SGU_HEADS, CHUNK, CHUNK), lambda i: (0, 0, 0)),
            pl.BlockSpec((CHUNK, SGU_HEADS), const2),
            pl.BlockSpec((d, d), const2, pipeline_mode=pl.Buffered(1)),
        ],
        out_specs=pl.BlockSpec((SG_TM, d), lambda i: (i, 0)),
        out_shape=jax.ShapeDtypeStruct((t, d), F32),
        scratch_shapes=[pltpu.VMEM((SG_TM, d), BF16)],
        compiler_params=_cparams(("arbitrary",)),
        name="sgu_mix",
    )(x2, g, sh, sc, gt, w_in_bf, b_in, g_v, w_s_bf, b_s_t, w_out_bf)


RT_TM = 256


def _route_kernel(x_ref, g_ref, sh_ref, sc_ref, whl_ref, br_ref, h_ref, route_ref, cnt_ref, carry_ref):
    tm = x_ref.shape[0]
    i = pl.program_id(0)

    @pl.when(i == 0)
    def _():
        carry_ref[...] = jnp.zeros_like(carry_ref)

    h = _rms(x_ref[...], g_ref[...]) * (1.0 + sc_ref[0]) + sh_ref[0]
    h_hi = h.astype(BF16)
    _store_packed(h_ref, h_hi, tm)
    h_lo = (h - h_hi.astype(F32)).astype(BF16)
    whl = whl_ref[...]
    both = jnp.dot(h_hi, whl, preferred_element_type=F32)
    lo = jnp.dot(h_lo, whl[:, :LANES], preferred_element_type=F32)
    lg = both[:, :LANES] + both[:, LANES:] + lo + br_ref[...]

    lane = lax.broadcasted_iota(jnp.int32, (tm, LANES), 1)
    lanef = lane.astype(F32)
    is_g = lane < N_GROUPS
    gl = jnp.where(is_g, lg, NEG_BIG)
    gmax = jnp.max(gl, axis=1, keepdims=True)
    gidx = jnp.min(jnp.where(gl == gmax, lanef, float(LANES)), axis=1, keepdims=True)
    gsum = jnp.sum(jnp.where(is_g, jnp.exp(gl - gmax), 0.0), axis=1, keepdims=True)
    g_w = 1.0 / gsum
    lo_lane = float(N_GROUPS) + gidx * float(E_PER_GROUP)
    in_grp = (lanef >= lo_lane) & (lanef < lo_lane + float(E_PER_GROUP))
    el = jnp.where(in_grp, lg, NEG_BIG)
    v1 = jnp.max(el, axis=1, keepdims=True)
    i1 = jnp.min(jnp.where(el == v1, lanef, float(LANES)), axis=1, keepdims=True)
    el2 = jnp.where(lanef == i1, NEG_BIG, el)
    v2 = jnp.max(el2, axis=1, keepdims=True)
    i2 = jnp.min(jnp.where(el2 == v2, lanef, float(LANES)), axis=1, keepdims=True)
    p = jnp.exp(v2 - v1)
    w0 = g_w / (1.0 + p)
    w1 = g_w * p / (1.0 + p)
    e0 = i1 - float(N_GROUPS)
    e1 = i2 - float(N_GROUPS)

    oh0 = jnp.where(lanef == e0, 1.0, 0.0)
    oh1 = jnp.where(lanef == e1, 1.0, 0.0)
    rr = lax.broadcasted_iota(jnp.int32, (tm, tm), 0)
    cc = lax.broadcasted_iota(jnp.int32, (tm, tm), 1)
    tri = jnp.where(rr > cc, 1.0, 0.0).astype(BF16)
    pre0 = jnp.dot(tri, oh0.astype(BF16), preferred_element_type=F32)
    pre1 = jnp.dot(tri, oh1.astype(BF16), preferred_element_type=F32)
    carry = carry_ref[0:1, :]
    cnt0 = jnp.sum(oh0, axis=0, keepdims=True)
    cnt1 = jnp.sum(oh1, axis=0, keepdims=True)
    rank0 = jnp.sum(oh0 * (pre0 + carry), axis=1, keepdims=True)
    rank1 = jnp.sum(oh1 * (pre1 + carry + cnt0), axis=1, keepdims=True)
    new_carry = carry + cnt0 + cnt1
    carry_ref[...] = jnp.broadcast_to(new_carry, carry_ref.shape)
    cnt_ref[...] = jnp.broadcast_to(new_carry, cnt_ref.shape)

    route = jnp.where(lane == 0, e0, 0.0)
    route = jnp.where(lane == 1, e1, route)
    route = jnp.where(lane == 2, w0, route)
    route = jnp.where(lane == 3, w1, route)
    route = jnp.where(lane == 4, rank0, route)
    route = jnp.where(lane == 5, rank1, route)
    route_ref[...] = route


def _route(x2, g, sh, sc, whl, br, seq):
    t, d = x2.shape
    tpb = seq // RT_TM
    pk = d // (2 * LANES)
    perb = lambda i: (i // tpb, 0, 0)
    return pl.pallas_call(
        _route_kernel,
        grid=(t // RT_TM,),
        in_specs=[
            pl.BlockSpec((RT_TM, d), lambda i: (i, 0)),
            pl.BlockSpec((1, d), lambda i: (0, 0)),
            pl.BlockSpec((1, 1, d), perb),
            pl.BlockSpec((1, 1, d), perb),
            pl.BlockSpec((d, 2 * LANES), lambda i: (0, 0)),
            pl.BlockSpec((1, LANES), lambda i: (0, 0)),
        ],
        out_specs=[
            pl.BlockSpec((RT_TM * pk, LANES), lambda i: (i, 0)),
            pl.BlockSpec((RT_TM, LANES), lambda i: (i, 0)),
            pl.BlockSpec((SUBLANES, LANES), lambda i: (0, 0)),
        ],
        out_shape=[
            jax.ShapeDtypeStruct((t * pk, LANES), jnp.uint32),
            jax.ShapeDtypeStruct((t, LANES), F32),
            jax.ShapeDtypeStruct((SUBLANES, LANES), F32),
        ],
        scratch_shapes=[pltpu.VMEM((SUBLANES, LANES), F32)],
        compiler_params=_cparams(("arbitrary",)),
        name="moe_route",
    )(x2, g, sh, sc, whl, br)


SC_TM = 512


SC_UNROLL = 8


def _scatter_kernel(dest_ref, pad_start_ref, pad_n_ref, nu_ref, h_ref, buf_ref, zrow_ref, sem, zsem, tsem, *, pk):
    tm = h_ref.shape[0] // pk
    i = pl.program_id(0)
    base = i * (tm * TOP_K)
    blk_rows = EXPERT_BLOCK * pk
    n_blocks = buf_ref.shape[0] // blk_rows

    def tok_rows(tok):
        return pl.ds(pl.multiple_of(tok * pk, pk), pk)

    def row_copy(r, k):
        dst = dest_ref[base + r * TOP_K + k]
        return pltpu.make_async_copy(h_ref.at[tok_rows(r), :], buf_ref.at[tok_rows(dst), :], sem)

    def pad_fill(e, do):
        n = pad_n_ref[e]
        start = pad_start_ref[e]
        size = EXPERT_BLOCK // 2
        while size >= 1:
            has = (n & size) != 0
            cp = pltpu.make_async_copy(zrow_ref.at[pl.ds(0, size * pk), :],
                                       buf_ref.at[pl.ds(pl.multiple_of(start * pk, pk), size * pk), :], zsem)

            @pl.when(has)
            def _():
                do(cp)

            start = start + jnp.where(has, size, 0)
            size //= 2

    def tail_copy(blk):
        row0 = pl.multiple_of(blk * blk_rows, blk_rows)
        return pltpu.make_async_copy(zrow_ref, buf_ref.at[pl.ds(row0, blk_rows), :], tsem)

    @pl.when(i == 0)
    def _():
        zrow_ref[...] = jnp.zeros_like(zrow_ref)

        def per_expert(e, c):
            pad_fill(e, lambda cp: cp.start())
            return c
        lax.fori_loop(0, N_EXPERTS, per_expert, 0)
        lax.fori_loop(nu_ref[0], n_blocks, lambda blk, c: (tail_copy(blk).start(), c)[1], 0)

    def issue(q, c):
        for u in range(SC_UNROLL):
            for k in range(TOP_K):
                row_copy(q * SC_UNROLL + u, k).start(priority=k)
        return c
    lax.fori_loop(0, tm // SC_UNROLL, issue, 0)

    def drain(q, c):
        for u in range(SC_UNROLL):
            for k in range(TOP_K):
                row_copy(q * SC_UNROLL + u, k).wait()
        return c
    lax.fori_loop(0, tm // SC_UNROLL, drain, 0)

    @pl.when(i == 0)
    def _():
        def per_expert(e, c):
            pad_fill(e, lambda cp: cp.wait())
            return c
        lax.fori_loop(0, N_EXPERTS, per_expert, 0)
        lax.fori_loop(nu_ref[0], n_blocks, lambda blk, c: (tail_copy(blk).wait(), c)[1], 0)


def _scatter(dest_flat, pad_start, pad_n, n_used, hp, n_rows, pk):
    t = hp.shape[0] // pk
    grid_spec = pltpu.PrefetchScalarGridSpec(
        num_scalar_prefetch=4,
        grid=(t // SC_TM,),
        in_specs=[pl.BlockSpec((SC_TM * pk, LANES), lambda i, *_: (i, 0))],
        out_specs=pl.BlockSpec(memory_space=pl.ANY),
        scratch_shapes=[pltpu.VMEM((EXPERT_BLOCK * pk, LANES), hp.dtype), pltpu.SemaphoreType.DMA,
                        pltpu.SemaphoreType.DMA, pltpu.SemaphoreType.DMA],
    )
    return pl.pallas_call(
        functools.partial(_scatter_kernel, pk=pk),
        grid_spec=grid_spec,
        out_shape=jax.ShapeDtypeStruct((n_rows * pk, LANES), hp.dtype),
        compiler_params=_cparams(("arbitrary",)),
        name="moe_scatter",
    )(dest_flat, pad_start, pad_n, n_used, hp)


EX_RING = 4
W_PARTS = 2


def _expert_kernel(pstart_ref, nblk_ref, nu_ref, wg_hbm, wu_hbm, wd_hbm, buf_ref, yb_ref,
                   wg_st, wu_st, wd_st, wg_bf, wu_bf, wd_bf, xbuf, obuf, w_sem, in_sem, out_sem, tail_sem,
                   *, layer, pk):
    e = pl.program_id(0)
    ne = pl.num_programs(0)
    nb = nblk_ref[e]
    g0 = pstart_ref[e]
    nu = nu_ref[0]
    xrows = EXPERT_BLOCK * pk
    orows = EXPERT_BLOCK * pk
    n_blocks = yb_ref.shape[0] // orows
    cur = e % 2
    nxt = (e + 1) % 2

    def w_copies(ex, slot):
        return [pltpu.make_async_copy(hbm.at[layer, ex], st.at[slot], w_sem.at[slot])
                for hbm, st in ((wg_hbm, wg_st), (wu_hbm, wu_st), (wd_hbm, wd_st))]

    def cast_part(slot, q):
        for st, bf in ((wg_st, wg_bf), (wu_st, wu_bf), (wd_st, wd_bf)):
            n = st.shape[1] // W_PARTS
            r = pl.ds(pl.multiple_of(q * n, n), n)
            bf[slot, r, :] = st[slot, r, :].astype(BF16)

    @pl.when(e == 0)
    def _():
        for c in w_copies(0, 0):
            c.start()
        for c in w_copies(1, 1):
            c.start()
        for c in w_copies(0, 0):
            c.wait()
        for q in range(W_PARTS):
            cast_part(0, q)
        for c in w_copies(2, 0):
            c.start()

    @pl.when((e >= 1) & (e + 2 < ne))
    def _():
        for c in w_copies(e + 2, cur):
            c.start()

    @pl.when(e + 1 < ne)
    def _():
        for c in w_copies(e + 1, nxt):
            c.wait()

    def x_copy(g):
        slot = g % EX_RING
        src = buf_ref.at[pl.ds(pl.multiple_of(g * xrows, xrows), xrows), :]
        return pltpu.make_async_copy(src, xbuf.at[slot], in_sem.at[slot])

    def o_copy(g):
        slot = g % EX_RING
        dst = yb_ref.at[pl.ds(pl.multiple_of(g * orows, orows), orows), :]
        return pltpu.make_async_copy(obuf.at[slot], dst, out_sem.at[slot])

    @pl.when(e == 0)
    def _():
        for k in range(EX_RING - 1):
            @pl.when(k < nu)
            def _():
                x_copy(k).start()

    def body(g, c):
        @pl.when(g + (EX_RING - 1) < nu)
        def _():
            x_copy(g + (EX_RING - 1)).start()

        x_copy(g).wait()

        @pl.when(g >= EX_RING)
        def _():
            o_copy(g - EX_RING).wait()

        slot = g % EX_RING
        xb = _load_packed(xbuf.at[slot], EXPERT_BLOCK, pk, BF16)
        gte = jnp.dot(xb, wg_bf[cur], preferred_element_type=F32)
        up = jnp.dot(xb, wu_bf[cur], preferred_element_type=F32)
        act = gte * (1.0 / (1.0 + jnp.exp(-gte))) * up
        y = jnp.dot(act.astype(BF16), wd_bf[cur], preferred_element_type=F32)
        _store_packed(obuf.at[slot], y.astype(BF16), EXPERT_BLOCK)
        o_copy(g).start()
        cast_part(nxt, jnp.minimum(g - g0, W_PARTS - 1))
        return c

    lax.fori_loop(g0, g0 + nb, body, 0)
    lax.fori_loop(jnp.minimum(nb, W_PARTS), W_PARTS, lambda q, c: (cast_part(nxt, q), c)[1], 0)

    @pl.when(e == ne - 1)
    def _():
        lax.fori_loop(jnp.maximum(nu - EX_RING, 0), nu, lambda g, c: (o_copy(g).wait(), c)[1], 0)
        obuf[0] = jnp.zeros(obuf.shape[1:], obuf.dtype)

        def tail_copy(blk):
            r = pl.ds(pl.multiple_of(blk * orows, orows), orows)
            return pltpu.make_async_copy(obuf.at[0], yb_ref.at[r, :], tail_sem)

        lax.fori_loop(nu, n_blocks, lambda blk, c: (tail_copy(blk).start(), c)[1], 0)
        lax.fori_loop(nu, n_blocks, lambda blk, c: (tail_copy(blk).wait(), c)[1], 0)


def _experts(gstarts, nblk, n_used, buf, w_gate, w_up, w_down, layer, n_rows, pk):
    d, de = w_gate.shape[-2:]
    assert N_EXPERTS >= 3 and d % W_PARTS == 0 and de % W_PARTS == 0
    hbm = pl.BlockSpec(memory_space=pl.ANY)
    grid_spec = pltpu.PrefetchScalarGridSpec(
        num_scalar_prefetch=3,
        grid=(N_EXPERTS,),
        in_specs=[hbm, hbm, hbm, hbm],
        out_specs=hbm,
        scratch_shapes=[
            pltpu.VMEM((2, d, de), F32), pltpu.VMEM((2, d, de), F32), pltpu.VMEM((2, de, d), F32),
            pltpu.VMEM((2, d, de), BF16), pltpu.VMEM((2, d, de), BF16), pltpu.VMEM((2, de, d), BF16),
            pltpu.VMEM((EX_RING, EXPERT_BLOCK * pk, LANES), buf.dtype),
            pltpu.VMEM((EX_RING, EXPERT_BLOCK * pk, LANES), buf.dtype),
            pltpu.SemaphoreType.DMA((2,)), pltpu.SemaphoreType.DMA((EX_RING,)), pltpu.SemaphoreType.DMA((EX_RING,)),
            pltpu.SemaphoreType.DMA,
        ],
    )
    return pl.pallas_call(
        functools.partial(_expert_kernel, layer=layer, pk=pk),
        grid_spec=grid_spec,
        out_shape=jax.ShapeDtypeStruct((n_rows * pk, LANES), buf.dtype),
        compiler_params=_cparams(("arbitrary",)),
        name="moe_experts",
    )(gstarts, nblk, n_used, w_gate, w_up, w_down, buf)


CB_TM = 128


def _combine_kernel(dest_ref, x_ref, route_ref, gt_ref, gf_ref, yb_ref, o_ref, ybuf, sems, *, final_norm):
    tm, d = x_ref.shape
    fs = d // (2 * LANES)
    i = pl.program_id(0)
    n = pl.num_programs(0)

    def row_copy(tile, slot, r, k):
        src = dest_ref[(tile * tm + r) * TOP_K + k]
        return pltpu.make_async_copy(yb_ref.at[pl.ds(pl.multiple_of(src * fs, fs), fs), :],
                                     ybuf.at[slot, k, pl.ds(pl.multiple_of(r * fs, fs), fs), :], sems.at[slot])

    def issue(tile, slot):
        def body(q, c):
            for u in range(SC_UNROLL):
                for k in range(TOP_K):
                    row_copy(tile, slot, q * SC_UNROLL + u, k).start(priority=k)
            return c
        lax.fori_loop(0, tm // SC_UNROLL, body, 0)

    @pl.when(i == 0)
    def _():
        issue(0, 0)

    @pl.when(i + 1 < n)
    def _():
        issue(i + 1, (i + 1) % 2)

    slot = i % 2

    def drain(q, c):
        for u in range(SC_UNROLL):
            for k in range(TOP_K):
                row_copy(i, slot, q * SC_UNROLL + u, k).wait()
        return c
    lax.fori_loop(0, tm // SC_UNROLL, drain, 0)

    route = route_ref[...]
    ys = [_load_packed(ybuf.at[slot, k], tm, fs, F32) for k in range(TOP_K)]
    y = ys[0] * route[:, 2:3] + ys[1] * route[:, 3:4]
    xn = x_ref[...] + gt_ref[0] * y
    if final_norm:
        xn = _rms(xn, gf_ref[...])
    o_ref[...] = xn


def _combine(dest_flat, x2, route, gt, g_final, yb, seq, final_norm):
    t, d = x2.shape
    tpb = seq // CB_TM
    grid_spec = pltpu.PrefetchScalarGridSpec(
        num_scalar_prefetch=1,
        grid=(t // CB_TM,),
        in_specs=[
            pl.BlockSpec((CB_TM, d), lambda i, ds: (i, 0)),
            pl.BlockSpec((CB_TM, LANES), lambda i, ds: (i, 0)),
            pl.BlockSpec((1, 1, d), lambda i, ds: (i // tpb, 0, 0)),
            pl.BlockSpec((1, d), lambda i, ds: (0, 0)),
            pl.BlockSpec(memory_space=pl.ANY),
        ],
        out_specs=pl.BlockSpec((CB_TM, d), lambda i, ds: (i, 0)),
        scratch_shapes=[pltpu.VMEM((2, TOP_K, CB_TM * (d // (2 * LANES)), LANES), yb.dtype),
                        pltpu.SemaphoreType.DMA((2,))],
    )
    return pl.pallas_call(
        functools.partial(_combine_kernel, final_norm=final_norm),
        grid_spec=grid_spec,
        out_shape=jax.ShapeDtypeStruct((t, d), F32),
        compiler_params=_cparams(("arbitrary",)),
        name="moe_combine",
    )(dest_flat, x2, route, gt, g_final, yb)


def _dft_cos_sin(n):
    k = np.arange(n, dtype=np.int64)
    ang = (np.outer(k, k) % n).astype(np.float64) * (2.0 * np.pi / n)
    return np.cos(ang), np.sin(ang)


@functools.lru_cache(maxsize=None)
def _dft_constants(seq, dg):
    cc, sc = _dft_cos_sin(dg)
    cs_chan = np.concatenate([cc, sc], axis=1).astype(np.float32)
    k = np.arange(seq // 2, dtype=np.int64)[:, None]
    parts = []
    for par in range(2):
        n = 2 * np.arange(seq // 2, dtype=np.int64)[None, :] + par
        ang = ((k * n) % seq).astype(np.float64) * (2.0 * np.pi / seq)
        parts.append(np.concatenate([np.cos(ang), -np.sin(ang)], axis=1))
    cs_seq = np.stack(parts, axis=0).astype(np.float32)
    return cs_chan.astype(BF16), cs_seq.astype(BF16)


def _moe(x2, g2, sh, sc, gt, w_group, b_group, w_router, b_router, w_gate, w_up, w_down, layer,
         g_final, seq, final_norm):
    t, d = x2.shape
    w_all = jnp.concatenate([w_group, w_router], axis=1)
    w_all = jnp.pad(w_all, ((0, 0), (0, LANES - w_all.shape[1])))
    w_hi = w_all.astype(BF16)
    w_lo = (w_all - w_hi.astype(F32)).astype(BF16)
    whl = jnp.concatenate([w_hi, w_lo], axis=1)
    br = jnp.pad(jnp.concatenate([b_group, b_router]), (0, LANES - N_GROUPS - N_EXPERTS)).reshape(1, LANES)

    pk = d // (2 * LANES)
    hp, route, cnt = _route(x2, g2, sh, sc, whl, br, seq)

    counts = cnt[0, :N_EXPERTS].astype(jnp.int32)
    padded = (counts + EXPERT_BLOCK - 1) // EXPERT_BLOCK * EXPERT_BLOCK
    pends = jnp.cumsum(padded)
    pstarts = pends - padded
    n_assign = t * TOP_K
    n_blocks = (n_assign + N_EXPERTS * (EXPERT_BLOCK - 1) + EXPERT_BLOCK - 1) // EXPERT_BLOCK
    n_rows = n_blocks * EXPERT_BLOCK
    e_idx = route[:, 0:TOP_K].astype(jnp.int32)
    rank = route[:, 4:4 + TOP_K].astype(jnp.int32)
    is_e = e_idx[:, :, None] == jnp.arange(N_EXPERTS, dtype=jnp.int32)
    dest_flat = (jnp.sum(jnp.where(is_e, pstarts, 0), axis=-1) + rank).reshape(n_assign)
    n_used = (pends[-1:] // EXPERT_BLOCK).astype(jnp.int32)

    buf = _scatter(dest_flat, (pstarts + counts).astype(jnp.int32), (padded - counts).astype(jnp.int32), n_used,
                   hp, n_rows, pk)
    yb = _experts((pstarts // EXPERT_BLOCK).astype(jnp.int32), (padded // EXPERT_BLOCK).astype(jnp.int32), n_used,
                  buf, w_gate, w_up, w_down, layer, n_rows, pk)
    return _combine(dest_flat, x2, route, gt, g_final, yb, seq, final_norm)


def kernel(x, c, g_norm1, g_norm2, w_ada, b_ada, fa_w_in, fa_w_out, sg_w_in, sg_b_in, sg_g_v, sg_w_s, sg_b_s, sg_w_out, w_group, b_group, w_router, b_router, w_gate, w_up, w_down, g_final):
    n_batch, seq, d = x.shape
    depth = w_ada.shape[0]
    t = n_batch * seq
    x2 = x.reshape(t, d)

    mod = _ada(c, w_ada, b_ada)
    cs_chan, cs_seq = _dft_constants(seq, d // F_GROUPS)
    gfin = g_final.reshape(1, d)

    for l in range(depth):
        parts = [mod[l, :n_batch, k * d:(k + 1) * d].reshape(n_batch, 1, d) for k in range(6)]
        sh1, sc1, gt1, sh2, sc2, gt2 = parts
        g1 = g_norm1[l].reshape(1, d)
        j = l // 2
        if l % 2 == 0:
            ab = _fnet_in(x2, g1, sh1, sc1, fa_w_in[j].astype(BF16), jnp.asarray(cs_chan), n_batch, seq)
            x2 = _fnet_seq(jnp.asarray(cs_seq), ab, fa_w_out[j].astype(BF16), x2, gt1, n_batch, seq)
        else:
            x2 = _sgu(x2, g1, sh1, sc1, gt1, sg_w_in[j].astype(BF16), sg_b_in[j].reshape(1, 2 * d),
                      sg_g_v[j].reshape(1, d), sg_w_s[j].astype(BF16), sg_b_s[j].T, sg_w_out[j].astype(BF16), seq)
        x2 = _moe(x2, g_norm2[l].reshape(1, d), sh2, sc2, gt2, w_group[l], b_group[l], w_router[l], b_router[l],
                  w_gate, w_up, w_down, l, gfin, seq, final_norm=(l == depth - 1))
    return x2.reshape(n_batch, seq, d)
```

```python
import functools

import numpy as np
import jax
import jax.numpy as jnp
from jax import lax
from jax.experimental import pallas as pl
from jax.experimental.pallas import tpu as pltpu

F32 = jnp.float32
BF16 = jnp.bfloat16

EPS = 1e-6
F_GROUPS = 4
SGU_HEADS = 8
CHUNK = 128
N_GROUPS = 4
E_PER_GROUP = 8
N_EXPERTS = N_GROUPS * E_PER_GROUP
TOP_K = 2
EXPERT_BLOCK = 256

LANES = 128
SUBLANES = 8
VMEM_LIMIT = 56 * 1024 * 1024
NEG_BIG = -1e30


def _cparams(sem, vmem=VMEM_LIMIT):
    return pltpu.CompilerParams(dimension_semantics=sem, vmem_limit_bytes=vmem)


def _rms(x, g):
    return x * lax.rsqrt(jnp.mean(x * x, axis=-1, keepdims=True) + EPS) * g


def _store_packed(ref, vals_bf16, n_tok):
    pk = vals_bf16.shape[1] // (2 * LANES)
    bits = pltpu.bitcast(vals_bf16.astype(F32), jnp.uint32)
    for s in range(pk):
        low = bits[:, s * LANES:(s + 1) * LANES]
        high = bits[:, (s + pk) * LANES:(s + pk + 1) * LANES]
        ref[pl.ds(s, n_tok, stride=pk), :] = (high & jnp.uint32(0xFFFF0000)) | (low >> jnp.uint32(16))


def _load_packed(ref, n_tok, pk, dtype):
    lows, highs = [], []
    for s in range(pk):
        w = ref[pl.ds(s, n_tok, stride=pk), :]
        lows.append(pltpu.bitcast(w << jnp.uint32(16), F32).astype(dtype))
        highs.append(pltpu.bitcast(w & jnp.uint32(0xFFFF0000), F32).astype(dtype))
    return jnp.concatenate(lows + highs, axis=1)


ADA_TN = 512


def _ada_kernel(c_ref, w_ref, b_ref, o_ref, lhs_ref):
    @pl.when((pl.program_id(0) == 0) & (pl.program_id(1) == 0))
    def _():
        cv = c_ref[...]
        s = cv / (1.0 + jnp.exp(-cv))
        s_hi = s.astype(BF16).astype(F32)
        lhs_ref[...] = jnp.concatenate([s_hi, s - s_hi], axis=0).astype(BF16)

    w = w_ref[0]
    w_hi = w.astype(BF16)
    w_lo = (w - w_hi.astype(F32)).astype(BF16)
    lhs = lhs_ref[...]
    r = jnp.dot(lhs, w_hi, preferred_element_type=F32) + jnp.dot(lhs, w_lo, preferred_element_type=F32)
    o_ref[0] = r[:SUBLANES] + r[SUBLANES:] + b_ref[0]


def _ada(c, w_ada, b_ada):
    n_batch, d = c.shape
    depth, _, n6 = w_ada.shape
    assert n_batch <= SUBLANES
    c8 = jnp.pad(c, ((0, SUBLANES - n_batch), (0, 0)))
    return pl.pallas_call(
        _ada_kernel,
        grid=(depth, n6 // ADA_TN),
        in_specs=[
            pl.BlockSpec((SUBLANES, d), lambda l, j: (0, 0)),
            pl.BlockSpec((1, d, ADA_TN), lambda l, j: (l, 0, j)),
            pl.BlockSpec((1, 1, ADA_TN), lambda l, j: (l, 0, j)),
        ],
        out_specs=pl.BlockSpec((1, SUBLANES, ADA_TN), lambda l, j: (l, 0, j)),
        out_shape=jax.ShapeDtypeStruct((depth, SUBLANES, n6), F32),
        scratch_shapes=[pltpu.VMEM((2 * SUBLANES, d), BF16)],
        compiler_params=_cparams(("arbitrary", "arbitrary")),
        name="ada_mod",
    )(c8, w_ada, b_ada.reshape(depth, 1, n6))


FN_TM = 256


def _fnet_in_kernel(x_ref, g_ref, sh_ref, sc_ref, win_ref, cs_ref, ab_ref, xs_ref):
    tm, d = x_ref.shape
    dg = d // F_GROUPS
    half = tm // 2
    nch = d // LANES
    for c in range(nch):
        xs_ref[c] = x_ref[:, c * LANES:(c + 1) * LANES]
    x = jnp.concatenate(
        [jnp.concatenate([xs_ref[c, pl.ds(par, half, stride=2), :] for c in range(nch)], axis=1)
         for par in range(2)], axis=0)
    h = _rms(x, g_ref[...]) * (1.0 + sc_ref[0]) + sh_ref[0]
    z = jnp.dot(h.astype(BF16), win_ref[...], preferred_element_type=F32)
    for g in range(F_GROUPS):
        zg = z[:, g * dg:(g + 1) * dg].astype(BF16)
        ab = jnp.dot(zg, cs_ref[...], preferred_element_type=F32)
        for par in range(2):
            rows = slice(par * half, (par + 1) * half)
            ab_ref[0, par, 0, :, g * dg:(g + 1) * dg] = ab[rows, :dg].astype(BF16)
            ab_ref[0, par, 1, :, g * dg:(g + 1) * dg] = ab[rows, dg:].astype(BF16)


def _fnet_in(x2, g, sh, sc, w_in_bf, cs_chan, n_batch, seq):
    t, d = x2.shape
    tpb = seq // FN_TM
    dg = d // F_GROUPS
    return pl.pallas_call(
        _fnet_in_kernel,
        grid=(t // FN_TM,),
        in_specs=[
            pl.BlockSpec((FN_TM, d), lambda i: (i, 0)),
            pl.BlockSpec((1, d), lambda i: (0, 0)),
            pl.BlockSpec((1, 1, d), lambda i: (i // tpb, 0, 0)),
            pl.BlockSpec((1, 1, d), lambda i: (i // tpb, 0, 0)),
            pl.BlockSpec((d, d), lambda i: (0, 0), pipeline_mode=pl.Buffered(1)),
            pl.BlockSpec((dg, 2 * dg), lambda i: (0, 0), pipeline_mode=pl.Buffered(1)),
        ],
        out_specs=pl.BlockSpec((1, 2, 2, FN_TM // 2, d), lambda i: (i // tpb, 0, 0, i % tpb, 0)),
        out_shape=jax.ShapeDtypeStruct((n_batch, 2, 2, seq // 2, d), BF16),
        scratch_shapes=[pltpu.VMEM((d // LANES, FN_TM, LANES), F32)],
        compiler_params=_cparams(("arbitrary",)),
        name="fnet_in",
    )(x2, g, sh, sc, w_in_bf, cs_chan)


FS_TK = 128


def _fnet_seq_kernel(cs_ref, ab_ref, wout_ref, x_ref, gt_ref, o_ref, *, scale):
    tk = cs_ref.shape[1]
    _, two, hs, d = ab_ref.shape[1:]
    ye = jnp.dot(cs_ref[0], ab_ref[0, 0].reshape(two * hs, d), preferred_element_type=F32)
    yo = jnp.dot(cs_ref[1], ab_ref[0, 1].reshape(two * hs, d), preferred_element_type=F32)
    f = jnp.concatenate([(ye + yo) * scale, (ye - yo) * scale], axis=0).astype(BF16)
    y = jnp.dot(f, wout_ref[...], preferred_element_type=F32)
    o_ref[0, 0] = x_ref[0, 0] + gt_ref[0] * y[:tk]
    o_ref[0, 1] = x_ref[0, 1] + gt_ref[0] * y[tk:]


def _fnet_seq(cs_seq, ab, w_out_bf, x2, gt, n_batch, seq):
    t, d = x2.shape
    hs = seq // 2
    scale = 1.0 / float(np.sqrt(seq * (d // F_GROUPS)))
    x4 = x2.reshape(n_batch, 2, hs, d)
    out = pl.pallas_call(
        functools.partial(_fnet_seq_kernel, scale=scale),
        grid=(n_batch, hs // FS_TK),
        in_specs=[
            pl.BlockSpec((2, FS_TK, seq), lambda b, k: (0, k, 0)),
            pl.BlockSpec((1, 2, 2, hs, d), lambda b, k: (b, 0, 0, 0, 0), pipeline_mode=pl.Buffered(1)),
            pl.BlockSpec((d, d), lambda b, k: (0, 0), pipeline_mode=pl.Buffered(1)),
            pl.BlockSpec((1, 2, FS_TK, d), lambda b, k: (b, 0, k, 0)),
            pl.BlockSpec((1, 1, d), lambda b, k: (b, 0, 0)),
        ],
        out_specs=pl.BlockSpec((1, 2, FS_TK, d), lambda b, k: (b, 0, k, 0)),
        out_shape=jax.ShapeDtypeStruct((n_batch, 2, hs, d), F32),
        compiler_params=_cparams(("arbitrary", "arbitrary")),
        name="fnet_seq",
    )(cs_seq, ab, w_out_bf, x4, gt)
    return out.reshape(t, d)


SG_TM = 128


def _gelu_tanh(x):
    c = float(np.sqrt(2.0 / np.pi))
    return x * (0.5 * (1.0 + jnp.tanh(c * (x + 0.044715 * (x * x * x)))))


def _sgu_kernel(x_ref, g_ref, sh_ref, sc_ref, gt_ref, win_ref, bin_ref, gv_ref, ws_ref, bs_ref,
                wout_ref, o_ref, gated_ref):
    tm, d = x_ref.shape
    dh = d // SGU_HEADS
    x = x_ref[...]
    h = _rms(x, g_ref[...]) * (1.0 + sc_ref[0]) + sh_ref[0]
    z = jnp.dot(h.astype(BF16), win_ref[...], preferred_element_type=F32) + bin_ref[...]
    z = _gelu_tanh(z)
    u = z[:, :d]
    v = _rms(z[:, d:], gv_ref[...])
    for c in range(tm // CHUNK):
        rows = slice(c * CHUNK, (c + 1) * CHUNK)
        for hd in range(SGU_HEADS):
            cols = slice(hd * dh, (hd + 1) * dh)
            vc = v[rows, cols].astype(BF16)
            m = jnp.dot(ws_ref[hd], vc, preferred_element_type=F32) + bs_ref[:, hd:hd + 1]
            gated_ref[rows, cols] = (u[rows, cols] * m).astype(BF16)
    y = jnp.dot(gated_ref[...], wout_ref[...], preferred_element_type=F32)
    o_ref[...] = x + gt_ref[0] * y


def _sgu(x2, g, sh, sc, gt, w_in_bf, b_in, g_v, w_s_bf, b_s_t, w_out_bf, seq):
    t, d = x2.shape
    tpb = seq // SG_TM
    const2 = lambda i: (0, 0)
    perb = lambda i: (i // tpb, 0, 0)
    return pl.pallas_call(
        _sgu_kernel,
        grid=(t // SG_TM,),
        in_specs=[
            pl.BlockSpec((SG_TM, d), lambda i: (i, 0)),
            pl.BlockSpec((1, d), const2),
            pl.BlockSpec((1, 1, d), perb),
            pl.BlockSpec((1, 1, d), perb),
            pl.BlockSpec((1, 1, d), perb),
            pl.BlockSpec((d, 2 * d), const2, pipeline_mode=pl.Buffered(1)),
            pl.BlockSpec((1, 2 * d), const2),
            pl.BlockSpec((1, d), const2),
            pl.BlockSpec((SGU_HEADS, CHUNK, CHUNK), lambda i: (0, 0, 0)),
            pl.BlockSpec((CHUNK, SGU_HEADS), const2),
            pl.BlockSpec((d, d), const2, pipeline_mode=pl.Buffered(1)),
        ],
        out_specs=pl.BlockSpec((SG_TM, d), lambda i: (i, 0)),
        out_shape=jax.ShapeDtypeStruct((t, d), F32),
        scratch_shapes=[pltpu.VMEM((SG_TM, d), BF16)],
        compiler_params=_cparams(("arbitrary",)),
        name="sgu_mix",
    )(x2, g, sh, sc, gt, w_in_bf, b_in, g_v, w_s_bf, b_s_t, w_out_bf)


RT_TM = 256


def _route_kernel(x_ref, g_ref, sh_ref, sc_ref, whl_ref, br_ref, h_ref, route_ref, cnt_ref, carry_ref):
    tm = x_ref.shape[0]
    i = pl.program_id(0)

    @pl.when(i == 0)
    def _():
        carry_ref[...] = jnp.zeros_like(carry_ref)

    h = _rms(x_ref[...], g_ref[...]) * (1.0 + sc_ref[0]) + sh_ref[0]
    h_hi = h.astype(BF16)
    _store_packed(h_ref, h_hi, tm)
    h_lo = (h - h_hi.astype(F32)).astype(BF16)
    whl = whl_ref[...]
    both = jnp.dot(h_hi, whl, preferred_element_type=F32)
    lo = jnp.dot(h_lo, whl[:, :LANES], preferred_element_type=F32)
    lg = both[:, :LANES] + both[:, LANES:] + lo + br_ref[...]

    lane = lax.broadcasted_iota(jnp.int32, (tm, LANES), 1)
    lanef = lane.astype(F32)
    is_g = lane < N_GROUPS
    gl = jnp.where(is_g, lg, NEG_BIG)
    gmax = jnp.max(gl, axis=1, keepdims=True)
    gidx = jnp.min(jnp.where(gl == gmax, lanef, float(LANES)), axis=1, keepdims=True)
    gsum = jnp.sum(jnp.where(is_g, jnp.exp(gl - gmax), 0.0), axis=1, keepdims=True)
    g_w = 1.0 / gsum
    lo_lane = float(N_GROUPS) + gidx * float(E_PER_GROUP)
    in_grp = (lanef >= lo_lane) & (lanef < lo_lane + float(E_PER_GROUP))
    el = jnp.where(in_grp, lg, NEG_BIG)
    v1 = jnp.max(el, axis=1, keepdims=True)
    i1 = jnp.min(jnp.where(el == v1, lanef, float(LANES)), axis=1, keepdims=True)
    el2 = jnp.where(lanef == i1, NEG_BIG, el)
    v2 = jnp.max(el2, axis=1, keepdims=True)
    i2 = jnp.min(jnp.where(el2 == v2, lanef, float(LANES)), axis=1, keepdims=True)
    p = jnp.exp(v2 - v1)
    w0 = g_w / (1.0 + p)
    w1 = g_w * p / (1.0 + p)
    e0 = i1 - float(N_GROUPS)
    e1 = i2 - float(N_GROUPS)

    oh0 = jnp.where(lanef == e0, 1.0, 0.0)
    oh1 = jnp.where(lanef == e1, 1.0, 0.0)
    rr = lax.broadcasted_iota(jnp.int32, (tm, tm), 0)
    cc = lax.broadcasted_iota(jnp.int32, (tm, tm), 1)
    tri = jnp.where(rr > cc, 1.0, 0.0).astype(BF16)
    pre0 = jnp.dot(tri, oh0.astype(BF16), preferred_element_type=F32)
    pre1 = jnp.dot(tri, oh1.astype(BF16), preferred_element_type=F32)
    carry = carry_ref[0:1, :]
    cnt0 = jnp.sum(oh0, axis=0, keepdims=True)
    cnt1 = jnp.sum(oh1, axis=0, keepdims=True)
    rank0 = jnp.sum(oh0 * (pre0 + carry), axis=1, keepdims=True)
    rank1 = jnp.sum(oh1 * (pre1 + carry + cnt0), axis=1, keepdims=True)
    new_carry = carry + cnt0 + cnt1
    carry_ref[...] = jnp.broadcast_to(new_carry, carry_ref.shape)
    cnt_ref[...] = jnp.broadcast_to(new_carry, cnt_ref.shape)

    route = jnp.where(lane == 0, e0, 0.0)
    route = jnp.where(lane == 1, e1, route)
    route = jnp.where(lane == 2, w0, route)
    route = jnp.where(lane == 3, w1, route)
    route = jnp.where(lane == 4, rank0, route)
    route = jnp.where(lane == 5, rank1, route)
    route_ref[...] = route


def _route(x2, g, sh, sc, whl, br, seq):
    t, d = x2.shape
    tpb = seq // RT_TM
    pk = d // (2 * LANES)
    perb = lambda i: (i // tpb, 0, 0)
    return pl.pallas_call(
        _route_kernel,
        grid=(t // RT_TM,),
        in_specs=[
            pl.BlockSpec((RT_TM, d), lambda i: (i, 0)),
            pl.BlockSpec((1, d), lambda i: (0, 0)),
            pl.BlockSpec((1, 1, d), perb),
            pl.BlockSpec((1, 1, d), perb),
            pl.BlockSpec((d, 2 * LANES), lambda i: (0, 0)),
            pl.BlockSpec((1, LANES), lambda i: (0, 0)),
        ],
        out_specs=[
            pl.BlockSpec((RT_TM * pk, LANES), lambda i: (i, 0)),
            pl.BlockSpec((RT_TM, LANES), lambda i: (i, 0)),
            pl.BlockSpec((SUBLANES, LANES), lambda i: (0, 0)),
        ],
        out_shape=[
            jax.ShapeDtypeStruct((t * pk, LANES), jnp.uint32),
            jax.ShapeDtypeStruct((t, LANES), F32),
            jax.ShapeDtypeStruct((SUBLANES, LANES), F32),
        ],
        scratch_shapes=[pltpu.VMEM((SUBLANES, LANES), F32)],
        compiler_params=_cparams(("arbitrary",)),
        name="moe_route",
    )(x2, g, sh, sc, whl, br)


SC_TM = 1024


SC_UNROLL = 8


def _scatter_kernel(dest_ref, pad_start_ref, pad_n_ref, nu_ref, h_ref, buf_ref, zrow_ref, sem, zsem, tsem, *, pk):
    tm = h_ref.shape[0] // pk
    i = pl.program_id(0)
    base = i * (tm * TOP_K)
    blk_rows = EXPERT_BLOCK * pk
    n_blocks = buf_ref.shape[0] // blk_rows

    def tok_rows(tok):
        return pl.ds(pl.multiple_of(tok * pk, pk), pk)

    def row_copy(r, k):
        dst = dest_ref[base + r * TOP_K + k]
        return pltpu.make_async_copy(h_ref.at[tok_rows(r), :], buf_ref.at[tok_rows(dst), :], sem)

    def pad_fill(e, do):
        n = pad_n_ref[e]
        start = pad_start_ref[e]
        size = EXPERT_BLOCK // 2
        while size >= 1:
            has = (n & size) != 0
            cp = pltpu.make_async_copy(zrow_ref.at[pl.ds(0, size * pk), :],
                                       buf_ref.at[pl.ds(pl.multiple_of(start * pk, pk), size * pk), :], zsem)

            @pl.when(has)
            def _():
                do(cp)

            start = start + jnp.where(has, size, 0)
            size //= 2

    def tail_copy(blk):
        row0 = pl.multiple_of(blk * blk_rows, blk_rows)
        return pltpu.make_async_copy(zrow_ref, buf_ref.at[pl.ds(row0, blk_rows), :], tsem)

    @pl.when(i == 0)
    def _():
        zrow_ref[...] = jnp.zeros_like(zrow_ref)

        def per_expert(e, c):
            pad_fill(e, lambda cp: cp.start())
            return c
        lax.fori_loop(0, N_EXPERTS, per_expert, 0)
        lax.fori_loop(nu_ref[0], n_blocks, lambda blk, c: (tail_copy(blk).start(), c)[1], 0)

    def issue(q, c):
        for u in range(SC_UNROLL):
            for k in range(TOP_K):
                row_copy(q * SC_UNROLL + u, k).start(priority=k)
        return c
    lax.fori_loop(0, tm // SC_UNROLL, issue, 0)

    def drain(q, c):
        for u in range(SC_UNROLL):
            for k in range(TOP_K):
                row_copy(q * SC_UNROLL + u, k).wait()
        return c
    lax.fori_loop(0, tm // SC_UNROLL, drain, 0)

    @pl.when(i == 0)
    def _():
        def per_expert(e, c):
            pad_fill(e, lambda cp: cp.wait())
            return c
        lax.fori_loop(0, N_EXPERTS, per_expert, 0)
        lax.fori_loop(nu_ref[0], n_blocks, lambda blk, c: (tail_copy(blk).wait(), c)[1], 0)


def _scatter(dest_flat, pad_start, pad_n, n_used, hp, n_rows, pk):
    t = hp.shape[0] // pk
    grid_spec = pltpu.PrefetchScalarGridSpec(
        num_scalar_prefetch=4,
        grid=(t // SC_TM,),
        in_specs=[pl.BlockSpec((SC_TM * pk, LANES), lambda i, *_: (i, 0))],
        out_specs=pl.BlockSpec(memory_space=pl.ANY),
        scratch_shapes=[pltpu.VMEM((EXPERT_BLOCK * pk, LANES), hp.dtype), pltpu.SemaphoreType.DMA,
                        pltpu.SemaphoreType.DMA, pltpu.SemaphoreType.DMA],
    )
    return pl.pallas_call(
        functools.partial(_scatter_kernel, pk=pk),
        grid_spec=grid_spec,
        out_shape=jax.ShapeDtypeStruct((n_rows * pk, LANES), hp.dtype),
        compiler_params=_cparams(("arbitrary",)),
        name="moe_scatter",
    )(dest_flat, pad_start, pad_n, n_used, hp)


EX_RING = 4
W_PARTS = 2


def _expert_kernel(pstart_ref, nblk_ref, nu_ref, wg_hbm, wu_hbm, wd_hbm, buf_ref, yb_ref,
                   wg_st, wu_st, wd_st, wg_bf, wu_bf, wd_bf, xbuf, obuf, w_sem, in_sem, out_sem, tail_sem,
                   *, layer, pk):
    e = pl.program_id(0)
    ne = pl.num_programs(0)
    nb = nblk_ref[e]
    g0 = pstart_ref[e]
    nu = nu_ref[0]
    xrows = EXPERT_BLOCK * pk
    orows = EXPERT_BLOCK * pk
    n_blocks = yb_ref.shape[0] // orows
    cur = e % 2
    nxt = (e + 1) % 2

    def w_copies(ex, slot):
        return [pltpu.make_async_copy(hbm.at[layer, ex], st.at[slot], w_sem.at[slot])
                for hbm, st in ((wg_hbm, wg_st), (wu_hbm, wu_st), (wd_hbm, wd_st))]

    def cast_part(slot, q):
        for st, bf in ((wg_st, wg_bf), (wu_st, wu_bf), (wd_st, wd_bf)):
            n = st.shape[1] // W_PARTS
            r = pl.ds(pl.multiple_of(q * n, n), n)
            bf[slot, r, :] = st[slot, r, :].astype(BF16)

    @pl.when(e == 0)
    def _():
        for c in w_copies(0, 0):
            c.start()
        for c in w_copies(1, 1):
            c.start()
        for c in w_copies(0, 0):
            c.wait()
        for q in range(W_PARTS):
            cast_part(0, q)
        for c in w_copies(2, 0):
            c.start()

    @pl.when((e >= 1) & (e + 2 < ne))
    def _():
        for c in w_copies(e + 2, cur):
            c.start()

    @pl.when(e + 1 < ne)
    def _():
        for c in w_copies(e + 1, nxt):
            c.wait()

    def x_copy(g):
        slot = g % EX_RING
        src = buf_ref.at[pl.ds(pl.multiple_of(g * xrows, xrows), xrows), :]
        return pltpu.make_async_copy(src, xbuf.at[slot], in_sem.at[slot])

    def o_copy(g):
        slot = g % EX_RING
        dst = yb_ref.at[pl.ds(pl.multiple_of(g * orows, orows), orows), :]
        return pltpu.make_async_copy(obuf.at[slot], dst, out_sem.at[slot])

    @pl.when(e == 0)
    def _():
        for k in range(EX_RING - 1):
            @pl.when(k < nu)
            def _():
                x_copy(k).start()

    def body(g, c):
        @pl.when(g + (EX_RING - 1) < nu)
        def _():
            x_copy(g + (EX_RING - 1)).start()

        x_copy(g).wait()

        @pl.when(g >= EX_RING)
        def _():
            o_copy(g - EX_RING).wait()

        slot = g % EX_RING
        xb = _load_packed(xbuf.at[slot], EXPERT_BLOCK, pk, BF16)
        gte = jnp.dot(xb, wg_bf[cur], preferred_element_type=F32)
        up = jnp.dot(xb, wu_bf[cur], preferred_element_type=F32)
        act = gte * (1.0 / (1.0 + jnp.exp(-gte))) * up
        y = jnp.dot(act.astype(BF16), wd_bf[cur], preferred_element_type=F32)
        _store_packed(obuf.at[slot], y.astype(BF16), EXPERT_BLOCK)
        o_copy(g).start()
        cast_part(nxt, jnp.minimum(g - g0, W_PARTS - 1))
        return c

    lax.fori_loop(g0, g0 + nb, body, 0)
    lax.fori_loop(jnp.minimum(nb, W_PARTS), W_PARTS, lambda q, c: (cast_part(nxt, q), c)[1], 0)

    @pl.when(e == ne - 1)
    def _():
        lax.fori_loop(jnp.maximum(nu - EX_RING, 0), nu, lambda g, c: (o_copy(g).wait(), c)[1], 0)
        obuf[0] = jnp.zeros(obuf.shape[1:], obuf.dtype)

        def tail_copy(blk):
            r = pl.ds(pl.multiple_of(blk * orows, orows), orows)
            return pltpu.make_async_copy(obuf.at[0], yb_ref.at[r, :], tail_sem)

        lax.fori_loop(nu, n_blocks, lambda blk, c: (tail_copy(blk).start(), c)[1], 0)
        lax.fori_loop(nu, n_blocks, lambda blk, c: (tail_copy(blk).wait(), c)[1], 0)


def _experts(gstarts, nblk, n_used, buf, w_gate, w_up, w_down, layer, n_rows, pk):
    d, de = w_gate.shape[-2:]
    assert N_EXPERTS >= 3 and d % W_PARTS == 0 and de % W_PARTS == 0
    hbm = pl.BlockSpec(memory_space=pl.ANY)
    grid_spec = pltpu.PrefetchScalarGridSpec(
        num_scalar_prefetch=3,
        grid=(N_EXPERTS,),
        in_specs=[hbm, hbm, hbm, hbm],
        out_specs=hbm,
        scratch_shapes=[
            pltpu.VMEM((2, d, de), F32), pltpu.VMEM((2, d, de), F32), pltpu.VMEM((2, de, d), F32),
            pltpu.VMEM((2, d, de), BF16), pltpu.VMEM((2, d, de), BF16), pltpu.VMEM((2, de, d), BF16),
            pltpu.VMEM((EX_RING, EXPERT_BLOCK * pk, LANES), buf.dtype),
            pltpu.VMEM((EX_RING, EXPERT_BLOCK * pk, LANES), buf.dtype),
            pltpu.SemaphoreType.DMA((2,)), pltpu.SemaphoreType.DMA((EX_RING,)), pltpu.SemaphoreType.DMA((EX_RING,)),
            pltpu.SemaphoreType.DMA,
        ],
    )
    return pl.pallas_call(
        functools.partial(_expert_kernel, layer=layer, pk=pk),
        grid_spec=grid_spec,
        out_shape=jax.ShapeDtypeStruct((n_rows * pk, LANES), buf.dtype),
        compiler_params=_cparams(("arbitrary",)),
        name="moe_experts",
    )(gstarts, nblk, n_used, w_gate, w_up, w_down, buf)


CB_TM = 256


def _combine_kernel(dest_ref, x_ref, route_ref, gt_ref, gf_ref, yb_ref, o_ref, ybuf, sems, *, final_norm):
    tm, d = x_ref.shape
    fs = d // (2 * LANES)
    i = pl.program_id(0)
    n = pl.num_programs(0)

    def row_copy(tile, slot, r, k):
        src = dest_ref[(tile * tm + r) * TOP_K + k]
        return pltpu.make_async_copy(yb_ref.at[pl.ds(pl.multiple_of(src * fs, fs), fs), :],
                                     ybuf.at[slot, k, pl.ds(pl.multiple_of(r * fs, fs), fs), :], sems.at[slot])

    def issue(tile, slot):
        def body(q, c):
            for u in range(SC_UNROLL):
                for k in range(TOP_K):
                    row_copy(tile, slot, q * SC_UNROLL + u, k).start(priority=k)
            return c
        lax.fori_loop(0, tm // SC_UNROLL, body, 0)

    @pl.when(i == 0)
    def _():
        issue(0, 0)

    @pl.when(i + 1 < n)
    def _():
        issue(i + 1, (i + 1) % 2)

    slot = i % 2

    def drain(q, c):
        for u in range(SC_UNROLL):
            for k in range(TOP_K):
                row_copy(i, slot, q * SC_UNROLL + u, k).wait()
        return c
    lax.fori_loop(0, tm // SC_UNROLL, drain, 0)

    route = route_ref[...]
    ys = [_load_packed(ybuf.at[slot, k], tm, fs, F32) for k in range(TOP_K)]
    y = ys[0] * route[:, 2:3] + ys[1] * route[:, 3:4]
    xn = x_ref[...] + gt_ref[0] * y
    if final_norm:
        xn = _rms(xn, gf_ref[...])
    o_ref[...] = xn


def _combine(dest_flat, x2, route, gt, g_final, yb, seq, final_norm):
    t, d = x2.shape
    tpb = seq // CB_TM
    grid_spec = pltpu.PrefetchScalarGridSpec(
        num_scalar_prefetch=1,
        grid=(t // CB_TM,),
        in_specs=[
            pl.BlockSpec((CB_TM, d), lambda i, ds: (i, 0)),
            pl.BlockSpec((CB_TM, LANES), lambda i, ds: (i, 0)),
            pl.BlockSpec((1, 1, d), lambda i, ds: (i // tpb, 0, 0)),
            pl.BlockSpec((1, d), lambda i, ds: (0, 0)),
            pl.BlockSpec(memory_space=pl.ANY),
        ],
        out_specs=pl.BlockSpec((CB_TM, d), lambda i, ds: (i, 0)),
        scratch_shapes=[pltpu.VMEM((2, TOP_K, CB_TM * (d // (2 * LANES)), LANES), yb.dtype),
                        pltpu.SemaphoreType.DMA((2,))],
    )
    return pl.pallas_call(
        functools.partial(_combine_kernel, final_norm=final_norm),
        grid_spec=grid_spec,
        out_shape=jax.ShapeDtypeStruct((t, d), F32),
        compiler_params=_cparams(("arbitrary",)),
        name="moe_combine",
    )(dest_flat, x2, route, gt, g_final, yb)


def _dft_cos_sin(n):
    k = np.arange(n, dtype=np.int64)
    ang = (np.outer(k, k) % n).astype(np.float64) * (2.0 * np.pi / n)
    return np.cos(ang), np.sin(ang)


@functools.lru_cache(maxsize=None)
def _dft_constants(seq, dg):
    cc, sc = _dft_cos_sin(dg)
    cs_chan = np.concatenate([cc, sc], axis=1).astype(np.float32)
    k = np.arange(seq // 2, dtype=np.int64)[:, None]
    parts = []
    for par in range(2):
        n = 2 * np.arange(seq // 2, dtype=np.int64)[None, :] + par
        ang = ((k * n) % seq).astype(np.float64) * (2.0 * np.pi / seq)
        parts.append(np.concatenate([np.cos(ang), -np.sin(ang)], axis=1))
    cs_seq = np.stack(parts, axis=0).astype(np.float32)
    return cs_chan.astype(BF16), cs_seq.astype(BF16)


def _moe(x2, g2, sh, sc, gt, w_group, b_group, w_router, b_router, w_gate, w_up, w_down, layer,
         g_final, seq, final_norm):
    t, d = x2.shape
    w_all = jnp.concatenate([w_group, w_router], axis=1)
    w_all = jnp.pad(w_all, ((0, 0), (0, LANES - w_all.shape[1])))
    w_hi = w_all.astype(BF16)
    w_lo = (w_all - w_hi.astype(F32)).astype(BF16)
    whl = jnp.concatenate([w_hi, w_lo], axis=1)
    br = jnp.pad(jnp.concatenate([b_group, b_router]), (0, LANES - N_GROUPS - N_EXPERTS)).reshape(1, LANES)

    pk = d // (2 * LANES)
    hp, route, cnt = _route(x2, g2, sh, sc, whl, br, seq)

    counts = cnt[0, :N_EXPERTS].astype(jnp.int32)
    padded = (counts + EXPERT_BLOCK - 1) // EXPERT_BLOCK * EXPERT_BLOCK
    pends = jnp.cumsum(padded)
    pstarts = pends - padded
    n_assign = t * TOP_K
    n_blocks = (n_assign + N_EXPERTS * (EXPERT_BLOCK - 1) + EXPERT_BLOCK - 1) // EXPERT_BLOCK
    n_rows = n_blocks * EXPERT_BLOCK
    e_idx = route[:, 0:TOP_K].astype(jnp.int32)
    rank = route[:, 4:4 + TOP_K].astype(jnp.int32)
    is_e = e_idx[:, :, None] == jnp.arange(N_EXPERTS, dtype=jnp.int32)
    dest_flat = (jnp.sum(jnp.where(is_e, pstarts, 0), axis=-1) + rank).reshape(n_assign)
    n_used = (pends[-1:] // EXPERT_BLOCK).astype(jnp.int32)

    buf = _scatter(dest_flat, (pstarts + counts).astype(jnp.int32), (padded - counts).astype(jnp.int32), n_used,
                   hp, n_rows, pk)
    yb = _experts((pstarts // EXPERT_BLOCK).astype(jnp.int32), (padded // EXPERT_BLOCK).astype(jnp.int32), n_used,
                  buf, w_gate, w_up, w_down, layer, n_rows, pk)
    return _combine(dest_flat, x2, route, gt, g_final, yb, seq, final_norm)


def kernel(x, c, g_norm1, g_norm2, w_ada, b_ada, fa_w_in, fa_w_out, sg_w_in, sg_b_in, sg_g_v, sg_w_s, sg_b_s, sg_w_out, w_group, b_group, w_router, b_router, w_gate, w_up, w_down, g_final):
    n_batch, seq, d = x.shape
    depth = w_ada.shape[0]
    t = n_batch * seq
    x2 = x.reshape(t, d)

    mod = _ada(c, w_ada, b_ada)
    cs_chan, cs_seq = _dft_constants(seq, d // F_GROUPS)
    gfin = g_final.reshape(1, d)

    for l in range(depth):
        parts = [mod[l, :n_batch, k * d:(k + 1) * d].reshape(n_batch, 1, d) for k in range(6)]
        sh1, sc1, gt1, sh2, sc2, gt2 = parts
        g1 = g_norm1[l].reshape(1, d)
        j = l // 2
        if l % 2 == 0:
            ab = _fnet_in(x2, g1, sh1, sc1, fa_w_in[j].astype(BF16), jnp.asarray(cs_chan), n_batch, seq)
            x2 = _fnet_seq(jnp.asarray(cs_seq), ab, fa_w_out[j].astype(BF16), x2, gt1, n_batch, seq)
        else:
            x2 = _sgu(x2, g1, sh1, sc1, gt1, sg_w_in[j].astype(BF16), sg_b_in[j].reshape(1, 2 * d),
                      sg_g_v[j].reshape(1, d), sg_w_s[j].astype(BF16), sg_b_s[j].T, sg_w_out[j].astype(BF16), seq)
        x2 = _moe(x2, g_norm2[l].reshape(1, d), sh2, sc2, gt2, w_group[l], b_group[l], w_router[l], b_router[l],
                  w_gate, w_up, w_down, l, gfin, seq, final_norm=(l == depth - 1))
    return x2.reshape(n_batch, seq, d)
```

```python
import functools

import numpy as np
import jax
import jax.numpy as jnp
from jax import lax
from jax.experimental import pallas as pl
from jax.experimental.pallas import tpu as pltpu

F32 = jnp.float32
BF16 = jnp.bfloat16

EPS = 1e-6
F_GROUPS = 4
SGU_HEADS = 8
CHUNK = 128
N_GROUPS = 4
E_PER_GROUP = 8
N_EXPERTS = N_GROUPS * E_PER_GROUP
TOP_K = 2
EXPERT_BLOCK = 256

LANES = 128
SUBLANES = 8
VMEM_LIMIT = 56 * 1024 * 1024
NEG_BIG = -1e30


def _cparams(sem, vmem=VMEM_LIMIT):
    return pltpu.CompilerParams(dimension_semantics=sem, vmem_limit_bytes=vmem)


def _rms(x, g):
    return x * lax.rsqrt(jnp.mean(x * x, axis=-1, keepdims=True) + EPS) * g


def _store_packed(ref, vals_bf16, n_tok):
    pk = vals_bf16.shape[1] // (2 * LANES)
    bits = pltpu.bitcast(vals_bf16.astype(F32), jnp.uint32)
    for s in range(pk):
        low = bits[:, s * LANES:(s + 1) * LANES]
        high = bits[:, (s + pk) * LANES:(s + pk + 1) * LANES]
        ref[pl.ds(s, n_tok, stride=pk), :] = (high & jnp.uint32(0xFFFF0000)) | (low >> jnp.uint32(16))


def _load_packed(ref, n_tok, pk, dtype):
    lows, highs = [], []
    for s in range(pk):
        w = ref[pl.ds(s, n_tok, stride=pk), :]
        lows.append(pltpu.bitcast(w << jnp.uint32(16), F32).astype(dtype))
        highs.append(pltpu.bitcast(w & jnp.uint32(0xFFFF0000), F32).astype(dtype))
    return jnp.concatenate(lows + highs, axis=1)


ADA_TN = 512


def _ada_kernel(c_ref, w_ref, b_ref, o_ref, lhs_ref):
    @pl.when((pl.program_id(0) == 0) & (pl.program_id(1) == 0))
    def _():
        cv = c_ref[...]
        s = cv / (1.0 + jnp.exp(-cv))
        s_hi = s.astype(BF16).astype(F32)
        lhs_ref[...] = jnp.concatenate([s_hi, s - s_hi], axis=0).astype(BF16)

    w = w_ref[0]
    w_hi = w.astype(BF16)
    w_lo = (w - w_hi.astype(F32)).astype(BF16)
    lhs = lhs_ref[...]
    r = jnp.dot(lhs, w_hi, preferred_element_type=F32) + jnp.dot(lhs, w_lo, preferred_element_type=F32)
    o_ref[0] = r[:SUBLANES] + r[SUBLANES:] + b_ref[0]


def _ada(c, w_ada, b_ada):
    n_batch, d = c.shape
    depth, _, n6 = w_ada.shape
    assert n_batch <= SUBLANES
    c8 = jnp.pad(c, ((0, SUBLANES - n_batch), (0, 0)))
    return pl.pallas_call(
        _ada_kernel,
        grid=(depth, n6 // ADA_TN),
        in_specs=[
            pl.BlockSpec((SUBLANES, d), lambda l, j: (0, 0)),
            pl.BlockSpec((1, d, ADA_TN), lambda l, j: (l, 0, j)),
            pl.BlockSpec((1, 1, ADA_TN), lambda l, j: (l, 0, j)),
        ],
        out_specs=pl.BlockSpec((1, SUBLANES, ADA_TN), lambda l, j: (l, 0, j)),
        out_shape=jax.ShapeDtypeStruct((depth, SUBLANES, n6), F32),
        scratch_shapes=[pltpu.VMEM((2 * SUBLANES, d), BF16)],
        compiler_params=_cparams(("arbitrary", "arbitrary")),
        name="ada_mod",
    )(c8, w_ada, b_ada.reshape(depth, 1, n6))


FN_TM = 256


def _fnet_in_kernel(x_ref, g_ref, sh_ref, sc_ref, win_ref, cs_ref, ab_ref, xs_ref):
    tm, d = x_ref.shape
    dg = d // F_GROUPS
    half = tm // 2
    nch = d // LANES
    for c in range(nch):
        xs_ref[c] = x_ref[:, c * LANES:(c + 1) * LANES]
    x = jnp.concatenate(
        [jnp.concatenate([xs_ref[c, pl.ds(par, half, stride=2), :] for c in range(nch)], axis=1)
         for par in range(2)], axis=0)
    h = _rms(x, g_ref[...]) * (1.0 + sc_ref[0]) + sh_ref[0]
    z = jnp.dot(h.astype(BF16), win_ref[...], preferred_element_type=F32)
    for g in range(F_GROUPS):
        zg = z[:, g * dg:(g + 1) * dg].astype(BF16)
        ab = jnp.dot(zg, cs_ref[...], preferred_element_type=F32)
        for par in range(2):
            rows = slice(par * half, (par + 1) * half)
            ab_ref[0, par, 0, :, g * dg:(g + 1) * dg] = ab[rows, :dg].astype(BF16)
            ab_ref[0, par, 1, :, g * dg:(g + 1) * dg] = ab[rows, dg:].astype(BF16)


def _fnet_in(x2, g, sh, sc, w_in_bf, cs_chan, n_batch, seq):
    t, d = x2.shape
    tpb = seq // FN_TM
    dg = d // F_GROUPS
    return pl.pallas_call(
        _fnet_in_kernel,
        grid=(t // FN_TM,),
        in_specs=[
            pl.BlockSpec((FN_TM, d), lambda i: (i, 0)),
            pl.BlockSpec((1, d), lambda i: (0, 0)),
            pl.BlockSpec((1, 1, d), lambda i: (i // tpb, 0, 0)),
            pl.BlockSpec((1, 1, d), lambda i: (i // tpb, 0, 0)),
            pl.BlockSpec((d, d), lambda i: (0, 0), pipeline_mode=pl.Buffered(1)),
            pl.BlockSpec((dg, 2 * dg), lambda i: (0, 0), pipeline_mode=pl.Buffered(1)),
        ],
        out_specs=pl.BlockSpec((1, 2, 2, FN_TM // 2, d), lambda i: (i // tpb, 0, 0, i % tpb, 0)),
        out_shape=jax.ShapeDtypeStruct((n_batch, 2, 2, seq // 2, d), BF16),
        scratch_shapes=[pltpu.VMEM((d // LANES, FN_TM, LANES), F32)],
        compiler_params=_cparams(("arbitrary",)),
        name="fnet_in",
    )(x2, g, sh, sc, w_in_bf, cs_chan)


FS_TK = 256
FS_SUB = 128


def _fnet_seq_kernel(cs_ref, ab_ref, wout_ref, x_ref, gt_ref, o_ref, *, scale):
    tk = cs_ref.shape[1]
    _, two, hs, d = ab_ref.shape[1:]
    sub = FS_SUB
    for c in range(tk // sub):
        rows = slice(c * sub, (c + 1) * sub)
        ye = jnp.dot(cs_ref[0, rows, :], ab_ref[0, 0].reshape(two * hs, d), preferred_element_type=F32)
        yo = jnp.dot(cs_ref[1, rows, :], ab_ref[0, 1].reshape(two * hs, d), preferred_element_type=F32)
        f = jnp.concatenate([(ye + yo) * scale, (ye - yo) * scale], axis=0).astype(BF16)
        y = jnp.dot(f, wout_ref[...], preferred_element_type=F32)
        o_ref[0, 0, rows, :] = x_ref[0, 0, rows, :] + gt_ref[0] * y[:sub]
        o_ref[0, 1, rows, :] = x_ref[0, 1, rows, :] + gt_ref[0] * y[sub:]


def _fnet_seq(cs_seq, ab, w_out_bf, x2, gt, n_batch, seq):
    t, d = x2.shape
    hs = seq // 2
    scale = 1.0 / float(np.sqrt(seq * (d // F_GROUPS)))
    x4 = x2.reshape(n_batch, 2, hs, d)
    out = pl.pallas_call(
        functools.partial(_fnet_seq_kernel, scale=scale),
        grid=(n_batch, hs // FS_TK),
        in_specs=[
            pl.BlockSpec((2, FS_TK, seq), lambda b, k: (0, k, 0)),
            pl.BlockSpec((1, 2, 2, hs, d), lambda b, k: (b, 0, 0, 0, 0), pipeline_mode=pl.Buffered(1)),
            pl.BlockSpec((d, d), lambda b, k: (0, 0), pipeline_mode=pl.Buffered(1)),
            pl.BlockSpec((1, 2, FS_TK, d), lambda b, k: (b, 0, k, 0)),
            pl.BlockSpec((1, 1, d), lambda b, k: (b, 0, 0)),
        ],
        out_specs=pl.BlockSpec((1, 2, FS_TK, d), lambda b, k: (b, 0, k, 0)),
        out_shape=jax.ShapeDtypeStruct((n_batch, 2, hs, d), F32),
        compiler_params=_cparams(("arbitrary", "arbitrary")),
        name="fnet_seq",
    )(cs_seq, ab, w_out_bf, x4, gt)
    return out.reshape(t, d)


SG_TM = 256


def _gelu_tanh(x):
    c = float(np.sqrt(2.0 / np.pi))
    return x * (0.5 * (1.0 + jnp.tanh(c * (x + 0.044715 * (x * x * x)))))


def _sgu_kernel(x_ref, g_ref, sh_ref, sc_ref, gt_ref, win_ref, bin_ref, gv_ref, ws_ref, bs_ref,
                wout_ref, o_ref, gated_ref):
    tm, d = x_ref.shape
    dh = d // SGU_HEADS
    for c in range(tm // CHUNK):
        rows = slice(c * CHUNK, (c + 1) * CHUNK)
        x = x_ref[rows, :]
        h = _rms(x, g_ref[...]) * (1.0 + sc_ref[0]) + sh_ref[0]
        z = jnp.dot(h.astype(BF16), win_ref[...], preferred_element_type=F32) + bin_ref[...]
        z = _gelu_tanh(z)
        u = z[:, :d]
        v = _rms(z[:, d:], gv_ref[...])
        for hd in range(SGU_HEADS):
            cols = slice(hd * dh, (hd + 1) * dh)
            vc = v[:, cols].astype(BF16)
            m = jnp.dot(ws_ref[hd], vc, preferred_element_type=F32) + bs_ref[:, hd:hd + 1]
            gated_ref[rows, cols] = (u[:, cols] * m).astype(BF16)
        y = jnp.dot(gated_ref[rows, :], wout_ref[...], preferred_element_type=F32)
        o_ref[rows, :] = x + gt_ref[0] * y


def _sgu(x2, g, sh, sc, gt, w_in_bf, b_in, g_v, w_s_bf, b_s_t, w_out_bf, seq):
    t, d = x2.shape
    tpb = seq // SG_TM
    const2 = lambda i: (0, 0)
    perb = lambda i: (i // tpb, 0, 0)
    return pl.pallas_call(
        _sgu_kernel,
        grid=(t // SG_TM,),
        in_specs=[
            pl.BlockSpec((SG_TM, d), lambda i: (i, 0)),
            pl.BlockSpec((1, d), const2),
            pl.BlockSpec((1, 1, d), perb),
            pl.BlockSpec((1, 1, d), perb),
            pl.BlockSpec((1, 1, d), perb),
            pl.BlockSpec((d, 2 * d), const2, pipeline_mode=pl.Buffered(1)),
            pl.BlockSpec((1, 2 * d), const2),
            pl.BlockSpec((1, d), const2),
            pl.BlockSpec((SGU_HEADS, CHUNK, CHUNK), lambda i: (0, 0, 0)),
            pl.BlockSpec((CHUNK, SGU_HEADS), const2),
            pl.BlockSpec((d, d), const2, pipeline_mode=pl.Buffered(1)),
        ],
        out_specs=pl.BlockSpec((SG_TM, d), lambda i: (i, 0)),
        out_shape=jax.ShapeDtypeStruct((t, d), F32),
        scratch_shapes=[pltpu.VMEM((SG_TM, d), BF16)],
        compiler_params=_cparams(("arbitrary",)),
        name="sgu_mix",
    )(x2, g, sh, sc, gt, w_in_bf, b_in, g_v, w_s_bf, b_s_t, w_out_bf)


RT_TM = 256


def _route_kernel(x_ref, g_ref, sh_ref, sc_ref, whl_ref, br_ref, h_ref, route_ref, cnt_ref, carry_ref):
    tm = x_ref.shape[0]
    i = pl.program_id(0)

    @pl.when(i == 0)
    def _():
        carry_ref[...] = jnp.zeros_like(carry_ref)

    h = _rms(x_ref[...], g_ref[...]) * (1.0 + sc_ref[0]) + sh_ref[0]
    h_hi = h.astype(BF16)
    _store_packed(h_ref, h_hi, tm)
    h_lo = (h - h_hi.astype(F32)).astype(BF16)
    whl = whl_ref[...]
    both = jnp.dot(h_hi, whl, preferred_element_type=F32)
    lo = jnp.dot(h_lo, whl[:, :LANES], preferred_element_type=F32)
    lg = both[:, :LANES] + both[:, LANES:] + lo + br_ref[...]

    lane = lax.broadcasted_iota(jnp.int32, (tm, LANES), 1)
    lanef = lane.astype(F32)
    is_g = lane < N_GROUPS
    gl = jnp.where(is_g, lg, NEG_BIG)
    gmax = jnp.max(gl, axis=1, keepdims=True)
    gidx = jnp.min(jnp.where(gl == gmax, lanef, float(LANES)), axis=1, keepdims=True)
    gsum = jnp.sum(jnp.where(is_g, jnp.exp(gl - gmax), 0.0), axis=1, keepdims=True)
    g_w = 1.0 / gsum
    lo_lane = float(N_GROUPS) + gidx * float(E_PER_GROUP)
    in_grp = (lanef >= lo_lane) & (lanef < lo_lane + float(E_PER_GROUP))
    el = jnp.where(in_grp, lg, NEG_BIG)
    v1 = jnp.max(el, axis=1, keepdims=True)
    i1 = jnp.min(jnp.where(el == v1, lanef, float(LANES)), axis=1, keepdims=True)
    el2 = jnp.where(lanef == i1, NEG_BIG, el)
    v2 = jnp.max(el2, axis=1, keepdims=True)
    i2 = jnp.min(jnp.where(el2 == v2, lanef, float(LANES)), axis=1, keepdims=True)
    p = jnp.exp(v2 - v1)
    w0 = g_w / (1.0 + p)
    w1 = g_w * p / (1.0 + p)
    e0 = i1 - float(N_GROUPS)
    e1 = i2 - float(N_GROUPS)

    oh0 = jnp.where(lanef == e0, 1.0, 0.0)
    oh1 = jnp.where(lanef == e1, 1.0, 0.0)
    rr = lax.broadcasted_iota(jnp.int32, (tm, tm), 0)
    cc = lax.broadcasted_iota(jnp.int32, (tm, tm), 1)
    tri = jnp.where(rr > cc, 1.0, 0.0).astype(BF16)
    pre0 = jnp.dot(tri, oh0.astype(BF16), preferred_element_type=F32)
    pre1 = jnp.dot(tri, oh1.astype(BF16), preferred_element_type=F32)
    carry = carry_ref[0:1, :]
    cnt0 = jnp.sum(oh0, axis=0, keepdims=True)
    cnt1 = jnp.sum(oh1, axis=0, keepdims=True)
    rank0 = jnp.sum(oh0 * (pre0 + carry), axis=1, keepdims=True)
    rank1 = jnp.sum(oh1 * (pre1 + carry + cnt0), axis=1, keepdims=True)
    new_carry = carry + cnt0 + cnt1
    carry_ref[...] = jnp.broadcast_to(new_carry, carry_ref.shape)
    cnt_ref[...] = jnp.broadcast_to(new_carry, cnt_ref.shape)

    route = jnp.where(lane == 0, e0, 0.0)
    route = jnp.where(lane == 1, e1, route)
    route = jnp.where(lane == 2, w0, route)
    route = jnp.where(lane == 3, w1, route)
    route = jnp.where(lane == 4, rank0, route)
    route = jnp.where(lane == 5, rank1, route)
    route_ref[...] = route


def _route(x2, g, sh, sc, whl, br, seq):
    t, d = x2.shape
    tpb = seq // RT_TM
    pk = d // (2 * LANES)
    perb = lambda i: (i // tpb, 0, 0)
    return pl.pallas_call(
        _route_kernel,
        grid=(t // RT_TM,),
        in_specs=[
            pl.BlockSpec((RT_TM, d), lambda i: (i, 0)),
            pl.BlockSpec((1, d), lambda i: (0, 0)),
            pl.BlockSpec((1, 1, d), perb),
            pl.BlockSpec((1, 1, d), perb),
            pl.BlockSpec((d, 2 * LANES), lambda i: (0, 0)),
            pl.BlockSpec((1, LANES), lambda i: (0, 0)),
        ],
        out_specs=[
            pl.BlockSpec((RT_TM * pk, LANES), lambda i: (i, 0)),
            pl.BlockSpec((RT_TM, LANES), lambda i: (i, 0)),
            pl.BlockSpec((SUBLANES, LANES), lambda i: (0, 0)),
        ],
        out_shape=[
            jax.ShapeDtypeStruct((t * pk, LANES), jnp.uint32),
            jax.ShapeDtypeStruct((t, LANES), F32),
            jax.ShapeDtypeStruct((SUBLANES, LANES), F32),
        ],
        scratch_shapes=[pltpu.VMEM((SUBLANES, LANES), F32)],
        compiler_params=_cparams(("arbitrary",)),
        name="moe_route",
    )(x2, g, sh, sc, whl, br)


SC_TM = 1024


SC_UNROLL = 8


def _scatter_kernel(dest_ref, pad_start_ref, pad_n_ref, nu_ref, h_ref, buf_ref, zrow_ref, sem, zsem, tsem, *, pk):
    tm = h_ref.shape[0] // pk
    i = pl.program_id(0)
    base = i * (tm * TOP_K)
    blk_rows = EXPERT_BLOCK * pk
    n_blocks = buf_ref.shape[0] // blk_rows

    def tok_rows(tok):
        return pl.ds(pl.multiple_of(tok * pk, pk), pk)

    def row_copy(r, k):
        dst = dest_ref[base + r * TOP_K + k]
        return pltpu.make_async_copy(h_ref.at[tok_rows(r), :], buf_ref.at[tok_rows(dst), :], sem)

    def pad_fill(e, do):
        n = pad_n_ref[e]
        start = pad_start_ref[e]
        size = EXPERT_BLOCK // 2
        while size >= 1:
            has = (n & size) != 0
            cp = pltpu.make_async_copy(zrow_ref.at[pl.ds(0, size * pk), :],
                                       buf_ref.at[pl.ds(pl.multiple_of(start * pk, pk), size * pk), :], zsem)

            @pl.when(has)
            def _():
                do(cp)

            start = start + jnp.where(has, size, 0)
            size //= 2

    def tail_copy(blk):
        row0 = pl.multiple_of(blk * blk_rows, blk_rows)
        return pltpu.make_async_copy(zrow_ref, buf_ref.at[pl.ds(row0, blk_rows), :], tsem)

    @pl.when(i == 0)
    def _():
        zrow_ref[...] = jnp.zeros_like(zrow_ref)

        def per_expert(e, c):
            pad_fill(e, lambda cp: cp.start())
            return c
        lax.fori_loop(0, N_EXPERTS, per_expert, 0)
        lax.fori_loop(nu_ref[0], n_blocks, lambda blk, c: (tail_copy(blk).start(), c)[1], 0)

    def issue(q, c):
        for u in range(SC_UNROLL):
            for k in range(TOP_K):
                row_copy(q * SC_UNROLL + u, k).start(priority=k)
        return c
    lax.fori_loop(0, tm // SC_UNROLL, issue, 0)

    def drain(q, c):
        for u in range(SC_UNROLL):
            for k in range(TOP_K):
                row_copy(q * SC_UNROLL + u, k).wait()
        return c
    lax.fori_loop(0, tm // SC_UNROLL, drain, 0)

    @pl.when(i == 0)
    def _():
        def per_expert(e, c):
            pad_fill(e, lambda cp: cp.wait())
            return c
        lax.fori_loop(0, N_EXPERTS, per_expert, 0)
        lax.fori_loop(nu_ref[0], n_blocks, lambda blk, c: (tail_copy(blk).wait(), c)[1], 0)


def _scatter(dest_flat, pad_start, pad_n, n_used, hp, n_rows, pk):
    t = hp.shape[0] // pk
    grid_spec = pltpu.PrefetchScalarGridSpec(
        num_scalar_prefetch=4,
        grid=(t // SC_TM,),
        in_specs=[pl.BlockSpec((SC_TM * pk, LANES), lambda i, *_: (i, 0))],
        out_specs=pl.BlockSpec(memory_space=pl.ANY),
        scratch_shapes=[pltpu.VMEM((EXPERT_BLOCK * pk, LANES), hp.dtype), pltpu.SemaphoreType.DMA,
                        pltpu.SemaphoreType.DMA, pltpu.SemaphoreType.DMA],
    )
    return pl.pallas_call(
        functools.partial(_scatter_kernel, pk=pk),
        grid_spec=grid_spec,
        out_shape=jax.ShapeDtypeStruct((n_rows * pk, LANES), hp.dtype),
        compiler_params=_cparams(("arbitrary",)),
        name="moe_scatter",
    )(dest_flat, pad_start, pad_n, n_used, hp)


EX_RING = 4
W_PARTS = 2


def _expert_kernel(pstart_ref, nblk_ref, nu_ref, wg_hbm, wu_hbm, wd_hbm, buf_ref, yb_ref,
                   wg_st, wu_st, wd_st, wg_bf, wu_bf, wd_bf, xbuf, obuf, w_sem, in_sem, out_sem, tail_sem,
                   *, layer, pk):
    e = pl.program_id(0)
    ne = pl.num_programs(0)
    nb = nblk_ref[e]
    g0 = pstart_ref[e]
    nu = nu_ref[0]
    xrows = EXPERT_BLOCK * pk
    orows = EXPERT_BLOCK * pk
    n_blocks = yb_ref.shape[0] // orows
    cur = e % 2
    nxt = (e + 1) % 2

    def w_copies(ex, slot):
        return [pltpu.make_async_copy(hbm.at[layer, ex], st.at[slot], w_sem.at[slot])
                for hbm, st in ((wg_hbm, wg_st), (wu_hbm, wu_st), (wd_hbm, wd_st))]

    def cast_part(slot, q):
        for st, bf in ((wg_st, wg_bf), (wu_st, wu_bf), (wd_st, wd_bf)):
            n = st.shape[1] // W_PARTS
            r = pl.ds(pl.multiple_of(q * n, n), n)
            bf[slot, r, :] = st[slot, r, :].astype(BF16)

    @pl.when(e == 0)
    def _():
        for c in w_copies(0, 0):
            c.start()
        for c in w_copies(1, 1):
            c.start()
        for c in w_copies(0, 0):
            c.wait()
        for q in range(W_PARTS):
            cast_part(0, q)
        for c in w_copies(2, 0):
            c.start()

    @pl.when((e >= 1) & (e + 2 < ne))
    def _():
        for c in w_copies(e + 2, cur):
            c.start()

    @pl.when(e + 1 < ne)
    def _():
        for c in w_copies(e + 1, nxt):
            c.wait()

    def x_copy(g):
        slot = g % EX_RING
        src = buf_ref.at[pl.ds(pl.multiple_of(g * xrows, xrows), xrows), :]
        return pltpu.make_async_copy(src, xbuf.at[slot], in_sem.at[slot])

    def o_copy(g):
        slot = g % EX_RING
        dst = yb_ref.at[pl.ds(pl.multiple_of(g * orows, orows), orows), :]
        return pltpu.make_async_copy(obuf.at[slot], dst, out_sem.at[slot])

    @pl.when(e == 0)
    def _():
        for k in range(2):
            @pl.when(k < nu)
            def _():
                x_copy(k).start()

    def stage(g, n):
        for k in range(n):
            @pl.when(g + 2 + k < nu)
            def _():
                x_copy(g + 2 + k).start()
        for k in range(n):
            x_copy(g + k).wait()

            @pl.when(g + k >= EX_RING)
            def _():
                o_copy(g + k - EX_RING).wait()

    def compute(g):
        slot = g % EX_RING
        xb = _load_packed(xbuf.at[slot], EXPERT_BLOCK, pk, BF16)
        gte = jnp.dot(xb, wg_bf[cur], preferred_element_type=F32)
        up = jnp.dot(xb, wu_bf[cur], preferred_element_type=F32)
        act = gte * (1.0 / (1.0 + jnp.exp(-gte))) * up
        y = jnp.dot(act.astype(BF16), wd_bf[cur], preferred_element_type=F32)
        _store_packed(obuf.at[slot], y.astype(BF16), EXPERT_BLOCK)

    def pair_body(p, c):
        g = g0 + 2 * p
        stage(g, 2)
        compute(g)
        compute(g + 1)
        o_copy(g).start()
        o_copy(g + 1).start()
        cast_part(nxt, jnp.minimum(p, W_PARTS - 1))
        return c

    n_pairs = nb // 2
    lax.fori_loop(0, n_pairs, pair_body, 0)

    @pl.when(nb % 2 == 1)
    def _():
        g = g0 + nb - 1
        stage(g, 1)
        compute(g)
        o_copy(g).start()

    lax.fori_loop(jnp.minimum(n_pairs, W_PARTS), W_PARTS, lambda q, c: (cast_part(nxt, q), c)[1], 0)

    @pl.when(e == ne - 1)
    def _():
        lax.fori_loop(jnp.maximum(nu - EX_RING, 0), nu, lambda g, c: (o_copy(g).wait(), c)[1], 0)
        obuf[0] = jnp.zeros(obuf.shape[1:], obuf.dtype)

        def tail_copy(blk):
            r = pl.ds(pl.multiple_of(blk * orows, orows), orows)
            return pltpu.make_async_copy(obuf.at[0], yb_ref.at[r, :], tail_sem)

        lax.fori_loop(nu, n_blocks, lambda blk, c: (tail_copy(blk).start(), c)[1], 0)
        lax.fori_loop(nu, n_blocks, lambda blk, c: (tail_copy(blk).wait(), c)[1], 0)


def _experts(gstarts, nblk, n_used, buf, w_gate, w_up, w_down, layer, n_rows, pk):
    d, de = w_gate.shape[-2:]
    assert N_EXPERTS >= 3 and d % W_PARTS == 0 and de % W_PARTS == 0
    hbm = pl.BlockSpec(memory_space=pl.ANY)
    grid_spec = pltpu.PrefetchScalarGridSpec(
        num_scalar_prefetch=3,
        grid=(N_EXPERTS,),
        in_specs=[hbm, hbm, hbm, hbm],
        out_specs=hbm,
        scratch_shapes=[
            pltpu.VMEM((2, d, de), F32), pltpu.VMEM((2, d, de), F32), pltpu.VMEM((2, de, d), F32),
            pltpu.VMEM((2, d, de), BF16), pltpu.VMEM((2, d, de), BF16), pltpu.VMEM((2, de, d), BF16),
            pltpu.VMEM((EX_RING, EXPERT_BLOCK * pk, LANES), buf.dtype),
            pltpu.VMEM((EX_RING, EXPERT_BLOCK * pk, LANES), buf.dtype),
            pltpu.SemaphoreType.DMA((2,)), pltpu.SemaphoreType.DMA((EX_RING,)), pltpu.SemaphoreType.DMA((EX_RING,)),
            pltpu.SemaphoreType.DMA,
        ],
    )
    return pl.pallas_call(
        functools.partial(_expert_kernel, layer=layer, pk=pk),
        grid_spec=grid_spec,
        out_shape=jax.ShapeDtypeStruct((n_rows * pk, LANES), buf.dtype),
        compiler_params=_cparams(("arbitrary",)),
        name="moe_experts",
    )(gstarts, nblk, n_used, w_gate, w_up, w_down, buf)


CB_TM = 256


def _combine_kernel(dest_ref, x_ref, route_ref, gt_ref, gf_ref, yb_ref, o_ref, ybuf, sems, *, final_norm):
    tm, d = x_ref.shape
    fs = d // (2 * LANES)
    i = pl.program_id(0)
    n = pl.num_programs(0)

    def row_copy(tile, slot, r, k):
        src = dest_ref[(tile * tm + r) * TOP_K + k]
        return pltpu.make_async_copy(yb_ref.at[pl.ds(pl.multiple_of(src * fs, fs), fs), :],
                                     ybuf.at[slot, k, pl.ds(pl.multiple_of(r * fs, fs), fs), :], sems.at[slot])

    def issue(tile, slot):
        def body(q, c):
            for u in range(SC_UNROLL):
                for k in range(TOP_K):
                    row_copy(tile, slot, q * SC_UNROLL + u, k).start(priority=k)
            return c
        lax.fori_loop(0, tm // SC_UNROLL, body, 0)

    @pl.when(i == 0)
    def _():
        issue(0, 0)

    @pl.when(i + 1 < n)
    def _():
        issue(i + 1, (i + 1) % 2)

    slot = i % 2

    def drain(q, c):
        for u in range(SC_UNROLL):
            for k in range(TOP_K):
                row_copy(i, slot, q * SC_UNROLL + u, k).wait()
        return c
    lax.fori_loop(0, tm // SC_UNROLL, drain, 0)

    route = route_ref[...]
    ys = [_load_packed(ybuf.at[slot, k], tm, fs, F32) for k in range(TOP_K)]
    y = ys[0] * route[:, 2:3] + ys[1] * route[:, 3:4]
    xn = x_ref[...] + gt_ref[0] * y
    if final_norm:
        xn = _rms(xn, gf_ref[...])
    o_ref[...] = xn


def _combine(dest_flat, x2, route, gt, g_final, yb, seq, final_norm):
    t, d = x2.shape
    tpb = seq // CB_TM
    grid_spec = pltpu.PrefetchScalarGridSpec(
        num_scalar_prefetch=1,
        grid=(t // CB_TM,),
        in_specs=[
            pl.BlockSpec((CB_TM, d), lambda i, ds: (i, 0)),
            pl.BlockSpec((CB_TM, LANES), lambda i, ds: (i, 0)),
            pl.BlockSpec((1, 1, d), lambda i, ds: (i // tpb, 0, 0)),
            pl.BlockSpec((1, d), lambda i, ds: (0, 0)),
            pl.BlockSpec(memory_space=pl.ANY),
        ],
        out_specs=pl.BlockSpec((CB_TM, d), lambda i, ds: (i, 0)),
        scratch_shapes=[pltpu.VMEM((2, TOP_K, CB_TM * (d // (2 * LANES)), LANES), yb.dtype),
                        pltpu.SemaphoreType.DMA((2,))],
    )
    return pl.pallas_call(
        functools.partial(_combine_kernel, final_norm=final_norm),
        grid_spec=grid_spec,
        out_shape=jax.ShapeDtypeStruct((t, d), F32),
        compiler_params=_cparams(("arbitrary",)),
        name="moe_combine",
    )(dest_flat, x2, route, gt, g_final, yb)


def _dft_cos_sin(n):
    k = np.arange(n, dtype=np.int64)
    ang = (np.outer(k, k) % n).astype(np.float64) * (2.0 * np.pi / n)
    return np.cos(ang), np.sin(ang)


@functools.lru_cache(maxsize=None)
def _dft_constants(seq, dg):
    cc, sc = _dft_cos_sin(dg)
    cs_chan = np.concatenate([cc, sc], axis=1).astype(np.float32)
    k = np.arange(seq // 2, dtype=np.int64)[:, None]
    parts = []
    for par in range(2):
        n = 2 * np.arange(seq // 2, dtype=np.int64)[None, :] + par
        ang = ((k * n) % seq).astype(np.float64) * (2.0 * np.pi / seq)
        parts.append(np.concatenate([np.cos(ang), -np.sin(ang)], axis=1))
    cs_seq = np.stack(parts, axis=0).astype(np.float32)
    return cs_chan.astype(BF16), cs_seq.astype(BF16)


def _moe(x2, g2, sh, sc, gt, w_group, b_group, w_router, b_router, w_gate, w_up, w_down, layer,
         g_final, seq, final_norm):
    t, d = x2.shape
    w_all = jnp.concatenate([w_group, w_router], axis=1)
    w_all = jnp.pad(w_all, ((0, 0), (0, LANES - w_all.shape[1])))
    w_hi = w_all.astype(BF16)
    w_lo = (w_all - w_hi.astype(F32)).astype(BF16)
    whl = jnp.concatenate([w_hi, w_lo], axis=1)
    br = jnp.pad(jnp.concatenate([b_group, b_router]), (0, LANES - N_GROUPS - N_EXPERTS)).reshape(1, LANES)

    pk = d // (2 * LANES)
    hp, route, cnt = _route(x2, g2, sh, sc, whl, br, seq)

    counts = cnt[0, :N_EXPERTS].astype(jnp.int32)
    padded = (counts + EXPERT_BLOCK - 1) // EXPERT_BLOCK * EXPERT_BLOCK
    pends = jnp.cumsum(padded)
    pstarts = pends - padded
    n_assign = t * TOP_K
    n_blocks = (n_assign + N_EXPERTS * (EXPERT_BLOCK - 1) + EXPERT_BLOCK - 1) // EXPERT_BLOCK
    n_rows = n_blocks * EXPERT_BLOCK
    e_idx = route[:, 0:TOP_K].astype(jnp.int32)
    rank = route[:, 4:4 + TOP_K].astype(jnp.int32)
    is_e = e_idx[:, :, None] == jnp.arange(N_EXPERTS, dtype=jnp.int32)
    dest_flat = (jnp.sum(jnp.where(is_e, pstarts, 0), axis=-1) + rank).reshape(n_assign)
    n_used = (pends[-1:] // EXPERT_BLOCK).astype(jnp.int32)

    buf = _scatter(dest_flat, (pstarts + counts).astype(jnp.int32), (padded - counts).astype(jnp.int32), n_used,
                   hp, n_rows, pk)
    yb = _experts((pstarts // EXPERT_BLOCK).astype(jnp.int32), (padded // EXPERT_BLOCK).astype(jnp.int32), n_used,
                  buf, w_gate, w_up, w_down, layer, n_rows, pk)
    return _combine(dest_flat, x2, route, gt, g_final, yb, seq, final_norm)


def kernel(x, c, g_norm1, g_norm2, w_ada, b_ada, fa_w_in, fa_w_out, sg_w_in, sg_b_in, sg_g_v, sg_w_s, sg_b_s, sg_w_out, w_group, b_group, w_router, b_router, w_gate, w_up, w_down, g_final):
    n_batch, seq, d = x.shape
    depth = w_ada.shape[0]
    t = n_batch * seq
    x2 = x.reshape(t, d)

    mod = _ada(c, w_ada, b_ada)
    cs_chan, cs_seq = _dft_constants(seq, d // F_GROUPS)
    gfin = g_final.reshape(1, d)

    for l in range(depth):
        parts = [mod[l, :n_batch, k * d:(k + 1) * d].reshape(n_batch, 1, d) for k in range(6)]
        sh1, sc1, gt1, sh2, sc2, gt2 = parts
        g1 = g_norm1[l].reshape(1, d)
        j = l // 2
        if l % 2 == 0:
            ab = _fnet_in(x2, g1, sh1, sc1, fa_w_in[j].astype(BF16), jnp.asarray(cs_chan), n_batch, seq)
            x2 = _fnet_seq(jnp.asarray(cs_seq), ab, fa_w_out[j].astype(BF16), x2, gt1, n_batch, seq)
        else:
            x2 = _sgu(x2, g1, sh1, sc1, gt1, sg_w_in[j].astype(BF16), sg_b_in[j].reshape(1, 2 * d),
                      sg_g_v[j].reshape(1, d), sg_w_s[j].astype(BF16), sg_b_s[j].T, sg_w_out[j].astype(BF16), seq)
        x2 = _moe(x2, g_norm2[l].reshape(1, d), sh2, sc2, gt2, w_group[l], b_group[l], w_router[l], b_router[l],
                  w_gate, w_up, w_down, l, gfin, seq, final_norm=(l == depth - 1))
    return x2.reshape(n_batch, seq, d)
```

```python
import functools

import numpy as np
import jax
import jax.numpy as jnp
from jax import lax
from jax.experimental import pallas as pl
from jax.experimental.pallas import tpu as pltpu

F32 = jnp.float32
BF16 = jnp.bfloat16

EPS = 1e-6
F_GROUPS = 4
SGU_HEADS = 8
CHUNK = 128
N_GROUPS = 4
E_PER_GROUP = 8
N_EXPERTS = N_GROUPS * E_PER_GROUP
TOP_K = 2
EXPERT_BLOCK = 256

LANES = 128
SUBLANES = 8
VMEM_LIMIT = 56 * 1024 * 1024
NEG_BIG = -1e30


def _cparams(sem, vmem=VMEM_LIMIT):
    return pltpu.CompilerParams(dimension_semantics=sem, vmem_limit_bytes=vmem)


def _rms(x, g):
    return x * lax.rsqrt(jnp.mean(x * x, axis=-1, keepdims=True) + EPS) * g


def _store_packed(ref, vals_bf16, n_tok):
    pk = vals_bf16.shape[1] // (2 * LANES)
    bits = pltpu.bitcast(vals_bf16.astype(F32), jnp.uint32)
    for s in range(pk):
        low = bits[:, s * LANES:(s + 1) * LANES]
        high = bits[:, (s + pk) * LANES:(s + pk + 1) * LANES]
        ref[pl.ds(s, n_tok, stride=pk), :] = (high & jnp.uint32(0xFFFF0000)) | (low >> jnp.uint32(16))


def _load_packed(ref, n_tok, pk, dtype):
    lows, highs = [], []
    for s in range(pk):
        w = ref[pl.ds(s, n_tok, stride=pk), :]
        lows.append(pltpu.bitcast(w << jnp.uint32(16), F32).astype(dtype))
        highs.append(pltpu.bitcast(w & jnp.uint32(0xFFFF0000), F32).astype(dtype))
    return jnp.concatenate(lows + highs, axis=1)


ADA_TN = 512


def _ada_kernel(c_ref, w_ref, b_ref, o_ref, lhs_ref):
    @pl.when((pl.program_id(0) == 0) & (pl.program_id(1) == 0))
    def _():
        cv = c_ref[...]
        s = cv / (1.0 + jnp.exp(-cv))
        s_hi = s.astype(BF16).astype(F32)
        lhs_ref[...] = jnp.concatenate([s_hi, s - s_hi], axis=0).astype(BF16)

    w = w_ref[0]
    w_hi = w.astype(BF16)
    w_lo = (w - w_hi.astype(F32)).astype(BF16)
    lhs = lhs_ref[...]
    r = jnp.dot(lhs, w_hi, preferred_element_type=F32) + jnp.dot(lhs, w_lo, preferred_element_type=F32)
    o_ref[0] = r[:SUBLANES] + r[SUBLANES:] + b_ref[0]


def _ada(c, w_ada, b_ada):
    n_batch, d = c.shape
    depth, _, n6 = w_ada.shape
    assert n_batch <= SUBLANES
    c8 = jnp.pad(c, ((0, SUBLANES - n_batch), (0, 0)))
    return pl.pallas_call(
        _ada_kernel,
        grid=(depth, n6 // ADA_TN),
        in_specs=[
            pl.BlockSpec((SUBLANES, d), lambda l, j: (0, 0)),
            pl.BlockSpec((1, d, ADA_TN), lambda l, j: (l, 0, j)),
            pl.BlockSpec((1, 1, ADA_TN), lambda l, j: (l, 0, j)),
        ],
        out_specs=pl.BlockSpec((1, SUBLANES, ADA_TN), lambda l, j: (l, 0, j)),
        out_shape=jax.ShapeDtypeStruct((depth, SUBLANES, n6), F32),
        scratch_shapes=[pltpu.VMEM((2 * SUBLANES, d), BF16)],
        compiler_params=_cparams(("arbitrary", "arbitrary")),
        name="ada_mod",
    )(c8, w_ada, b_ada.reshape(depth, 1, n6))


FN_TM = 256


def _fnet_in_kernel(x_ref, g_ref, sh_ref, sc_ref, win_ref, cs_ref, ab_ref, xs_ref):
    tm, d = x_ref.shape
    dg = d // F_GROUPS
    half = tm // 2
    nch = d // LANES
    for c in range(nch):
        xs_ref[c] = x_ref[:, c * LANES:(c + 1) * LANES]
    x = jnp.concatenate(
        [jnp.concatenate([xs_ref[c, pl.ds(par, half, stride=2), :] for c in range(nch)], axis=1)
         for par in range(2)], axis=0)
    h = _rms(x, g_ref[...]) * (1.0 + sc_ref[0]) + sh_ref[0]
    z = jnp.dot(h.astype(BF16), win_ref[...], preferred_element_type=F32)
    hk = dg // 2
    pw = F_GROUPS * hk + LANES
    lane = lax.broadcasted_iota(jnp.int32, (1, dg), 1)
    alt = (1 - 2 * (lane & 1)).astype(F32)
    lane_n = lax.broadcasted_iota(jnp.int32, (tm, LANES), 1)
    nyq = jnp.zeros((tm, LANES), F32)
    for g in range(F_GROUPS):
        zg = z[:, g * dg:(g + 1) * dg]
        ab = jnp.dot(zg.astype(BF16), cs_ref[...], preferred_element_type=F32)
        nyq = jnp.where(lane_n == g, jnp.sum(zg * alt, axis=-1, keepdims=True), nyq)
        for par in range(2):
            rows = slice(par * half, (par + 1) * half)
            ab_ref[0, par, :, g * hk:(g + 1) * hk] = ab[rows, :hk].astype(BF16)
            ab_ref[0, par, :, pw + g * hk:pw + (g + 1) * hk] = ab[rows, hk:].astype(BF16)
    for par in range(2):
        ab_ref[0, par, :, F_GROUPS * hk:pw] = nyq[par * half:(par + 1) * half].astype(BF16)


def _fnet_in(x2, g, sh, sc, w_in_bf, cs_chan, n_batch, seq):
    t, d = x2.shape
    tpb = seq // FN_TM
    dg = d // F_GROUPS
    rw = d + LANES
    return pl.pallas_call(
        _fnet_in_kernel,
        grid=(t // FN_TM,),
        in_specs=[
            pl.BlockSpec((FN_TM, d), lambda i: (i, 0)),
            pl.BlockSpec((1, d), lambda i: (0, 0)),
            pl.BlockSpec((1, 1, d), lambda i: (i // tpb, 0, 0)),
            pl.BlockSpec((1, 1, d), lambda i: (i // tpb, 0, 0)),
            pl.BlockSpec((d, d), lambda i: (0, 0), pipeline_mode=pl.Buffered(1)),
            pl.BlockSpec((dg, dg), lambda i: (0, 0), pipeline_mode=pl.Buffered(1)),
        ],
        out_specs=pl.BlockSpec((1, 2, FN_TM // 2, rw), lambda i: (i // tpb, 0, i % tpb, 0)),
        out_shape=jax.ShapeDtypeStruct((n_batch, 2, seq // 2, rw), BF16),
        scratch_shapes=[pltpu.VMEM((d // LANES, FN_TM, LANES), F32)],
        compiler_params=_cparams(("arbitrary",)),
        name="fnet_in",
    )(x2, g, sh, sc, w_in_bf, cs_chan)


FS_TK = 256
FS_SUB = 128


def _fnet_seq_kernel(cs_ref, ab_ref, wout_ref, x_ref, gt_ref, o_ref, *, scale):
    tk = cs_ref.shape[2]
    rw = ab_ref.shape[3]
    pw = rw - wout_ref.shape[1] // 2
    sub = FS_SUB
    for c in range(tk // sub):
        rows = slice(c * sub, (c + 1) * sub)
        pe = jnp.dot(cs_ref[0, 0, rows, :], ab_ref[0, 0, :, :pw], preferred_element_type=F32)
        po = jnp.dot(cs_ref[1, 0, rows, :], ab_ref[0, 1, :, :pw], preferred_element_type=F32)
        qe = jnp.dot(cs_ref[0, 1, rows, :], ab_ref[0, 0, :, pw:], preferred_element_type=F32)
        qo = jnp.dot(cs_ref[1, 1, rows, :], ab_ref[0, 1, :, pw:], preferred_element_type=F32)
        lo = jnp.concatenate([pe + po, qe + qo], axis=1)
        hi = jnp.concatenate([pe - po, qe - qo], axis=1)
        f = (jnp.concatenate([lo, hi], axis=0) * scale).astype(BF16)
        y = jnp.dot(f, wout_ref[...], preferred_element_type=F32)
        o_ref[0, 0, rows, :] = x_ref[0, 0, rows, :] + gt_ref[0] * y[:sub]
        o_ref[0, 1, rows, :] = x_ref[0, 1, rows, :] + gt_ref[0] * y[sub:]


def _fold_w_out(w_out):
    d = w_out.shape[0]
    dg = d // F_GROUPS
    hk = dg // 2
    wg = w_out.reshape(F_GROUPS, dg, d)
    wm = jnp.roll(wg[:, ::-1, :], 1, axis=1)
    k = jnp.arange(hk)[None, :, None]
    wp = jnp.where(k == 0, wg[:, :hk], wg[:, :hk] + wm[:, :hk])
    wq = wm[:, :hk] - wg[:, :hk]
    wn = jnp.pad(wg[:, hk, :], ((0, LANES - F_GROUPS), (0, 0)))
    return jnp.concatenate([wp.reshape(F_GROUPS * hk, d), wn, wq.reshape(F_GROUPS * hk, d)], axis=0).astype(BF16)


def _fnet_seq(cs_seq, ab, w_fold_bf, x2, gt, n_batch, seq):
    t, d = x2.shape
    hs = seq // 2
    rw = ab.shape[-1]
    scale = 1.0 / float(np.sqrt(seq * (d // F_GROUPS)))
    x4 = x2.reshape(n_batch, 2, hs, d)
    out = pl.pallas_call(
        functools.partial(_fnet_seq_kernel, scale=scale),
        grid=(n_batch, hs // FS_TK),
        in_specs=[
            pl.BlockSpec((2, 2, FS_TK, hs), lambda b, k: (0, 0, k, 0)),
            pl.BlockSpec((1, 2, hs, rw), lambda b, k: (b, 0, 0, 0), pipeline_mode=pl.Buffered(1)),
            pl.BlockSpec((rw, d), lambda b, k: (0, 0), pipeline_mode=pl.Buffered(1)),
            pl.BlockSpec((1, 2, FS_TK, d), lambda b, k: (b, 0, k, 0)),
            pl.BlockSpec((1, 1, d), lambda b, k: (b, 0, 0)),
        ],
        out_specs=pl.BlockSpec((1, 2, FS_TK, d), lambda b, k: (b, 0, k, 0)),
        out_shape=jax.ShapeDtypeStruct((n_batch, 2, hs, d), F32),
        compiler_params=_cparams(("arbitrary", "arbitrary")),
        name="fnet_seq",
    )(cs_seq, ab, w_fold_bf, x4, gt)
    return out.reshape(t, d)


SG_TM = 256


def _gelu_tanh(x):
    c = float(np.sqrt(2.0 / np.pi))
    return x * (0.5 * (1.0 + jnp.tanh(c * (x + 0.044715 * (x * x * x)))))


def _sgu_kernel(x_ref, g_ref, sh_ref, sc_ref, gt_ref, win_ref, bin_ref, gv_ref, ws_ref, bs_ref,
                wout_ref, o_ref, gated_ref):
    tm, d = x_ref.shape
    dh = d // SGU_HEADS
    for c in range(tm // CHUNK):
        rows = slice(c * CHUNK, (c + 1) * CHUNK)
        x = x_ref[rows, :]
        h = _rms(x, g_ref[...]) * (1.0 + sc_ref[0]) + sh_ref[0]
        z = jnp.dot(h.astype(BF16), win_ref[...], preferred_element_type=F32) + bin_ref[...]
        z = _gelu_tanh(z)
        u = z[:, :d]
        v = _rms(z[:, d:], gv_ref[...])
        for hd in range(SGU_HEADS):
            cols = slice(hd * dh, (hd + 1) * dh)
            vc = v[:, cols].astype(BF16)
            m = jnp.dot(ws_ref[hd], vc, preferred_element_type=F32) + bs_ref[:, hd:hd + 1]
            gated_ref[rows, cols] = (u[:, cols] * m).astype(BF16)
        y = jnp.dot(gated_ref[rows, :], wout_ref[...], preferred_element_type=F32)
        o_ref[rows, :] = x + gt_ref[0] * y


def _sgu(x2, g, sh, sc, gt, w_in_bf, b_in, g_v, w_s_bf, b_s_t, w_out_bf, seq):
    t, d = x2.shape
    tpb = seq // SG_TM
    const2 = lambda i: (0, 0)
    perb = lambda i: (i // tpb, 0, 0)
    return pl.pallas_call(
        _sgu_kernel,
        grid=(t // SG_TM,),
        in_specs=[
            pl.BlockSpec((SG_TM, d), lambda i: (i, 0)),
            pl.BlockSpec((1, d), const2),
            pl.BlockSpec((1, 1, d), perb),
            pl.BlockSpec((1, 1, d), perb),
            pl.BlockSpec((1, 1, d), perb),
            pl.BlockSpec((d, 2 * d), const2, pipeline_mode=pl.Buffered(1)),
            pl.BlockSpec((1, 2 * d), const2),
            pl.BlockSpec((1, d), const2),
            pl.BlockSpec((SGU_HEADS, CHUNK, CHUNK), lambda i: (0, 0, 0)),
            pl.BlockSpec((CHUNK, SGU_HEADS), const2),
            pl.BlockSpec((d, d), const2, pipeline_mode=pl.Buffered(1)),
        ],
        out_specs=pl.BlockSpec((SG_TM, d), lambda i: (i, 0)),
        out_shape=jax.ShapeDtypeStruct((t, d), F32),
        scratch_shapes=[pltpu.VMEM((SG_TM, d), BF16)],
        compiler_params=_cparams(("arbitrary",)),
        name="sgu_mix",
    )(x2, g, sh, sc, gt, w_in_bf, b_in, g_v, w_s_bf, b_s_t, w_out_bf)


RT_TM = 256


def _route_kernel(x_ref, g_ref, sh_ref, sc_ref, whl_ref, br_ref, h_ref, route_ref, cnt_ref, carry_ref):
    tm = x_ref.shape[0]
    i = pl.program_id(0)

    @pl.when(i == 0)
    def _():
        carry_ref[...] = jnp.zeros_like(carry_ref)

    h = _rms(x_ref[...], g_ref[...]) * (1.0 + sc_ref[0]) + sh_ref[0]
    h_hi = h.astype(BF16)
    _store_packed(h_ref, h_hi, tm)
    h_lo = (h - h_hi.astype(F32)).astype(BF16)
    whl = whl_ref[...]
    both = jnp.dot(h_hi, whl, preferred_element_type=F32)
    lo = jnp.dot(h_lo, whl[:, :LANES], preferred_element_type=F32)
    lg = both[:, :LANES] + both[:, LANES:] + lo + br_ref[...]

    lane = lax.broadcasted_iota(jnp.int32, (tm, LANES), 1)
    lanef = lane.astype(F32)
    is_g = lane < N_GROUPS
    gl = jnp.where(is_g, lg, NEG_BIG)
    gmax = jnp.max(gl, axis=1, keepdims=True)
    gidx = jnp.min(jnp.where(gl == gmax, lanef, float(LANES)), axis=1, keepdims=True)
    gsum = jnp.sum(jnp.where(is_g, jnp.exp(gl - gmax), 0.0), axis=1, keepdims=True)
    g_w = 1.0 / gsum
    lo_lane = float(N_GROUPS) + gidx * float(E_PER_GROUP)
    in_grp = (lanef >= lo_lane) & (lanef < lo_lane + float(E_PER_GROUP))
    el = jnp.where(in_grp, lg, NEG_BIG)
    v1 = jnp.max(el, axis=1, keepdims=True)
    i1 = jnp.min(jnp.where(el == v1, lanef, float(LANES)), axis=1, keepdims=True)
    el2 = jnp.where(lanef == i1, NEG_BIG, el)
    v2 = jnp.max(el2, axis=1, keepdims=True)
    i2 = jnp.min(jnp.where(el2 == v2, lanef, float(LANES)), axis=1, keepdims=True)
    p = jnp.exp(v2 - v1)
    w0 = g_w / (1.0 + p)
    w1 = g_w * p / (1.0 + p)
    e0 = i1 - float(N_GROUPS)
    e1 = i2 - float(N_GROUPS)

    oh0 = jnp.where(lanef == e0, 1.0, 0.0)
    oh1 = jnp.where(lanef == e1, 1.0, 0.0)
    rr = lax.broadcasted_iota(jnp.int32, (tm, tm), 0)
    cc = lax.broadcasted_iota(jnp.int32, (tm, tm), 1)
    tri = jnp.where(rr > cc, 1.0, 0.0).astype(BF16)
    pre0 = jnp.dot(tri, oh0.astype(BF16), preferred_element_type=F32)
    pre1 = jnp.dot(tri, oh1.astype(BF16), preferred_element_type=F32)
    carry = carry_ref[0:1, :]
    cnt0 = jnp.sum(oh0, axis=0, keepdims=True)
    cnt1 = jnp.sum(oh1, axis=0, keepdims=True)
    rank0 = jnp.sum(oh0 * (pre0 + carry), axis=1, keepdims=True)
    rank1 = jnp.sum(oh1 * (pre1 + carry + cnt0), axis=1, keepdims=True)
    new_carry = carry + cnt0 + cnt1
    carry_ref[...] = jnp.broadcast_to(new_carry, carry_ref.shape)
    cnt_ref[...] = jnp.broadcast_to(new_carry, cnt_ref.shape)

    route = jnp.where(lane == 0, e0, 0.0)
    route = jnp.where(lane == 1, e1, route)
    route = jnp.where(lane == 2, w0, route)
    route = jnp.where(lane == 3, w1, route)
    route = jnp.where(lane == 4, rank0, route)
    route = jnp.where(lane == 5, rank1, route)
    route_ref[...] = route


def _route(x2, g, sh, sc, whl, br, seq):
    t, d = x2.shape
    tpb = seq // RT_TM
    pk = d // (2 * LANES)
    perb = lambda i: (i // tpb, 0, 0)
    return pl.pallas_call(
        _route_kernel,
        grid=(t // RT_TM,),
        in_specs=[
            pl.BlockSpec((RT_TM, d), lambda i: (i, 0)),
            pl.BlockSpec((1, d), lambda i: (0, 0)),
            pl.BlockSpec((1, 1, d), perb),
            pl.BlockSpec((1, 1, d), perb),
            pl.BlockSpec((d, 2 * LANES), lambda i: (0, 0)),
            pl.BlockSpec((1, LANES), lambda i: (0, 0)),
        ],
        out_specs=[
            pl.BlockSpec((RT_TM * pk, LANES), lambda i: (i, 0)),
            pl.BlockSpec((RT_TM, LANES), lambda i: (i, 0)),
            pl.BlockSpec((SUBLANES, LANES), lambda i: (0, 0)),
        ],
        out_shape=[
            jax.ShapeDtypeStruct((t * pk, LANES), jnp.uint32),
            jax.ShapeDtypeStruct((t, LANES), F32),
            jax.ShapeDtypeStruct((SUBLANES, LANES), F32),
        ],
        scratch_shapes=[pltpu.VMEM((SUBLANES, LANES), F32)],
        compiler_params=_cparams(("arbitrary",)),
        name="moe_route",
    )(x2, g, sh, sc, whl, br)


SC_TM = 1024


SC_UNROLL = 8


def _scatter_kernel(dest_ref, pad_start_ref, pad_n_ref, nu_ref, h_ref, buf_ref, zrow_ref, sem, zsem, tsem, *, pk):
    tm = h_ref.shape[0] // pk
    i = pl.program_id(0)
    base = i * (tm * TOP_K)
    blk_rows = EXPERT_BLOCK * pk
    n_blocks = buf_ref.shape[0] // blk_rows

    def tok_rows(tok):
        return pl.ds(pl.multiple_of(tok * pk, pk), pk)

    def row_copy(r, k):
        dst = dest_ref[base + r * TOP_K + k]
        return pltpu.make_async_copy(h_ref.at[tok_rows(r), :], buf_ref.at[tok_rows(dst), :], sem)

    def pad_fill(e, do):
        n = pad_n_ref[e]
        start = pad_start_ref[e]
        size = EXPERT_BLOCK // 2
        while size >= 1:
            has = (n & size) != 0
            cp = pltpu.make_async_copy(zrow_ref.at[pl.ds(0, size * pk), :],
                                       buf_ref.at[pl.ds(pl.multiple_of(start * pk, pk), size * pk), :], zsem)

            @pl.when(has)
            def _():
                do(cp)

            start = start + jnp.where(has, size, 0)
            size //= 2

    def tail_copy(blk):
        row0 = pl.multiple_of(blk * blk_rows, blk_rows)
        return pltpu.make_async_copy(zrow_ref, buf_ref.at[pl.ds(row0, blk_rows), :], tsem)

    @pl.when(i == 0)
    def _():
        zrow_ref[...] = jnp.zeros_like(zrow_ref)

        def per_expert(e, c):
            pad_fill(e, lambda cp: cp.start())
            return c
        lax.fori_loop(0, N_EXPERTS, per_expert, 0)
        lax.fori_loop(nu_ref[0], n_blocks, lambda blk, c: (tail_copy(blk).start(), c)[1], 0)

    def issue(q, c):
        for u in range(SC_UNROLL):
            for k in range(TOP_K):
                row_copy(q * SC_UNROLL + u, k).start(priority=k)
        return c
    lax.fori_loop(0, tm // SC_UNROLL, issue, 0)

    def drain(q, c):
        for u in range(SC_UNROLL):
            for k in range(TOP_K):
                row_copy(q * SC_UNROLL + u, k).wait()
        return c
    lax.fori_loop(0, tm // SC_UNROLL, drain, 0)

    @pl.when(i == 0)
    def _():
        def per_expert(e, c):
            pad_fill(e, lambda cp: cp.wait())
            return c
        lax.fori_loop(0, N_EXPERTS, per_expert, 0)
        lax.fori_loop(nu_ref[0], n_blocks, lambda blk, c: (tail_copy(blk).wait(), c)[1], 0)


def _scatter(dest_flat, pad_start, pad_n, n_used, hp, n_rows, pk):
    t = hp.shape[0] // pk
    grid_spec = pltpu.PrefetchScalarGridSpec(
        num_scalar_prefetch=4,
        grid=(t // SC_TM,),
        in_specs=[pl.BlockSpec((SC_TM * pk, LANES), lambda i, *_: (i, 0))],
        out_specs=pl.BlockSpec(memory_space=pl.ANY),
        scratch_shapes=[pltpu.VMEM((EXPERT_BLOCK * pk, LANES), hp.dtype), pltpu.SemaphoreType.DMA,
                        pltpu.SemaphoreType.DMA, pltpu.SemaphoreType.DMA],
    )
    return pl.pallas_call(
        functools.partial(_scatter_kernel, pk=pk),
        grid_spec=grid_spec,
        out_shape=jax.ShapeDtypeStruct((n_rows * pk, LANES), hp.dtype),
        compiler_params=_cparams(("arbitrary",)),
        name="moe_scatter",
    )(dest_flat, pad_start, pad_n, n_used, hp)


EX_RING = 4
W_PARTS = 2


def _expert_kernel(pstart_ref, nblk_ref, nu_ref, wg_hbm, wu_hbm, wd_hbm, buf_ref, yb_ref,
                   wg_st, wu_st, wd_st, wg_bf, wu_bf, wd_bf, xbuf, obuf, w_sem, in_sem, out_sem, tail_sem,
                   *, layer, pk):
    e = pl.program_id(0)
    ne = pl.num_programs(0)
    nb = nblk_ref[e]
    g0 = pstart_ref[e]
    nu = nu_ref[0]
    xrows = EXPERT_BLOCK * pk
    orows = EXPERT_BLOCK * pk
    n_blocks = yb_ref.shape[0] // orows
    cur = e % 2
    nxt = (e + 1) % 2

    def w_copies(ex, slot):
        return [pltpu.make_async_copy(hbm.at[layer, ex], st.at[slot], w_sem.at[slot])
                for hbm, st in ((wg_hbm, wg_st), (wu_hbm, wu_st), (wd_hbm, wd_st))]

    def cast_part(slot, q):
        for st, bf in ((wg_st, wg_bf), (wu_st, wu_bf), (wd_st, wd_bf)):
            n = st.shape[1] // W_PARTS
            r = pl.ds(pl.multiple_of(q * n, n), n)
            bf[slot, r, :] = st[slot, r, :].astype(BF16)

    @pl.when(e == 0)
    def _():
        for c in w_copies(0, 0):
            c.start()
        for c in w_copies(1, 1):
            c.start()
        for c in w_copies(0, 0):
            c.wait()
        for q in range(W_PARTS):
            cast_part(0, q)
        for c in w_copies(2, 0):
            c.start()

    @pl.when((e >= 1) & (e + 2 < ne))
    def _():
        for c in w_copies(e + 2, cur):
            c.start()

    @pl.when(e + 1 < ne)
    def _():
        for c in w_copies(e + 1, nxt):
            c.wait()

    def x_copy(g):
        slot = g % EX_RING
        src = buf_ref.at[pl.ds(pl.multiple_of(g * xrows, xrows), xrows), :]
        return pltpu.make_async_copy(src, xbuf.at[slot], in_sem.at[slot])

    def o_copy(g):
        slot = g % EX_RING
        dst = yb_ref.at[pl.ds(pl.multiple_of(g * orows, orows), orows), :]
        return pltpu.make_async_copy(obuf.at[slot], dst, out_sem.at[slot])

    @pl.when(e == 0)
    def _():
        for k in range(2):
            @pl.when(k < nu)
            def _():
                x_copy(k).start()

    def stage(g, n):
        for k in range(n):
            @pl.when(g + 2 + k < nu)
            def _():
                x_copy(g + 2 + k).start()
        for k in range(n):
            x_copy(g + k).wait()

            @pl.when(g + k >= EX_RING)
            def _():
                o_copy(g + k - EX_RING).wait()

    def compute(g):
        slot = g % EX_RING
        xb = _load_packed(xbuf.at[slot], EXPERT_BLOCK, pk, BF16)
        gte = jnp.dot(xb, wg_bf[cur], preferred_element_type=F32)
        up = jnp.dot(xb, wu_bf[cur], preferred_element_type=F32)
        act = gte * (1.0 / (1.0 + jnp.exp(-gte))) * up
        y = jnp.dot(act.astype(BF16), wd_bf[cur], preferred_element_type=F32)
        _store_packed(obuf.at[slot], y.astype(BF16), EXPERT_BLOCK)

    def pair_body(p, c):
        g = g0 + 2 * p
        stage(g, 2)
        compute(g)
        compute(g + 1)
        o_copy(g).start()
        o_copy(g + 1).start()
        cast_part(nxt, jnp.minimum(p, W_PARTS - 1))
        return c

    n_pairs = nb // 2
    lax.fori_loop(0, n_pairs, pair_body, 0)

    @pl.when(nb % 2 == 1)
    def _():
        g = g0 + nb - 1
        stage(g, 1)
        compute(g)
        o_copy(g).start()

    lax.fori_loop(jnp.minimum(n_pairs, W_PARTS), W_PARTS, lambda q, c: (cast_part(nxt, q), c)[1], 0)

    @pl.when(e == ne - 1)
    def _():
        lax.fori_loop(jnp.maximum(nu - EX_RING, 0), nu, lambda g, c: (o_copy(g).wait(), c)[1], 0)
        obuf[0] = jnp.zeros(obuf.shape[1:], obuf.dtype)

        def tail_copy(blk):
            r = pl.ds(pl.multiple_of(blk * orows, orows), orows)
            return pltpu.make_async_copy(obuf.at[0], yb_ref.at[r, :], tail_sem)

        lax.fori_loop(nu, n_blocks, lambda blk, c: (tail_copy(blk).start(), c)[1], 0)
        lax.fori_loop(nu, n_blocks, lambda blk, c: (tail_copy(blk).wait(), c)[1], 0)


def _experts(gstarts, nblk, n_used, buf, w_gate, w_up, w_down, layer, n_rows, pk):
    d, de = w_gate.shape[-2:]
    assert N_EXPERTS >= 3 and d % W_PARTS == 0 and de % W_PARTS == 0
    hbm = pl.BlockSpec(memory_space=pl.ANY)
    grid_spec = pltpu.PrefetchScalarGridSpec(
        num_scalar_prefetch=3,
        grid=(N_EXPERTS,),
        in_specs=[hbm, hbm, hbm, hbm],
        out_specs=hbm,
        scratch_shapes=[
            pltpu.VMEM((2, d, de), F32), pltpu.VMEM((2, d, de), F32), pltpu.VMEM((2, de, d), F32),
            pltpu.VMEM((2, d, de), BF16), pltpu.VMEM((2, d, de), BF16), pltpu.VMEM((2, de, d), BF16),
            pltpu.VMEM((EX_RING, EXPERT_BLOCK * pk, LANES), buf.dtype),
            pltpu.VMEM((EX_RING, EXPERT_BLOCK * pk, LANES), buf.dtype),
            pltpu.SemaphoreType.DMA((2,)), pltpu.SemaphoreType.DMA((EX_RING,)), pltpu.SemaphoreType.DMA((EX_RING,)),
            pltpu.SemaphoreType.DMA,
        ],
    )
    return pl.pallas_call(
        functools.partial(_expert_kernel, layer=layer, pk=pk),
        grid_spec=grid_spec,
        out_shape=jax.ShapeDtypeStruct((n_rows * pk, LANES), buf.dtype),
        compiler_params=_cparams(("arbitrary",)),
        name="moe_experts",
    )(gstarts, nblk, n_used, w_gate, w_up, w_down, buf)


CB_TM = 256


def _combine_kernel(dest_ref, x_ref, route_ref, gt_ref, gf_ref, yb_ref, o_ref, ybuf, sems, *, final_norm):
    tm, d = x_ref.shape
    fs = d // (2 * LANES)
    i = pl.program_id(0)
    n = pl.num_programs(0)

    def row_copy(tile, slot, r, k):
        src = dest_ref[(tile * tm + r) * TOP_K + k]
        return pltpu.make_async_copy(yb_ref.at[pl.ds(pl.multiple_of(src * fs, fs), fs), :],
                                     ybuf.at[slot, k, pl.ds(pl.multiple_of(r * fs, fs), fs), :], sems.at[slot])

    def issue(tile, slot):
        def body(q, c):
            for u in range(SC_UNROLL):
                for k in range(TOP_K):
                    row_copy(tile, slot, q * SC_UNROLL + u, k).start(priority=k)
            return c
        lax.fori_loop(0, tm // SC_UNROLL, body, 0)

    @pl.when(i == 0)
    def _():
        issue(0, 0)

    @pl.when(i + 1 < n)
    def _():
        issue(i + 1, (i + 1) % 2)

    slot = i % 2

    def drain(q, c):
        for u in range(SC_UNROLL):
            for k in range(TOP_K):
                row_copy(i, slot, q * SC_UNROLL + u, k).wait()
        return c
    lax.fori_loop(0, tm // SC_UNROLL, drain, 0)

    route = route_ref[...]
    ys = [_load_packed(ybuf.at[slot, k], tm, fs, F32) for k in range(TOP_K)]
    y = ys[0] * route[:, 2:3] + ys[1] * route[:, 3:4]
    xn = x_ref[...] + gt_ref[0] * y
    if final_norm:
        xn = _rms(xn, gf_ref[...])
    o_ref[...] = xn


def _combine(dest_flat, x2, route, gt, g_final, yb, seq, final_norm):
    t, d = x2.shape
    tpb = seq // CB_TM
    grid_spec = pltpu.PrefetchScalarGridSpec(
        num_scalar_prefetch=1,
        grid=(t // CB_TM,),
        in_specs=[
            pl.BlockSpec((CB_TM, d), lambda i, ds: (i, 0)),
            pl.BlockSpec((CB_TM, LANES), lambda i, ds: (i, 0)),
            pl.BlockSpec((1, 1, d), lambda i, ds: (i // tpb, 0, 0)),
            pl.BlockSpec((1, d), lambda i, ds: (0, 0)),
            pl.BlockSpec(memory_space=pl.ANY),
        ],
        out_specs=pl.BlockSpec((CB_TM, d), lambda i, ds: (i, 0)),
        scratch_shapes=[pltpu.VMEM((2, TOP_K, CB_TM * (d // (2 * LANES)), LANES), yb.dtype),
                        pltpu.SemaphoreType.DMA((2,))],
    )
    return pl.pallas_call(
        functools.partial(_combine_kernel, final_norm=final_norm),
        grid_spec=grid_spec,
        out_shape=jax.ShapeDtypeStruct((t, d), F32),
        compiler_params=_cparams(("arbitrary",)),
        name="moe_combine",
    )(dest_flat, x2, route, gt, g_final, yb)


def _dft_cos_sin(n):
    k = np.arange(n, dtype=np.int64)
    ang = (np.outer(k, k) % n).astype(np.float64) * (2.0 * np.pi / n)
    return np.cos(ang), np.sin(ang)


@functools.lru_cache(maxsize=None)
def _dft_constants(seq, dg):
    cc, sc = _dft_cos_sin(dg)
    hk = dg // 2
    cs_chan = np.concatenate([cc[:, :hk], sc[:, :hk]], axis=1).astype(np.float32)
    k = np.arange(seq // 2, dtype=np.int64)[:, None]
    parts = []
    for par in range(2):
        n = 2 * np.arange(seq // 2, dtype=np.int64)[None, :] + par
        ang = ((k * n) % seq).astype(np.float64) * (2.0 * np.pi / seq)
        parts.append(np.stack([np.cos(ang), np.sin(ang)], axis=0))
    cs_seq = np.stack(parts, axis=0).astype(np.float32)
    return cs_chan.astype(BF16), cs_seq.astype(BF16)


def _moe(x2, g2, sh, sc, gt, w_group, b_group, w_router, b_router, w_gate, w_up, w_down, layer,
         g_final, seq, final_norm):
    t, d = x2.shape
    w_all = jnp.concatenate([w_group, w_router], axis=1)
    w_all = jnp.pad(w_all, ((0, 0), (0, LANES - w_all.shape[1])))
    w_hi = w_all.astype(BF16)
    w_lo = (w_all - w_hi.astype(F32)).astype(BF16)
    whl = jnp.concatenate([w_hi, w_lo], axis=1)
    br = jnp.pad(jnp.concatenate([b_group, b_router]), (0, LANES - N_GROUPS - N_EXPERTS)).reshape(1, LANES)

    pk = d // (2 * LANES)
    hp, route, cnt = _route(x2, g2, sh, sc, whl, br, seq)

    counts = cnt[0, :N_EXPERTS].astype(jnp.int32)
    padded = (counts + EXPERT_BLOCK - 1) // EXPERT_BLOCK * EXPERT_BLOCK
    pends = jnp.cumsum(padded)
    pstarts = pends - padded
    n_assign = t * TOP_K
    n_blocks = (n_assign + N_EXPERTS * (EXPERT_BLOCK - 1) + EXPERT_BLOCK - 1) // EXPERT_BLOCK
    n_rows = n_blocks * EXPERT_BLOCK
    e_idx = route[:, 0:TOP_K].astype(jnp.int32)
    rank = route[:, 4:4 + TOP_K].astype(jnp.int32)
    is_e = e_idx[:, :, None] == jnp.arange(N_EXPERTS, dtype=jnp.int32)
    dest_flat = (jnp.sum(jnp.where(is_e, pstarts, 0), axis=-1) + rank).reshape(n_assign)
    n_used = (pends[-1:] // EXPERT_BLOCK).astype(jnp.int32)

    buf = _scatter(dest_flat, (pstarts + counts).astype(jnp.int32), (padded - counts).astype(jnp.int32), n_used,
                   hp, n_rows, pk)
    yb = _experts((pstarts // EXPERT_BLOCK).astype(jnp.int32), (padded // EXPERT_BLOCK).astype(jnp.int32), n_used,
                  buf, w_gate, w_up, w_down, layer, n_rows, pk)
    return _combine(dest_flat, x2, route, gt, g_final, yb, seq, final_norm)


def kernel(x, c, g_norm1, g_norm2, w_ada, b_ada, fa_w_in, fa_w_out, sg_w_in, sg_b_in, sg_g_v, sg_w_s, sg_b_s, sg_w_out, w_group, b_group, w_router, b_router, w_gate, w_up, w_down, g_final):
    n_batch, seq, d = x.shape
    depth = w_ada.shape[0]
    t = n_batch * seq
    x2 = x.reshape(t, d)

    mod = _ada(c, w_ada, b_ada)
    cs_chan, cs_seq = _dft_constants(seq, d // F_GROUPS)
    gfin = g_final.reshape(1, d)

    for l in range(depth):
        parts = [mod[l, :n_batch, k * d:(k + 1) * d].reshape(n_batch, 1, d) for k in range(6)]
        sh1, sc1, gt1, sh2, sc2, gt2 = parts
        g1 = g_norm1[l].reshape(1, d)
        j = l // 2
        if l % 2 == 0:
            ab = _fnet_in(x2, g1, sh1, sc1, fa_w_in[j].astype(BF16), jnp.asarray(cs_chan), n_batch, seq)
            x2 = _fnet_seq(jnp.asarray(cs_seq), ab, _fold_w_out(fa_w_out[j]), x2, gt1, n_batch, seq)
        else:
            x2 = _sgu(x2, g1, sh1, sc1, gt1, sg_w_in[j].astype(BF16), sg_b_in[j].reshape(1, 2 * d),
                      sg_g_v[j].reshape(1, d), sg_w_s[j].astype(BF16), sg_b_s[j].T, sg_w_out[j].astype(BF16), seq)
        x2 = _moe(x2, g_norm2[l].reshape(1, d), sh2, sc2, gt2, w_group[l], b_group[l], w_router[l], b_router[l],
                  w_gate, w_up, w_down, l, gfin, seq, final_norm=(l == depth - 1))
    return x2.reshape(n_batch, seq, d)
```

```python
import functools

import numpy as np
import jax
import jax.numpy as jnp
from jax import lax
from jax.experimental import pallas as pl
from jax.experimental.pallas import tpu as pltpu

F32 = jnp.float32
BF16 = jnp.bfloat16

EPS = 1e-6
F_GROUPS = 4
SGU_HEADS = 8
CHUNK = 128
N_GROUPS = 4
E_PER_GROUP = 8
N_EXPERTS = N_GROUPS * E_PER_GROUP
TOP_K = 2
EXPERT_BLOCK = 256

LANES = 128
SUBLANES = 8
VMEM_LIMIT = 56 * 1024 * 1024
NEG_BIG = -1e30


def _cparams(sem, vmem=VMEM_LIMIT):
    return pltpu.CompilerParams(dimension_semantics=sem, vmem_limit_bytes=vmem)


def _rms(x, g):
    return x * lax.rsqrt(jnp.mean(x * x, axis=-1, keepdims=True) + EPS) * g


def _store_packed(ref, vals_bf16, n_tok):
    pk = vals_bf16.shape[1] // (2 * LANES)
    bits = pltpu.bitcast(vals_bf16.astype(F32), jnp.uint32)
    for s in range(pk):
        low = bits[:, s * LANES:(s + 1) * LANES]
        high = bits[:, (s + pk) * LANES:(s + pk + 1) * LANES]
        ref[pl.ds(s, n_tok, stride=pk), :] = (high & jnp.uint32(0xFFFF0000)) | (low >> jnp.uint32(16))


def _load_packed(ref, n_tok, pk, dtype):
    lows, highs = [], []
    for s in range(pk):
        w = ref[pl.ds(s, n_tok, stride=pk), :]
        lows.append(pltpu.bitcast(w << jnp.uint32(16), F32).astype(dtype))
        highs.append(pltpu.bitcast(w & jnp.uint32(0xFFFF0000), F32).astype(dtype))
    return jnp.concatenate(lows + highs, axis=1)


ADA_TN = 512


def _ada_kernel(c_ref, w_ref, b_ref, o_ref, lhs_ref):
    @pl.when((pl.program_id(0) == 0) & (pl.program_id(1) == 0))
    def _():
        cv = c_ref[...]
        s = cv / (1.0 + jnp.exp(-cv))
        s_hi = s.astype(BF16).astype(F32)
        lhs_ref[...] = jnp.concatenate([s_hi, s - s_hi], axis=0).astype(BF16)

    w = w_ref[0]
    w_hi = w.astype(BF16)
    w_lo = (w - w_hi.astype(F32)).astype(BF16)
    lhs = lhs_ref[...]
    r = jnp.dot(lhs, w_hi, preferred_element_type=F32) + jnp.dot(lhs, w_lo, preferred_element_type=F32)
    o_ref[0] = r[:SUBLANES] + r[SUBLANES:] + b_ref[0]


def _ada(c, w_ada, b_ada):
    n_batch, d = c.shape
    depth, _, n6 = w_ada.shape
    assert n_batch <= SUBLANES
    c8 = jnp.pad(c, ((0, SUBLANES - n_batch), (0, 0)))
    return pl.pallas_call(
        _ada_kernel,
        grid=(depth, n6 // ADA_TN),
        in_specs=[
            pl.BlockSpec((SUBLANES, d), lambda l, j: (0, 0)),
            pl.BlockSpec((1, d, ADA_TN), lambda l, j: (l, 0, j)),
            pl.BlockSpec((1, 1, ADA_TN), lambda l, j: (l, 0, j)),
        ],
        out_specs=pl.BlockSpec((1, SUBLANES, ADA_TN), lambda l, j: (l, 0, j)),
        out_shape=jax.ShapeDtypeStruct((depth, SUBLANES, n6), F32),
        scratch_shapes=[pltpu.VMEM((2 * SUBLANES, d), BF16)],
        compiler_params=_cparams(("arbitrary", "arbitrary")),
        name="ada_mod",
    )(c8, w_ada, b_ada.reshape(depth, 1, n6))


FN_TM = 256


def _fnet_in_kernel(x_ref, g_ref, sh_ref, sc_ref, win_ref, cs_ref, ab_ref, xs_ref):
    tm, d = x_ref.shape
    dg = d // F_GROUPS
    half = tm // 2
    nch = d // LANES
    for c in range(nch):
        xs_ref[c] = x_ref[:, c * LANES:(c + 1) * LANES]
    x = jnp.concatenate(
        [jnp.concatenate([xs_ref[c, pl.ds(par, half, stride=2), :] for c in range(nch)], axis=1)
         for par in range(2)], axis=0)
    h = _rms(x, g_ref[...]) * (1.0 + sc_ref[0]) + sh_ref[0]
    z = jnp.dot(h.astype(BF16), win_ref[...], preferred_element_type=F32)
    hk = dg // 2
    pw = F_GROUPS * hk + LANES
    lane = lax.broadcasted_iota(jnp.int32, (1, dg), 1)
    alt = (1 - 2 * (lane & 1)).astype(F32)
    lane_n = lax.broadcasted_iota(jnp.int32, (tm, LANES), 1)
    nyq = jnp.zeros((tm, LANES), F32)
    for g in range(F_GROUPS):
        zg = z[:, g * dg:(g + 1) * dg]
        ab = jnp.dot(zg.astype(BF16), cs_ref[...], preferred_element_type=F32)
        nyq = jnp.where(lane_n == g, jnp.sum(zg * alt, axis=-1, keepdims=True), nyq)
        for par in range(2):
            rows = slice(par * half, (par + 1) * half)
            ab_ref[0, par, :, g * hk:(g + 1) * hk] = ab[rows, :hk].astype(BF16)
            ab_ref[0, par, :, pw + g * hk:pw + (g + 1) * hk] = ab[rows, hk:].astype(BF16)
    for par in range(2):
        ab_ref[0, par, :, F_GROUPS * hk:pw] = nyq[par * half:(par + 1) * half].astype(BF16)


def _fnet_in(x2, g, sh, sc, w_in_bf, cs_chan, n_batch, seq):
    t, d = x2.shape
    tpb = seq // FN_TM
    dg = d // F_GROUPS
    rw = d + LANES
    return pl.pallas_call(
        _fnet_in_kernel,
        grid=(t // FN_TM,),
        in_specs=[
            pl.BlockSpec((FN_TM, d), lambda i: (i, 0)),
            pl.BlockSpec((1, d), lambda i: (0, 0)),
            pl.BlockSpec((1, 1, d), lambda i: (i // tpb, 0, 0)),
            pl.BlockSpec((1, 1, d), lambda i: (i // tpb, 0, 0)),
            pl.BlockSpec((d, d), lambda i: (0, 0), pipeline_mode=pl.Buffered(1)),
            pl.BlockSpec((dg, dg), lambda i: (0, 0), pipeline_mode=pl.Buffered(1)),
        ],
        out_specs=pl.BlockSpec((1, 2, FN_TM // 2, rw), lambda i: (i // tpb, 0, i % tpb, 0)),
        out_shape=jax.ShapeDtypeStruct((n_batch, 2, seq // 2, rw), BF16),
        scratch_shapes=[pltpu.VMEM((d // LANES, FN_TM, LANES), F32)],
        compiler_params=_cparams(("arbitrary",)),
        name="fnet_in",
    )(x2, g, sh, sc, w_in_bf, cs_chan)


FS_TK = 256
FS_SUB = 128


def _fnet_seq_kernel(cs_ref, ab_ref, wout_ref, x_ref, gt_ref, o_ref, *, scale):
    tk = cs_ref.shape[2]
    rw = ab_ref.shape[3]
    pw = rw - wout_ref.shape[1] // 2
    sub = FS_SUB
    for c in range(tk // sub):
        rows = slice(c * sub, (c + 1) * sub)
        pe = jnp.dot(cs_ref[0, 0, rows, :], ab_ref[0, 0, :, :pw], preferred_element_type=F32)
        po = jnp.dot(cs_ref[1, 0, rows, :], ab_ref[0, 1, :, :pw], preferred_element_type=F32)
        qe = jnp.dot(cs_ref[0, 1, rows, :], ab_ref[0, 0, :, pw:], preferred_element_type=F32)
        qo = jnp.dot(cs_ref[1, 1, rows, :], ab_ref[0, 1, :, pw:], preferred_element_type=F32)
        lo = jnp.concatenate([pe + po, qe + qo], axis=1)
        hi = jnp.concatenate([pe - po, qe - qo], axis=1)
        f = (jnp.concatenate([lo, hi], axis=0) * scale).astype(BF16)
        y = jnp.dot(f, wout_ref[...], preferred_element_type=F32)
        o_ref[0, 0, rows, :] = x_ref[0, 0, rows, :] + gt_ref[0] * y[:sub]
        o_ref[0, 1, rows, :] = x_ref[0, 1, rows, :] + gt_ref[0] * y[sub:]


def _fold_w_out_kernel(w_ref, o_ref):
    d = w_ref.shape[1]
    dg = d // F_GROUPS
    hk = dg // 2
    nc = F_GROUPS * hk
    pw = nc + LANES
    r = lax.broadcasted_iota(jnp.int32, (dg, dg), 0)
    c = lax.broadcasted_iota(jnp.int32, (dg, dg), 1)
    mirror = jnp.where((r + c == dg) | ((r == 0) & (c == 0)), 1.0, 0.0).astype(BF16)
    first = lax.broadcasted_iota(jnp.int32, (hk, 1), 0) == 0
    row_n = lax.broadcasted_iota(jnp.int32, (LANES, 1), 0)
    nyq = jnp.zeros((LANES, d), F32)
    for g in range(F_GROUPS):
        w = w_ref[g * dg:(g + 1) * dg, :]
        wm = jnp.dot(mirror, w.astype(BF16), preferred_element_type=F32)
        o_ref[g * hk:(g + 1) * hk, :] = jnp.where(first, w[:hk], w[:hk] + wm[:hk]).astype(BF16)
        o_ref[pw + g * hk:pw + (g + 1) * hk, :] = jnp.where(first, 0.0, wm[:hk] - w[:hk]).astype(BF16)
        nyq = jnp.where(row_n == g, w[hk:hk + 1], nyq)
    o_ref[nc:pw, :] = nyq.astype(BF16)


def _fold_w_out(w_out):
    d = w_out.shape[0]
    return pl.pallas_call(
        _fold_w_out_kernel,
        grid=(1,),
        in_specs=[pl.BlockSpec((d, d), lambda i: (0, 0), pipeline_mode=pl.Buffered(1))],
        out_specs=pl.BlockSpec((d + LANES, d), lambda i: (0, 0)),
        out_shape=jax.ShapeDtypeStruct((d + LANES, d), BF16),
        compiler_params=_cparams(("arbitrary",)),
        name="fold_w_out",
    )(w_out)


def _fnet_seq(cs_seq, ab, w_fold_bf, x2, gt, n_batch, seq):
    t, d = x2.shape
    hs = seq // 2
    rw = ab.shape[-1]
    scale = 1.0 / float(np.sqrt(seq * (d // F_GROUPS)))
    x4 = x2.reshape(n_batch, 2, hs, d)
    out = pl.pallas_call(
        functools.partial(_fnet_seq_kernel, scale=scale),
        grid=(n_batch, hs // FS_TK),
        in_specs=[
            pl.BlockSpec((2, 2, FS_TK, hs), lambda b, k: (0, 0, k, 0)),
            pl.BlockSpec((1, 2, hs, rw), lambda b, k: (b, 0, 0, 0), pipeline_mode=pl.Buffered(1)),
            pl.BlockSpec((rw, d), lambda b, k: (0, 0), pipeline_mode=pl.Buffered(1)),
            pl.BlockSpec((1, 2, FS_TK, d), lambda b, k: (b, 0, k, 0)),
            pl.BlockSpec((1, 1, d), lambda b, k: (b, 0, 0)),
        ],
        out_specs=pl.BlockSpec((1, 2, FS_TK, d), lambda b, k: (b, 0, k, 0)),
        out_shape=jax.ShapeDtypeStruct((n_batch, 2, hs, d), F32),
        compiler_params=_cparams(("arbitrary", "arbitrary")),
        name="fnet_seq",
    )(cs_seq, ab, w_fold_bf, x4, gt)
    return out.reshape(t, d)


SG_TM = 256


def _gelu_tanh(x):
    c = float(np.sqrt(2.0 / np.pi))
    return x * (0.5 * (1.0 + jnp.tanh(c * (x + 0.044715 * (x * x * x)))))


def _sgu_kernel(x_ref, g_ref, sh_ref, sc_ref, gt_ref, win_ref, bin_ref, gv_ref, ws_ref, bs_ref,
                wout_ref, o_ref, gated_ref):
    tm, d = x_ref.shape
    dh = d // SGU_HEADS
    for c in range(tm // CHUNK):
        rows = slice(c * CHUNK, (c + 1) * CHUNK)
        x = x_ref[rows, :]
        h = _rms(x, g_ref[...]) * (1.0 + sc_ref[0]) + sh_ref[0]
        z = jnp.dot(h.astype(BF16), win_ref[...], preferred_element_type=F32) + bin_ref[...]
        z = _gelu_tanh(z)
        u = z[:, :d]
        v = _rms(z[:, d:], gv_ref[...])
        for hd in range(SGU_HEADS):
            cols = slice(hd * dh, (hd + 1) * dh)
            vc = v[:, cols].astype(BF16)
            m = jnp.dot(ws_ref[hd], vc, preferred_element_type=F32) + bs_ref[:, hd:hd + 1]
            gated_ref[rows, cols] = (u[:, cols] * m).astype(BF16)
        y = jnp.dot(gated_ref[rows, :], wout_ref[...], preferred_element_type=F32)
        o_ref[rows, :] = x + gt_ref[0] * y


def _sgu(x2, g, sh, sc, gt, w_in_bf, b_in, g_v, w_s_bf, b_s_t, w_out_bf, seq):
    t, d = x2.shape
    tpb = seq // SG_TM
    const2 = lambda i: (0, 0)
    perb = lambda i: (i // tpb, 0, 0)
    return pl.pallas_call(
        _sgu_kernel,
        grid=(t // SG_TM,),
        in_specs=[
            pl.BlockSpec((SG_TM, d), lambda i: (i, 0)),
            pl.BlockSpec((1, d), const2),
            pl.BlockSpec((1, 1, d), perb),
            pl.BlockSpec((1, 1, d), perb),
            pl.BlockSpec((1, 1, d), perb),
            pl.BlockSpec((d, 2 * d), const2, pipeline_mode=pl.Buffered(1)),
            pl.BlockSpec((1, 2 * d), const2),
            pl.BlockSpec((1, d), const2),
            pl.BlockSpec((SGU_HEADS, CHUNK, CHUNK), lambda i: (0, 0, 0)),
            pl.BlockSpec((CHUNK, SGU_HEADS), const2),
            pl.BlockSpec((d, d), const2, pipeline_mode=pl.Buffered(1)),
        ],
        out_specs=pl.BlockSpec((SG_TM, d), lambda i: (i, 0)),
        out_shape=jax.ShapeDtypeStruct((t, d), F32),
        scratch_shapes=[pltpu.VMEM((SG_TM, d), BF16)],
        compiler_params=_cparams(("arbitrary",)),
        name="sgu_mix",
    )(x2, g, sh, sc, gt, w_in_bf, b_in, g_v, w_s_bf, b_s_t, w_out_bf)


RT_TM = 256


def _route_kernel(x_ref, g_ref, sh_ref, sc_ref, whl_ref, br_ref, h_ref, route_ref, cnt_ref, carry_ref):
    tm = x_ref.shape[0]
    i = pl.program_id(0)

    @pl.when(i == 0)
    def _():
        carry_ref[...] = jnp.zeros_like(carry_ref)

    h = _rms(x_ref[...], g_ref[...]) * (1.0 + sc_ref[0]) + sh_ref[0]
    h_hi = h.astype(BF16)
    _store_packed(h_ref, h_hi, tm)
    h_lo = (h - h_hi.astype(F32)).astype(BF16)
    whl = whl_ref[...]
    both = jnp.dot(h_hi, whl, preferred_element_type=F32)
    lo = jnp.dot(h_lo, whl[:, :LANES], preferred_element_type=F32)
    lg = both[:, :LANES] + both[:, LANES:] + lo + br_ref[...]

    lane = lax.broadcasted_iota(jnp.int32, (tm, LANES), 1)
    lanef = lane.astype(F32)
    is_g = lane < N_GROUPS
    gl = jnp.where(is_g, lg, NEG_BIG)
    gmax = jnp.max(gl, axis=1, keepdims=True)
    gidx = jnp.min(jnp.where(gl == gmax, lanef, float(LANES)), axis=1, keepdims=True)
    gsum = jnp.sum(jnp.where(is_g, jnp.exp(gl - gmax), 0.0), axis=1, keepdims=True)
    g_w = 1.0 / gsum
    lo_lane = float(N_GROUPS) + gidx * float(E_PER_GROUP)
    in_grp = (lanef >= lo_lane) & (lanef < lo_lane + float(E_PER_GROUP))
    el = jnp.where(in_grp, lg, NEG_BIG)
    v1 = jnp.max(el, axis=1, keepdims=True)
    i1 = jnp.min(jnp.where(el == v1, lanef, float(LANES)), axis=1, keepdims=True)
    el2 = jnp.where(lanef == i1, NEG_BIG, el)
    v2 = jnp.max(el2, axis=1, keepdims=True)
    i2 = jnp.min(jnp.where(el2 == v2, lanef, float(LANES)), axis=1, keepdims=True)
    p = jnp.exp(v2 - v1)
    w0 = g_w / (1.0 + p)
    w1 = g_w * p / (1.0 + p)
    e0 = i1 - float(N_GROUPS)
    e1 = i2 - float(N_GROUPS)

    oh0 = jnp.where(lanef == e0, 1.0, 0.0)
    oh1 = jnp.where(lanef == e1, 1.0, 0.0)
    rr = lax.broadcasted_iota(jnp.int32, (tm, tm), 0)
    cc = lax.broadcasted_iota(jnp.int32, (tm, tm), 1)
    tri = jnp.where(rr > cc, 1.0, 0.0).astype(BF16)
    pre0 = jnp.dot(tri, oh0.astype(BF16), preferred_element_type=F32)
    pre1 = jnp.dot(tri, oh1.astype(BF16), preferred_element_type=F32)
    carry = carry_ref[0:1, :]
    cnt0 = jnp.sum(oh0, axis=0, keepdims=True)
    cnt1 = jnp.sum(oh1, axis=0, keepdims=True)
    rank0 = jnp.sum(oh0 * (pre0 + carry), axis=1, keepdims=True)
    rank1 = jnp.sum(oh1 * (pre1 + carry + cnt0), axis=1, keepdims=True)
    new_carry = carry + cnt0 + cnt1
    carry_ref[...] = jnp.broadcast_to(new_carry, carry_ref.shape)
    cnt_ref[...] = jnp.broadcast_to(new_carry, cnt_ref.shape)

    route = jnp.where(lane == 0, e0, 0.0)
    route = jnp.where(lane == 1, e1, route)
    route = jnp.where(lane == 2, w0, route)
    route = jnp.where(lane == 3, w1, route)
    route = jnp.where(lane == 4, rank0, route)
    route = jnp.where(lane == 5, rank1, route)
    route_ref[...] = route


def _route(x2, g, sh, sc, whl, br, seq):
    t, d = x2.shape
    tpb = seq // RT_TM
    pk = d // (2 * LANES)
    perb = lambda i: (i // tpb, 0, 0)
    return pl.pallas_call(
        _route_kernel,
        grid=(t // RT_TM,),
        in_specs=[
            pl.BlockSpec((RT_TM, d), lambda i: (i, 0)),
            pl.BlockSpec((1, d), lambda i: (0, 0)),
            pl.BlockSpec((1, 1, d), perb),
            pl.BlockSpec((1, 1, d), perb),
            pl.BlockSpec((d, 2 * LANES), lambda i: (0, 0)),
            pl.BlockSpec((1, LANES), lambda i: (0, 0)),
        ],
        out_specs=[
            pl.BlockSpec((RT_TM * pk, LANES), lambda i: (i, 0)),
            pl.BlockSpec((RT_TM, LANES), lambda i: (i, 0)),
            pl.BlockSpec((SUBLANES, LANES), lambda i: (0, 0)),
        ],
        out_shape=[
            jax.ShapeDtypeStruct((t * pk, LANES), jnp.uint32),
            jax.ShapeDtypeStruct((t, LANES), F32),
            jax.ShapeDtypeStruct((SUBLANES, LANES), F32),
        ],
        scratch_shapes=[pltpu.VMEM((SUBLANES, LANES), F32)],
        compiler_params=_cparams(("arbitrary",)),
        name="moe_route",
    )(x2, g, sh, sc, whl, br)


SC_TM = 1024


SC_UNROLL = 8


def _scatter_kernel(dest_ref, pad_start_ref, pad_n_ref, nu_ref, h_ref, buf_ref, zrow_ref, sem, zsem, tsem, *, pk):
    tm = h_ref.shape[0] // pk
    i = pl.program_id(0)
    base = i * (tm * TOP_K)
    blk_rows = EXPERT_BLOCK * pk
    n_blocks = buf_ref.shape[0] // blk_rows

    def tok_rows(tok):
        return pl.ds(pl.multiple_of(tok * pk, pk), pk)

    def row_copy(r, k):
        dst = dest_ref[base + r * TOP_K + k]
        return pltpu.make_async_copy(h_ref.at[tok_rows(r), :], buf_ref.at[tok_rows(dst), :], sem)

    def pad_fill(e, do):
        n = pad_n_ref[e]
        start = pad_start_ref[e]
        size = EXPERT_BLOCK // 2
        while size >= 1:
            has = (n & size) != 0
            cp = pltpu.make_async_copy(zrow_ref.at[pl.ds(0, size * pk), :],
                                       buf_ref.at[pl.ds(pl.multiple_of(start * pk, pk), size * pk), :], zsem)

            @pl.when(has)
            def _():
                do(cp)

            start = start + jnp.where(has, size, 0)
            size //= 2

    def tail_copy(blk):
        row0 = pl.multiple_of(blk * blk_rows, blk_rows)
        return pltpu.make_async_copy(zrow_ref, buf_ref.at[pl.ds(row0, blk_rows), :], tsem)

    @pl.when(i == 0)
    def _():
        zrow_ref[...] = jnp.zeros_like(zrow_ref)

        def per_expert(e, c):
            pad_fill(e, lambda cp: cp.start())
            return c
        lax.fori_loop(0, N_EXPERTS, per_expert, 0)
        lax.fori_loop(nu_ref[0], n_blocks, lambda blk, c: (tail_copy(blk).start(), c)[1], 0)

    def issue(q, c):
        for u in range(SC_UNROLL):
            for k in range(TOP_K):
                row_copy(q * SC_UNROLL + u, k).start(priority=k)
        return c
    lax.fori_loop(0, tm // SC_UNROLL, issue, 0)

    def drain(q, c):
        for u in range(SC_UNROLL):
            for k in range(TOP_K):
                row_copy(q * SC_UNROLL + u, k).wait()
        return c
    lax.fori_loop(0, tm // SC_UNROLL, drain, 0)

    @pl.when(i == 0)
    def _():
        def per_expert(e, c):
            pad_fill(e, lambda cp: cp.wait())
            return c
        lax.fori_loop(0, N_EXPERTS, per_expert, 0)
        lax.fori_loop(nu_ref[0], n_blocks, lambda blk, c: (tail_copy(blk).wait(), c)[1], 0)


def _scatter(dest_flat, pad_start, pad_n, n_used, hp, n_rows, pk):
    t = hp.shape[0] // pk
    grid_spec = pltpu.PrefetchScalarGridSpec(
        num_scalar_prefetch=4,
        grid=(t // SC_TM,),
        in_specs=[pl.BlockSpec((SC_TM * pk, LANES), lambda i, *_: (i, 0))],
        out_specs=pl.BlockSpec(memory_space=pl.ANY),
        scratch_shapes=[pltpu.VMEM((EXPERT_BLOCK * pk, LANES), hp.dtype), pltpu.SemaphoreType.DMA,
                        pltpu.SemaphoreType.DMA, pltpu.SemaphoreType.DMA],
    )
    return pl.pallas_call(
        functools.partial(_scatter_kernel, pk=pk),
        grid_spec=grid_spec,
        out_shape=jax.ShapeDtypeStruct((n_rows * pk, LANES), hp.dtype),
        compiler_params=_cparams(("arbitrary",)),
        name="moe_scatter",
    )(dest_flat, pad_start, pad_n, n_used, hp)


EX_RING = 4
W_PARTS = 2


def _expert_kernel(pstart_ref, nblk_ref, nu_ref, wg_hbm, wu_hbm, wd_hbm, buf_ref, yb_ref,
                   wg_st, wu_st, wd_st, wg_bf, wu_bf, wd_bf, xbuf, obuf, w_sem, in_sem, out_sem, tail_sem,
                   *, layer, pk):
    e = pl.program_id(0)
    ne = pl.num_programs(0)
    nb = nblk_ref[e]
    g0 = pstart_ref[e]
    nu = nu_ref[0]
    xrows = EXPERT_BLOCK * pk
    orows = EXPERT_BLOCK * pk
    n_blocks = yb_ref.shape[0] // orows
    cur = e % 2
    nxt = (e + 1) % 2

    def w_copies(ex, slot):
        return [pltpu.make_async_copy(hbm.at[layer, ex], st.at[slot], w_sem.at[slot])
                for hbm, st in ((wg_hbm, wg_st), (wu_hbm, wu_st), (wd_hbm, wd_st))]

    def cast_part(slot, q):
        for st, bf in ((wg_st, wg_bf), (wu_st, wu_bf), (wd_st, wd_bf)):
            n = st.shape[1] // W_PARTS
            r = pl.ds(pl.multiple_of(q * n, n), n)
            bf[slot, r, :] = st[slot, r, :].astype(BF16)

    @pl.when(e == 0)
    def _():
        for c in w_copies(0, 0):
            c.start()
        for c in w_copies(1, 1):
            c.start()
        for c in w_copies(0, 0):
            c.wait()
        for q in range(W_PARTS):
            cast_part(0, q)
        for c in w_copies(2, 0):
            c.start()

    @pl.when((e >= 1) & (e + 2 < ne))
    def _():
        for c in w_copies(e + 2, cur):
            c.start()

    @pl.when(e + 1 < ne)
    def _():
        for c in w_copies(e + 1, nxt):
            c.wait()

    def x_copy(g):
        slot = g % EX_RING
        src = buf_ref.at[pl.ds(pl.multiple_of(g * xrows, xrows), xrows), :]
        return pltpu.make_async_copy(src, xbuf.at[slot], in_sem.at[slot])

    def o_copy(g):
        slot = g % EX_RING
        dst = yb_ref.at[pl.ds(pl.multiple_of(g * orows, orows), orows), :]
        return pltpu.make_async_copy(obuf.at[slot], dst, out_sem.at[slot])

    @pl.when(e == 0)
    def _():
        for k in range(2):
            @pl.when(k < nu)
            def _():
                x_copy(k).start()

    def stage(g, n):
        for k in range(n):
            @pl.when(g + 2 + k < nu)
            def _():
                x_copy(g + 2 + k).start()
        for k in range(n):
            x_copy(g + k).wait()

            @pl.when(g + k >= EX_RING)
            def _():
                o_copy(g + k - EX_RING).wait()

    def compute(g):
        slot = g % EX_RING
        xb = _load_packed(xbuf.at[slot], EXPERT_BLOCK, pk, BF16)
        gte = jnp.dot(xb, wg_bf[cur], preferred_element_type=F32)
        up = jnp.dot(xb, wu_bf[cur], preferred_element_type=F32)
        act = gte * (1.0 / (1.0 + jnp.exp(-gte))) * up
        y = jnp.dot(act.astype(BF16), wd_bf[cur], preferred_element_type=F32)
        _store_packed(obuf.at[slot], y.astype(BF16), EXPERT_BLOCK)

    def pair_body(p, c):
        g = g0 + 2 * p
        stage(g, 2)
        compute(g)
        compute(g + 1)
        o_copy(g).start()
        o_copy(g + 1).start()
        cast_part(nxt, jnp.minimum(p, W_PARTS - 1))
        return c

    n_pairs = nb // 2
    lax.fori_loop(0, n_pairs, pair_body, 0)

    @pl.when(nb % 2 == 1)
    def _():
        g = g0 + nb - 1
        stage(g, 1)
        compute(g)
        o_copy(g).start()

    lax.fori_loop(jnp.minimum(n_pairs, W_PARTS), W_PARTS, lambda q, c: (cast_part(nxt, q), c)[1], 0)

    @pl.when(e == ne - 1)
    def _():
        lax.fori_loop(jnp.maximum(nu - EX_RING, 0), nu, lambda g, c: (o_copy(g).wait(), c)[1], 0)
        obuf[0] = jnp.zeros(obuf.shape[1:], obuf.dtype)

        def tail_copy(blk):
            r = pl.ds(pl.multiple_of(blk * orows, orows), orows)
            return pltpu.make_async_copy(obuf.at[0], yb_ref.at[r, :], tail_sem)

        lax.fori_loop(nu, n_blocks, lambda blk, c: (tail_copy(blk).start(), c)[1], 0)
        lax.fori_loop(nu, n_blocks, lambda blk, c: (tail_copy(blk).wait(), c)[1], 0)


def _experts(gstarts, nblk, n_used, buf, w_gate, w_up, w_down, layer, n_rows, pk):
    d, de = w_gate.shape[-2:]
    assert N_EXPERTS >= 3 and d % W_PARTS == 0 and de % W_PARTS == 0
    hbm = pl.BlockSpec(memory_space=pl.ANY)
    grid_spec = pltpu.PrefetchScalarGridSpec(
        num_scalar_prefetch=3,
        grid=(N_EXPERTS,),
        in_specs=[hbm, hbm, hbm, hbm],
        out_specs=hbm,
        scratch_shapes=[
            pltpu.VMEM((2, d, de), F32), pltpu.VMEM((2, d, de), F32), pltpu.VMEM((2, de, d), F32),
            pltpu.VMEM((2, d, de), BF16), pltpu.VMEM((2, d, de), BF16), pltpu.VMEM((2, de, d), BF16),
            pltpu.VMEM((EX_RING, EXPERT_BLOCK * pk, LANES), buf.dtype),
            pltpu.VMEM((EX_RING, EXPERT_BLOCK * pk, LANES), buf.dtype),
            pltpu.SemaphoreType.DMA((2,)), pltpu.SemaphoreType.DMA((EX_RING,)), pltpu.SemaphoreType.DMA((EX_RING,)),
            pltpu.SemaphoreType.DMA,
        ],
    )
    return pl.pallas_call(
        functools.partial(_expert_kernel, layer=layer, pk=pk),
        grid_spec=grid_spec,
        out_shape=jax.ShapeDtypeStruct((n_rows * pk, LANES), buf.dtype),
        compiler_params=_cparams(("arbitrary",)),
        name="moe_experts",
    )(gstarts, nblk, n_used, w_gate, w_up, w_down, buf)


CB_TM = 256


def _combine_kernel(dest_ref, x_ref, route_ref, gt_ref, gf_ref, yb_ref, o_ref, ybuf, sems, *, final_norm):
    tm, d = x_ref.shape
    fs = d // (2 * LANES)
    i = pl.program_id(0)
    n = pl.num_programs(0)

    def row_copy(tile, slot, r, k):
        src = dest_ref[(tile * tm + r) * TOP_K + k]
        return pltpu.make_async_copy(yb_ref.at[pl.ds(pl.multiple_of(src * fs, fs), fs), :],
                                     ybuf.at[slot, k, pl.ds(pl.multiple_of(r * fs, fs), fs), :], sems.at[slot])

    def issue(tile, slot):
        def body(q, c):
            for u in range(SC_UNROLL):
                for k in range(TOP_K):
                    row_copy(tile, slot, q * SC_UNROLL + u, k).start(priority=k)
            return c
        lax.fori_loop(0, tm // SC_UNROLL, body, 0)

    @pl.when(i == 0)
    def _():
        issue(0, 0)

    @pl.when(i + 1 < n)
    def _():
        issue(i + 1, (i + 1) % 2)

    slot = i % 2

    def drain(q, c):
        for u in range(SC_UNROLL):
            for k in range(TOP_K):
                row_copy(i, slot, q * SC_UNROLL + u, k).wait()
        return c
    lax.fori_loop(0, tm // SC_UNROLL, drain, 0)

    route = route_ref[...]
    ys = [_load_packed(ybuf.at[slot, k], tm, fs, F32) for k in range(TOP_K)]
    y = ys[0] * route[:, 2:3] + ys[1] * route[:, 3:4]
    xn = x_ref[...] + gt_ref[0] * y
    if final_norm:
        xn = _rms(xn, gf_ref[...])
    o_ref[...] = xn


def _combine(dest_flat, x2, route, gt, g_final, yb, seq, final_norm):
    t, d = x2.shape
    tpb = seq // CB_TM
    grid_spec = pltpu.PrefetchScalarGridSpec(
        num_scalar_prefetch=1,
        grid=(t // CB_TM,),
        in_specs=[
            pl.BlockSpec((CB_TM, d), lambda i, ds: (i, 0)),
            pl.BlockSpec((CB_TM, LANES), lambda i, ds: (i, 0)),
            pl.BlockSpec((1, 1, d), lambda i, ds: (i // tpb, 0, 0)),
            pl.BlockSpec((1, d), lambda i, ds: (0, 0)),
            pl.BlockSpec(memory_space=pl.ANY),
        ],
        out_specs=pl.BlockSpec((CB_TM, d), lambda i, ds: (i, 0)),
        scratch_shapes=[pltpu.VMEM((2, TOP_K, CB_TM * (d // (2 * LANES)), LANES), yb.dtype),
                        pltpu.SemaphoreType.DMA((2,))],
    )
    return pl.pallas_call(
        functools.partial(_combine_kernel, final_norm=final_norm),
        grid_spec=grid_spec,
        out_shape=jax.ShapeDtypeStruct((t, d), F32),
        compiler_params=_cparams(("arbitrary",)),
        name="moe_combine",
    )(dest_flat, x2, route, gt, g_final, yb)


def _dft_cos_sin(n):
    k = np.arange(n, dtype=np.int64)
    ang = (np.outer(k, k) % n).astype(np.float64) * (2.0 * np.pi / n)
    return np.cos(ang), np.sin(ang)


@functools.lru_cache(maxsize=None)
def _dft_constants(seq, dg):
    cc, sc = _dft_cos_sin(dg)
    hk = dg // 2
    cs_chan = np.concatenate([cc[:, :hk], sc[:, :hk]], axis=1).astype(np.float32)
    k = np.arange(seq // 2, dtype=np.int64)[:, None]
    parts = []
    for par in range(2):
        n = 2 * np.arange(seq // 2, dtype=np.int64)[None, :] + par
        ang = ((k * n) % seq).astype(np.float64) * (2.0 * np.pi / seq)
        parts.append(np.stack([np.cos(ang), np.sin(ang)], axis=0))
    cs_seq = np.stack(parts, axis=0).astype(np.float32)
    return cs_chan.astype(BF16), cs_seq.astype(BF16)


def _moe(x2, g2, sh, sc, gt, w_group, b_group, w_router, b_router, w_gate, w_up, w_down, layer,
         g_final, seq, final_norm):
    t, d = x2.shape
    w_all = jnp.concatenate([w_group, w_router], axis=1)
    w_all = jnp.pad(w_all, ((0, 0), (0, LANES - w_all.shape[1])))
    w_hi = w_all.astype(BF16)
    w_lo = (w_all - w_hi.astype(F32)).astype(BF16)
    whl = jnp.concatenate([w_hi, w_lo], axis=1)
    br = jnp.pad(jnp.concatenate([b_group, b_router]), (0, LANES - N_GROUPS - N_EXPERTS)).reshape(1, LANES)

    pk = d // (2 * LANES)
    hp, route, cnt = _route(x2, g2, sh, sc, whl, br, seq)

    counts = cnt[0, :N_EXPERTS].astype(jnp.int32)
    padded = (counts + EXPERT_BLOCK - 1) // EXPERT_BLOCK * EXPERT_BLOCK
    pends = jnp.cumsum(padded)
    pstarts = pends - padded
    n_assign = t * TOP_K
    n_blocks = (n_assign + N_EXPERTS * (EXPERT_BLOCK - 1) + EXPERT_BLOCK - 1) // EXPERT_BLOCK
    n_rows = n_blocks * EXPERT_BLOCK
    e_idx = route[:, 0:TOP_K].astype(jnp.int32)
    rank = route[:, 4:4 + TOP_K].astype(jnp.int32)
    is_e = e_idx[:, :, None] == jnp.arange(N_EXPERTS, dtype=jnp.int32)
    dest_flat = (jnp.sum(jnp.where(is_e, pstarts, 0), axis=-1) + rank).reshape(n_assign)
    n_used = (pends[-1:] // EXPERT_BLOCK).astype(jnp.int32)

    buf = _scatter(dest_flat, (pstarts + counts).astype(jnp.int32), (padded - counts).astype(jnp.int32), n_used,
                   hp, n_rows, pk)
    yb = _experts((pstarts // EXPERT_BLOCK).astype(jnp.int32), (padded // EXPERT_BLOCK).astype(jnp.int32), n_used,
                  buf, w_gate, w_up, w_down, layer, n_rows, pk)
    return _combine(dest_flat, x2, route, gt, g_final, yb, seq, final_norm)


def kernel(x, c, g_norm1, g_norm2, w_ada, b_ada, fa_w_in, fa_w_out, sg_w_in, sg_b_in, sg_g_v, sg_w_s, sg_b_s, sg_w_out, w_group, b_group, w_router, b_router, w_gate, w_up, w_down, g_final):
    n_batch, seq, d = x.shape
    depth = w_ada.shape[0]
    t = n_batch * seq
    x2 = x.reshape(t, d)

    mod = _ada(c, w_ada, b_ada)
    cs_chan, cs_seq = _dft_constants(seq, d // F_GROUPS)
    gfin = g_final.reshape(1, d)

    for l in range(depth):
        parts = [mod[l, :n_batch, k * d:(k + 1) * d].reshape(n_batch, 1, d) for k in range(6)]
        sh1, sc1, gt1, sh2, sc2, gt2 = parts
        g1 = g_norm1[l].reshape(1, d)
        j = l // 2
        if l % 2 == 0:
            ab = _fnet_in(x2, g1, sh1, sc1, fa_w_in[j].astype(BF16), jnp.asarray(cs_chan), n_batch, seq)
            x2 = _fnet_seq(jnp.asarray(cs_seq), ab, _fold_w_out(fa_w_out[j]), x2, gt1, n_batch, seq)
        else:
            x2 = _sgu(x2, g1, sh1, sc1, gt1, sg_w_in[j].astype(BF16), sg_b_in[j].reshape(1, 2 * d),
                      sg_g_v[j].reshape(1, d), sg_w_s[j].astype(BF16), sg_b_s[j].T, sg_w_out[j].astype(BF16), seq)
        x2 = _moe(x2, g_norm2[l].reshape(1, d), sh2, sc2, gt2, w_group[l], b_group[l], w_router[l], b_router[l],
                  w_gate, w_up, w_down, l, gfin, seq, final_norm=(l == depth - 1))
    return x2.reshape(n_batch, seq, d)
```

```python
import functools

import numpy as np
import jax
import jax.numpy as jnp
from jax import lax
from jax.experimental import pallas as pl
from jax.experimental.pallas import tpu as pltpu

F32 = jnp.float32
BF16 = jnp.bfloat16

EPS = 1e-6
F_GROUPS = 4
SGU_HEADS = 8
CHUNK = 128
N_GROUPS = 4
E_PER_GROUP = 8
N_EXPERTS = N_GROUPS * E_PER_GROUP
TOP_K = 2
EXPERT_BLOCK = 256

LANES = 128
SUBLANES = 8
VMEM_LIMIT = 56 * 1024 * 1024
NEG_BIG = -1e30


def _cparams(sem, vmem=VMEM_LIMIT):
    return pltpu.CompilerParams(dimension_semantics=sem, vmem_limit_bytes=vmem)


def _rms(x, g):
    return x * lax.rsqrt(jnp.mean(x * x, axis=-1, keepdims=True) + EPS) * g


def _store_packed(ref, vals_bf16, n_tok):
    pk = vals_bf16.shape[1] // (2 * LANES)
    bits = pltpu.bitcast(vals_bf16.astype(F32), jnp.uint32)
    for s in range(pk):
        low = bits[:, s * LANES:(s + 1) * LANES]
        high = bits[:, (s + pk) * LANES:(s + pk + 1) * LANES]
        ref[pl.ds(s, n_tok, stride=pk), :] = (high & jnp.uint32(0xFFFF0000)) | (low >> jnp.uint32(16))


def _load_packed(ref, n_tok, pk, dtype):
    lows, highs = [], []
    for s in range(pk):
        w = ref[pl.ds(s, n_tok, stride=pk), :]
        lows.append(pltpu.bitcast(w << jnp.uint32(16), F32).astype(dtype))
        highs.append(pltpu.bitcast(w & jnp.uint32(0xFFFF0000), F32).astype(dtype))
    return jnp.concatenate(lows + highs, axis=1)


ADA_TN = 1024


def _ada_kernel(c_ref, w_ref, b_ref, o_ref, lhs_ref):
    @pl.when((pl.program_id(0) == 0) & (pl.program_id(1) == 0))
    def _():
        cv = c_ref[...]
        s = cv / (1.0 + jnp.exp(-cv))
        s_hi = s.astype(BF16).astype(F32)
        lhs_ref[...] = jnp.concatenate([s_hi, s - s_hi], axis=0).astype(BF16)

    w = w_ref[0]
    w_hi = w.astype(BF16)
    w_lo = (w - w_hi.astype(F32)).astype(BF16)
    lhs = lhs_ref[...]
    r = jnp.dot(lhs, w_hi, preferred_element_type=F32) + jnp.dot(lhs, w_lo, preferred_element_type=F32)
    o_ref[0] = r[:SUBLANES] + r[SUBLANES:] + b_ref[0]


def _ada(c, w_ada, b_ada):
    n_batch, d = c.shape
    depth, _, n6 = w_ada.shape
    assert n_batch <= SUBLANES
    c8 = jnp.pad(c, ((0, SUBLANES - n_batch), (0, 0)))
    return pl.pallas_call(
        _ada_kernel,
        grid=(depth, n6 // ADA_TN),
        in_specs=[
            pl.BlockSpec((SUBLANES, d), lambda l, j: (0, 0)),
            pl.BlockSpec((1, d, ADA_TN), lambda l, j: (l, 0, j)),
            pl.BlockSpec((1, 1, ADA_TN), lambda l, j: (l, 0, j)),
        ],
        out_specs=pl.BlockSpec((1, SUBLANES, ADA_TN), lambda l, j: (l, 0, j)),
        out_shape=jax.ShapeDtypeStruct((depth, SUBLANES, n6), F32),
        scratch_shapes=[pltpu.VMEM((2 * SUBLANES, d), BF16)],
        compiler_params=_cparams(("arbitrary", "arbitrary")),
        name="ada_mod",
    )(c8, w_ada, b_ada.reshape(depth, 1, n6))


FN_TM = 256


def _fnet_in_kernel(x_ref, g_ref, sh_ref, sc_ref, win_ref, cs_ref, ab_ref, xs_ref):
    tm, d = x_ref.shape
    dg = d // F_GROUPS
    half = tm // 2
    nch = d // LANES
    for c in range(nch):
        xs_ref[c] = x_ref[:, c * LANES:(c + 1) * LANES]
    x = jnp.concatenate(
        [jnp.concatenate([xs_ref[c, pl.ds(par, half, stride=2), :] for c in range(nch)], axis=1)
         for par in range(2)], axis=0)
    h = _rms(x, g_ref[...]) * (1.0 + sc_ref[0]) + sh_ref[0]
    z = jnp.dot(h.astype(BF16), win_ref[...], preferred_element_type=F32)
    hk = dg // 2
    pw = F_GROUPS * hk + LANES
    lane = lax.broadcasted_iota(jnp.int32, (1, dg), 1)
    alt = (1 - 2 * (lane & 1)).astype(F32)
    lane_n = lax.broadcasted_iota(jnp.int32, (tm, LANES), 1)
    nyq = jnp.zeros((tm, LANES), F32)
    for g in range(F_GROUPS):
        zg = z[:, g * dg:(g + 1) * dg]
        ab = jnp.dot(zg.astype(BF16), cs_ref[...], preferred_element_type=F32)
        nyq = jnp.where(lane_n == g, jnp.sum(zg * alt, axis=-1, keepdims=True), nyq)
        for par in range(2):
            rows = slice(par * half, (par + 1) * half)
            ab_ref[0, par, :, g * hk:(g + 1) * hk] = ab[rows, :hk].astype(BF16)
            ab_ref[0, par, :, pw + g * hk:pw + (g + 1) * hk] = ab[rows, hk:].astype(BF16)
    for par in range(2):
        ab_ref[0, par, :, F_GROUPS * hk:pw] = nyq[par * half:(par + 1) * half].astype(BF16)


def _fnet_in(x2, g, sh, sc, w_in_bf, cs_chan, n_batch, seq):
    t, d = x2.shape
    tpb = seq // FN_TM
    dg = d // F_GROUPS
    rw = d + LANES
    return pl.pallas_call(
        _fnet_in_kernel,
        grid=(t // FN_TM,),
        in_specs=[
            pl.BlockSpec((FN_TM, d), lambda i: (i, 0)),
            pl.BlockSpec((1, d), lambda i: (0, 0)),
            pl.BlockSpec((1, 1, d), lambda i: (i // tpb, 0, 0)),
            pl.BlockSpec((1, 1, d), lambda i: (i // tpb, 0, 0)),
            pl.BlockSpec((d, d), lambda i: (0, 0), pipeline_mode=pl.Buffered(1)),
            pl.BlockSpec((dg, dg), lambda i: (0, 0), pipeline_mode=pl.Buffered(1)),
        ],
        out_specs=pl.BlockSpec((1, 2, FN_TM // 2, rw), lambda i: (i // tpb, 0, i % tpb, 0)),
        out_shape=jax.ShapeDtypeStruct((n_batch, 2, seq // 2, rw), BF16),
        scratch_shapes=[pltpu.VMEM((d // LANES, FN_TM, LANES), F32)],
        compiler_params=_cparams(("arbitrary",)),
        name="fnet_in",
    )(x2, g, sh, sc, w_in_bf, cs_chan)


FS_TK = 256
FS_SUB = 128


def _fnet_seq_kernel(cs_ref, ab_ref, wout_ref, x_ref, gt_ref, o_ref, *, scale):
    tk = cs_ref.shape[2]
    rw = ab_ref.shape[3]
    pw = rw - wout_ref.shape[1] // 2
    sub = FS_SUB
    for c in range(tk // sub):
        rows = slice(c * sub, (c + 1) * sub)
        pe = jnp.dot(cs_ref[0, 0, rows, :], ab_ref[0, 0, :, :pw], preferred_element_type=F32)
        po = jnp.dot(cs_ref[1, 0, rows, :], ab_ref[0, 1, :, :pw], preferred_element_type=F32)
        qe = jnp.dot(cs_ref[0, 1, rows, :], ab_ref[0, 0, :, pw:], preferred_element_type=F32)
        qo = jnp.dot(cs_ref[1, 1, rows, :], ab_ref[0, 1, :, pw:], preferred_element_type=F32)
        lo = jnp.concatenate([pe + po, qe + qo], axis=1)
        hi = jnp.concatenate([pe - po, qe - qo], axis=1)
        f = (jnp.concatenate([lo, hi], axis=0) * scale).astype(BF16)
        y = jnp.dot(f, wout_ref[...], preferred_element_type=F32)
        o_ref[0, 0, rows, :] = x_ref[0, 0, rows, :] + gt_ref[0] * y[:sub]
        o_ref[0, 1, rows, :] = x_ref[0, 1, rows, :] + gt_ref[0] * y[sub:]


def _fold_w_out_kernel(w_ref, o_ref):
    d = w_ref.shape[1]
    dg = d // F_GROUPS
    hk = dg // 2
    nc = F_GROUPS * hk
    pw = nc + LANES
    r = lax.broadcasted_iota(jnp.int32, (dg, dg), 0)
    c = lax.broadcasted_iota(jnp.int32, (dg, dg), 1)
    mirror = jnp.where((r + c == dg) | ((r == 0) & (c == 0)), 1.0, 0.0).astype(BF16)
    first = lax.broadcasted_iota(jnp.int32, (hk, 1), 0) == 0
    row_n = lax.broadcasted_iota(jnp.int32, (LANES, 1), 0)
    nyq = jnp.zeros((LANES, d), F32)
    for g in range(F_GROUPS):
        w = w_ref[g * dg:(g + 1) * dg, :]
        wm = jnp.dot(mirror, w.astype(BF16), preferred_element_type=F32)
        o_ref[g * hk:(g + 1) * hk, :] = jnp.where(first, w[:hk], w[:hk] + wm[:hk]).astype(BF16)
        o_ref[pw + g * hk:pw + (g + 1) * hk, :] = jnp.where(first, 0.0, wm[:hk] - w[:hk]).astype(BF16)
        nyq = jnp.where(row_n == g, w[hk:hk + 1], nyq)
    o_ref[nc:pw, :] = nyq.astype(BF16)


def _fold_w_out(w_out):
    d = w_out.shape[0]
    return pl.pallas_call(
        _fold_w_out_kernel,
        grid=(1,),
        in_specs=[pl.BlockSpec((d, d), lambda i: (0, 0), pipeline_mode=pl.Buffered(1))],
        out_specs=pl.BlockSpec((d + LANES, d), lambda i: (0, 0)),
        out_shape=jax.ShapeDtypeStruct((d + LANES, d), BF16),
        compiler_params=_cparams(("arbitrary",)),
        name="fold_w_out",
    )(w_out)


def _fnet_seq(cs_seq, ab, w_fold_bf, x2, gt, n_batch, seq):
    t, d = x2.shape
    hs = seq // 2
    rw = ab.shape[-1]
    scale = 1.0 / float(np.sqrt(seq * (d // F_GROUPS)))
    x4 = x2.reshape(n_batch, 2, hs, d)
    out = pl.pallas_call(
        functools.partial(_fnet_seq_kernel, scale=scale),
        grid=(n_batch, hs // FS_TK),
        in_specs=[
            pl.BlockSpec((2, 2, FS_TK, hs), lambda b, k: (0, 0, k, 0)),
            pl.BlockSpec((1, 2, hs, rw), lambda b, k: (b, 0, 0, 0), pipeline_mode=pl.Buffered(1)),
            pl.BlockSpec((rw, d), lambda b, k: (0, 0), pipeline_mode=pl.Buffered(1)),
            pl.BlockSpec((1, 2, FS_TK, d), lambda b, k: (b, 0, k, 0)),
            pl.BlockSpec((1, 1, d), lambda b, k: (b, 0, 0)),
        ],
        out_specs=pl.BlockSpec((1, 2, FS_TK, d), lambda b, k: (b, 0, k, 0)),
        out_shape=jax.ShapeDtypeStruct((n_batch, 2, hs, d), F32),
        compiler_params=_cparams(("arbitrary", "arbitrary")),
        name="fnet_seq",
    )(cs_seq, ab, w_fold_bf, x4, gt)
    return out.reshape(t, d)


SG_TM = 256


def _gelu_tanh(x):
    c = float(np.sqrt(2.0 / np.pi))
    return x * (0.5 * (1.0 + jnp.tanh(c * (x + 0.044715 * (x * x * x)))))


def _sgu_kernel(x_ref, g_ref, sh_ref, sc_ref, gt_ref, win_ref, bin_ref, gv_ref, ws_ref, bs_ref,
                wout_ref, o_ref, gated_ref):
    tm, d = x_ref.shape
    dh = d // SGU_HEADS
    for c in range(tm // CHUNK):
        rows = slice(c * CHUNK, (c + 1) * CHUNK)
        x = x_ref[rows, :]
        h = _rms(x, g_ref[...]) * (1.0 + sc_ref[0]) + sh_ref[0]
        z = jnp.dot(h.astype(BF16), win_ref[...], preferred_element_type=F32) + bin_ref[...]
        z = _gelu_tanh(z)
        u = z[:, :d]
        v = _rms(z[:, d:], gv_ref[...])
        for hd in range(SGU_HEADS):
            cols = slice(hd * dh, (hd + 1) * dh)
            vc = v[:, cols].astype(BF16)
            m = jnp.dot(ws_ref[hd], vc, preferred_element_type=F32) + bs_ref[:, hd:hd + 1]
            gated_ref[rows, cols] = (u[:, cols] * m).astype(BF16)
        y = jnp.dot(gated_ref[rows, :], wout_ref[...], preferred_element_type=F32)
        o_ref[rows, :] = x + gt_ref[0] * y


def _sgu(x2, g, sh, sc, gt, w_in_bf, b_in, g_v, w_s_bf, b_s_t, w_out_bf, seq):
    t, d = x2.shape
    tpb = seq // SG_TM
    const2 = lambda i: (0, 0)
    perb = lambda i: (i // tpb, 0, 0)
    return pl.pallas_call(
        _sgu_kernel,
        grid=(t // SG_TM,),
        in_specs=[
            pl.BlockSpec((SG_TM, d), lambda i: (i, 0)),
            pl.BlockSpec((1, d), const2),
            pl.BlockSpec((1, 1, d), perb),
            pl.BlockSpec((1, 1, d), perb),
            pl.BlockSpec((1, 1, d), perb),
            pl.BlockSpec((d, 2 * d), const2, pipeline_mode=pl.Buffered(1)),
            pl.BlockSpec((1, 2 * d), const2),
            pl.BlockSpec((1, d), const2),
            pl.BlockSpec((SGU_HEADS, CHUNK, CHUNK), lambda i: (0, 0, 0)),
            pl.BlockSpec((CHUNK, SGU_HEADS), const2),
            pl.BlockSpec((d, d), const2, pipeline_mode=pl.Buffered(1)),
        ],
        out_specs=pl.BlockSpec((SG_TM, d), lambda i: (i, 0)),
        out_shape=jax.ShapeDtypeStruct((t, d), F32),
        scratch_shapes=[pltpu.VMEM((SG_TM, d), BF16)],
        compiler_params=_cparams(("arbitrary",)),
        name="sgu_mix",
    )(x2, g, sh, sc, gt, w_in_bf, b_in, g_v, w_s_bf, b_s_t, w_out_bf)


RT_TM = 256


def _route_kernel(x_ref, g_ref, sh_ref, sc_ref, whl_ref, br_ref, h_ref, route_ref, cnt_ref, rt_ref, carry_ref):
    tm = x_ref.shape[0]
    i = pl.program_id(0)

    @pl.when(i == 0)
    def _():
        carry_ref[...] = jnp.zeros_like(carry_ref)

    h = _rms(x_ref[...], g_ref[...]) * (1.0 + sc_ref[0]) + sh_ref[0]
    h_hi = h.astype(BF16)
    _store_packed(h_ref, h_hi, tm)
    h_lo = (h - h_hi.astype(F32)).astype(BF16)
    whl = whl_ref[...]
    both = jnp.dot(h_hi, whl, preferred_element_type=F32)
    lo = jnp.dot(h_lo, whl[:, :LANES], preferred_element_type=F32)
    lg = both[:, :LANES] + both[:, LANES:] + lo + br_ref[...]

    lane = lax.broadcasted_iota(jnp.int32, (tm, LANES), 1)
    lanef = lane.astype(F32)
    is_g = lane < N_GROUPS
    gl = jnp.where(is_g, lg, NEG_BIG)
    gmax = jnp.max(gl, axis=1, keepdims=True)
    gidx = jnp.min(jnp.where(gl == gmax, lanef, float(LANES)), axis=1, keepdims=True)
    gsum = jnp.sum(jnp.where(is_g, jnp.exp(gl - gmax), 0.0), axis=1, keepdims=True)
    g_w = 1.0 / gsum
    lo_lane = float(N_GROUPS) + gidx * float(E_PER_GROUP)
    in_grp = (lanef >= lo_lane) & (lanef < lo_lane + float(E_PER_GROUP))
    el = jnp.where(in_grp, lg, NEG_BIG)
    v1 = jnp.max(el, axis=1, keepdims=True)
    i1 = jnp.min(jnp.where(el == v1, lanef, float(LANES)), axis=1, keepdims=True)
    el2 = jnp.where(lanef == i1, NEG_BIG, el)
    v2 = jnp.max(el2, axis=1, keepdims=True)
    i2 = jnp.min(jnp.where(el2 == v2, lanef, float(LANES)), axis=1, keepdims=True)
    p = jnp.exp(v2 - v1)
    w0 = g_w / (1.0 + p)
    w1 = g_w * p / (1.0 + p)
    e0 = i1 - float(N_GROUPS)
    e1 = i2 - float(N_GROUPS)

    oh0 = jnp.where(lanef == e0, 1.0, 0.0)
    oh1 = jnp.where(lanef == e1, 1.0, 0.0)
    rr = lax.broadcasted_iota(jnp.int32, (tm, tm), 0)
    cc = lax.broadcasted_iota(jnp.int32, (tm, tm), 1)
    tri = jnp.where(rr > cc, 1.0, 0.0).astype(BF16)
    pre0 = jnp.dot(tri, oh0.astype(BF16), preferred_element_type=F32)
    pre1 = jnp.dot(tri, oh1.astype(BF16), preferred_element_type=F32)
    carry = carry_ref[0:1, :]
    cnt0 = jnp.sum(oh0, axis=0, keepdims=True)
    cnt1 = jnp.sum(oh1, axis=0, keepdims=True)
    rank0 = jnp.sum(oh0 * (pre0 + carry), axis=1, keepdims=True)
    rank1 = jnp.sum(oh1 * (pre1 + carry + cnt0), axis=1, keepdims=True)
    new_carry = carry + cnt0 + cnt1
    carry_ref[...] = jnp.broadcast_to(new_carry, carry_ref.shape)
    cnt_ref[...] = jnp.broadcast_to(new_carry, cnt_ref.shape)

    route = jnp.where(lane == 0, e0, 0.0)
    route = jnp.where(lane == 1, e1, route)
    route = jnp.where(lane == 2, w0, route)
    route = jnp.where(lane == 3, w1, route)
    route = jnp.where(lane == 4, rank0, route)
    route = jnp.where(lane == 5, rank1, route)
    route_ref[...] = route
    rt_ref[...] = jnp.transpose(route)[:SUBLANES, :]


def _route(x2, g, sh, sc, whl, br, seq):
    t, d = x2.shape
    tpb = seq // RT_TM
    pk = d // (2 * LANES)
    perb = lambda i: (i // tpb, 0, 0)
    return pl.pallas_call(
        _route_kernel,
        grid=(t // RT_TM,),
        in_specs=[
            pl.BlockSpec((RT_TM, d), lambda i: (i, 0)),
            pl.BlockSpec((1, d), lambda i: (0, 0)),
            pl.BlockSpec((1, 1, d), perb),
            pl.BlockSpec((1, 1, d), perb),
            pl.BlockSpec((d, 2 * LANES), lambda i: (0, 0)),
            pl.BlockSpec((1, LANES), lambda i: (0, 0)),
        ],
        out_specs=[
            pl.BlockSpec((RT_TM * pk, LANES), lambda i: (i, 0)),
            pl.BlockSpec((RT_TM, LANES), lambda i: (i, 0)),
            pl.BlockSpec((SUBLANES, LANES), lambda i: (0, 0)),
            pl.BlockSpec((SUBLANES, RT_TM), lambda i: (0, i)),
        ],
        out_shape=[
            jax.ShapeDtypeStruct((t * pk, LANES), jnp.uint32),
            jax.ShapeDtypeStruct((t, LANES), F32),
            jax.ShapeDtypeStruct((SUBLANES, LANES), F32),
            jax.ShapeDtypeStruct((SUBLANES, t), F32),
        ],
        scratch_shapes=[pltpu.VMEM((SUBLANES, LANES), F32)],
        compiler_params=_cparams(("arbitrary",)),
        name="moe_route",
    )(x2, g, sh, sc, whl, br)


SC_TM = 1024


SC_UNROLL = 8


RT_E, RT_RANK = 0, 4


def _dest_row(er_ref, pstart_ref, tok, k):
    return pstart_ref[er_ref[RT_E + k, tok]] + er_ref[RT_RANK + k, tok]


def _scatter_kernel(er_ref, pstart_ref, pad_start_ref, pad_n_ref, nu_ref, h_ref, buf_ref, zrow_ref, sem, zsem, tsem,
                    *, pk):
    tm = h_ref.shape[0] // pk
    i = pl.program_id(0)
    blk_rows = EXPERT_BLOCK * pk
    n_blocks = buf_ref.shape[0] // blk_rows

    def tok_rows(tok):
        return pl.ds(pl.multiple_of(tok * pk, pk), pk)

    def row_copy(r, k):
        dst = _dest_row(er_ref, pstart_ref, i * tm + r, k)
        return pltpu.make_async_copy(h_ref.at[tok_rows(r), :], buf_ref.at[tok_rows(dst), :], sem)

    def pad_fill(e, do):
        n = pad_n_ref[e]
        start = pad_start_ref[e]
        size = EXPERT_BLOCK // 2
        while size >= 1:
            has = (n & size) != 0
            cp = pltpu.make_async_copy(zrow_ref.at[pl.ds(0, size * pk), :],
                                       buf_ref.at[pl.ds(pl.multiple_of(start * pk, pk), size * pk), :], zsem)

            @pl.when(has)
            def _():
                do(cp)

            start = start + jnp.where(has, size, 0)
            size //= 2

    def tail_copy(blk):
        row0 = pl.multiple_of(blk * blk_rows, blk_rows)
        return pltpu.make_async_copy(zrow_ref, buf_ref.at[pl.ds(row0, blk_rows), :], tsem)

    @pl.when(i == 0)
    def _():
        zrow_ref[...] = jnp.zeros_like(zrow_ref)

        def per_expert(e, c):
            pad_fill(e, lambda cp: cp.start())
            return c
        lax.fori_loop(0, N_EXPERTS, per_expert, 0)
        lax.fori_loop(nu_ref[0], n_blocks, lambda blk, c: (tail_copy(blk).start(), c)[1], 0)

    def issue(q, c):
        for u in range(SC_UNROLL):
            for k in range(TOP_K):
                row_copy(q * SC_UNROLL + u, k).start(priority=k)
        return c
    lax.fori_loop(0, tm // SC_UNROLL, issue, 0)

    def drain(q, c):
        for u in range(SC_UNROLL):
            for k in range(TOP_K):
                row_copy(q * SC_UNROLL + u, k).wait()
        return c
    lax.fori_loop(0, tm // SC_UNROLL, drain, 0)

    @pl.when(i == 0)
    def _():
        def per_expert(e, c):
            pad_fill(e, lambda cp: cp.wait())
            return c
        lax.fori_loop(0, N_EXPERTS, per_expert, 0)
        lax.fori_loop(nu_ref[0], n_blocks, lambda blk, c: (tail_copy(blk).wait(), c)[1], 0)


def _scatter(er, pstarts, pad_start, pad_n, n_used, hp, n_rows, pk):
    t = hp.shape[0] // pk
    grid_spec = pltpu.PrefetchScalarGridSpec(
        num_scalar_prefetch=5,
        grid=(t // SC_TM,),
        in_specs=[pl.BlockSpec((SC_TM * pk, LANES), lambda i, *_: (i, 0))],
        out_specs=pl.BlockSpec(memory_space=pl.ANY),
        scratch_shapes=[pltpu.VMEM((EXPERT_BLOCK * pk, LANES), hp.dtype), pltpu.SemaphoreType.DMA,
                        pltpu.SemaphoreType.DMA, pltpu.SemaphoreType.DMA],
    )
    return pl.pallas_call(
        functools.partial(_scatter_kernel, pk=pk),
        grid_spec=grid_spec,
        out_shape=jax.ShapeDtypeStruct((n_rows * pk, LANES), hp.dtype),
        compiler_params=_cparams(("arbitrary",)),
        name="moe_scatter",
    )(er, pstarts, pad_start, pad_n, n_used, hp)


EX_RING = 4
W_PARTS = 2


def _expert_kernel(pstart_ref, nblk_ref, nu_ref, wg_hbm, wu_hbm, wd_hbm, buf_ref, yb_ref,
                   wg_st, wu_st, wd_st, wg_bf, wu_bf, wd_bf, xbuf, obuf, w_sem, in_sem, out_sem, tail_sem,
                   *, layer, pk):
    e = pl.program_id(0)
    ne = pl.num_programs(0)
    nb = nblk_ref[e]
    g0 = pstart_ref[e]
    nu = nu_ref[0]
    xrows = EXPERT_BLOCK * pk
    orows = EXPERT_BLOCK * pk
    n_blocks = yb_ref.shape[0] // orows
    cur = e % 2
    nxt = (e + 1) % 2

    def w_copies(ex, slot):
        return [pltpu.make_async_copy(hbm.at[layer, ex], st.at[slot], w_sem.at[slot])
                for hbm, st in ((wg_hbm, wg_st), (wu_hbm, wu_st), (wd_hbm, wd_st))]

    def cast_part(slot, q):
        for st, bf in ((wg_st, wg_bf), (wu_st, wu_bf), (wd_st, wd_bf)):
            n = st.shape[1] // W_PARTS
            r = pl.ds(pl.multiple_of(q * n, n), n)
            bf[slot, r, :] = st[slot, r, :].astype(BF16)

    @pl.when(e == 0)
    def _():
        for c in w_copies(0, 0):
            c.start()
        for c in w_copies(1, 1):
            c.start()
        for c in w_copies(0, 0):
            c.wait()
        for q in range(W_PARTS):
            cast_part(0, q)
        for c in w_copies(2, 0):
            c.start()

    @pl.when((e >= 1) & (e + 2 < ne))
    def _():
        for c in w_copies(e + 2, cur):
            c.start()

    @pl.when(e + 1 < ne)
    def _():
        for c in w_copies(e + 1, nxt):
            c.wait()

    def x_copy(g):
        slot = g % EX_RING
        src = buf_ref.at[pl.ds(pl.multiple_of(g * xrows, xrows), xrows), :]
        return pltpu.make_async_copy(src, xbuf.at[slot], in_sem.at[slot])

    def o_copy(g):
        slot = g % EX_RING
        dst = yb_ref.at[pl.ds(pl.multiple_of(g * orows, orows), orows), :]
        return pltpu.make_async_copy(obuf.at[slot], dst, out_sem.at[slot])

    @pl.when(e == 0)
    def _():
        for k in range(2):
            @pl.when(k < nu)
            def _():
                x_copy(k).start()

    def stage(g, n):
        for k in range(n):
            @pl.when(g + 2 + k < nu)
            def _():
                x_copy(g + 2 + k).start()
        for k in range(n):
            x_copy(g + k).wait()

            @pl.when(g + k >= EX_RING)
            def _():
                o_copy(g + k - EX_RING).wait()

    def compute(g):
        slot = g % EX_RING
        xb = _load_packed(xbuf.at[slot], EXPERT_BLOCK, pk, BF16)
        gte = jnp.dot(xb, wg_bf[cur], preferred_element_type=F32)
        up = jnp.dot(xb, wu_bf[cur], preferred_element_type=F32)
        act = gte * (1.0 / (1.0 + jnp.exp(-gte))) * up
        y = jnp.dot(act.astype(BF16), wd_bf[cur], preferred_element_type=F32)
        _store_packed(obuf.at[slot], y.astype(BF16), EXPERT_BLOCK)

    def pair_body(p, c):
        g = g0 + 2 * p
        stage(g, 2)
        compute(g)
        compute(g + 1)
        o_copy(g).start()
        o_copy(g + 1).start()
        cast_part(nxt, jnp.minimum(p, W_PARTS - 1))
        return c

    n_pairs = nb // 2
    lax.fori_loop(0, n_pairs, pair_body, 0)

    @pl.when(nb % 2 == 1)
    def _():
        g = g0 + nb - 1
        stage(g, 1)
        compute(g)
        o_copy(g).start()

    lax.fori_loop(jnp.minimum(n_pairs, W_PARTS), W_PARTS, lambda q, c: (cast_part(nxt, q), c)[1], 0)

    @pl.when(e == ne - 1)
    def _():
        lax.fori_loop(jnp.maximum(nu - EX_RING, 0), nu, lambda g, c: (o_copy(g).wait(), c)[1], 0)
        obuf[0] = jnp.zeros(obuf.shape[1:], obuf.dtype)

        def tail_copy(blk):
            r = pl.ds(pl.multiple_of(blk * orows, orows), orows)
            return pltpu.make_async_copy(obuf.at[0], yb_ref.at[r, :], tail_sem)

        lax.fori_loop(nu, n_blocks, lambda blk, c: (tail_copy(blk).start(), c)[1], 0)
        lax.fori_loop(nu, n_blocks, lambda blk, c: (tail_copy(blk).wait(), c)[1], 0)


def _experts(gstarts, nblk, n_used, buf, w_gate, w_up, w_down, layer, n_rows, pk):
    d, de = w_gate.shape[-2:]
    assert N_EXPERTS >= 3 and d % W_PARTS == 0 and de % W_PARTS == 0
    hbm = pl.BlockSpec(memory_space=pl.ANY)
    grid_spec = pltpu.PrefetchScalarGridSpec(
        num_scalar_prefetch=3,
        grid=(N_EXPERTS,),
        in_specs=[hbm, hbm, hbm, hbm],
        out_specs=hbm,
        scratch_shapes=[
            pltpu.VMEM((2, d, de), F32), pltpu.VMEM((2, d, de), F32), pltpu.VMEM((2, de, d), F32),
            pltpu.VMEM((2, d, de), BF16), pltpu.VMEM((2, d, de), BF16), pltpu.VMEM((2, de, d), BF16),
            pltpu.VMEM((EX_RING, EXPERT_BLOCK * pk, LANES), buf.dtype),
            pltpu.VMEM((EX_RING, EXPERT_BLOCK * pk, LANES), buf.dtype),
            pltpu.SemaphoreType.DMA((2,)), pltpu.SemaphoreType.DMA((EX_RING,)), pltpu.SemaphoreType.DMA((EX_RING,)),
            pltpu.SemaphoreType.DMA,
        ],
    )
    return pl.pallas_call(
        functools.partial(_expert_kernel, layer=layer, pk=pk),
        grid_spec=grid_spec,
        out_shape=jax.ShapeDtypeStruct((n_rows * pk, LANES), buf.dtype),
        compiler_params=_cparams(("arbitrary",)),
        name="moe_experts",
    )(gstarts, nblk, n_used, w_gate, w_up, w_down, buf)


CB_TM = 256


def _combine_kernel(er_ref, pstart_ref, x_ref, route_ref, gt_ref, gf_ref, yb_ref, o_ref, ybuf, sems, *, final_norm):
    tm, d = x_ref.shape
    fs = d // (2 * LANES)
    i = pl.program_id(0)
    n = pl.num_programs(0)

    def row_copy(tile, slot, r, k):
        src = _dest_row(er_ref, pstart_ref, tile * tm + r, k)
        return pltpu.make_async_copy(yb_ref.at[pl.ds(pl.multiple_of(src * fs, fs), fs), :],
                                     ybuf.at[slot, k, pl.ds(pl.multiple_of(r * fs, fs), fs), :], sems.at[slot])

    def issue(tile, slot):
        def body(q, c):
            for u in range(SC_UNROLL):
                for k in range(TOP_K):
                    row_copy(tile, slot, q * SC_UNROLL + u, k).start(priority=k)
            return c
        lax.fori_loop(0, tm // SC_UNROLL, body, 0)

    @pl.when(i == 0)
    def _():
        issue(0, 0)

    @pl.when(i + 1 < n)
    def _():
        issue(i + 1, (i + 1) % 2)

    slot = i % 2

    def drain(q, c):
        for u in range(SC_UNROLL):
            for k in range(TOP_K):
                row_copy(i, slot, q * SC_UNROLL + u, k).wait()
        return c
    lax.fori_loop(0, tm // SC_UNROLL, drain, 0)

    route = route_ref[...]
    ys = [_load_packed(ybuf.at[slot, k], tm, fs, F32) for k in range(TOP_K)]
    y = ys[0] * route[:, 2:3] + ys[1] * route[:, 3:4]
    xn = x_ref[...] + gt_ref[0] * y
    if final_norm:
        xn = _rms(xn, gf_ref[...])
    o_ref[...] = xn


def _combine(er, pstarts, x2, route, gt, g_final, yb, seq, final_norm):
    t, d = x2.shape
    tpb = seq // CB_TM
    grid_spec = pltpu.PrefetchScalarGridSpec(
        num_scalar_prefetch=2,
        grid=(t // CB_TM,),
        in_specs=[
            pl.BlockSpec((CB_TM, d), lambda i, *_: (i, 0)),
            pl.BlockSpec((CB_TM, LANES), lambda i, *_: (i, 0)),
            pl.BlockSpec((1, 1, d), lambda i, *_: (i // tpb, 0, 0)),
            pl.BlockSpec((1, d), lambda i, *_: (0, 0)),
            pl.BlockSpec(memory_space=pl.ANY),
        ],
        out_specs=pl.BlockSpec((CB_TM, d), lambda i, *_: (i, 0)),
        scratch_shapes=[pltpu.VMEM((2, TOP_K, CB_TM * (d // (2 * LANES)), LANES), yb.dtype),
                        pltpu.SemaphoreType.DMA((2,))],
    )
    return pl.pallas_call(
        functools.partial(_combine_kernel, final_norm=final_norm),
        grid_spec=grid_spec,
        out_shape=jax.ShapeDtypeStruct((t, d), F32),
        compiler_params=_cparams(("arbitrary",)),
        name="moe_combine",
    )(er, pstarts, x2, route, gt, g_final, yb)


def _dft_cos_sin(n):
    k = np.arange(n, dtype=np.int64)
    ang = (np.outer(k, k) % n).astype(np.float64) * (2.0 * np.pi / n)
    return np.cos(ang), np.sin(ang)


@functools.lru_cache(maxsize=None)
def _dft_constants(seq, dg):
    cc, sc = _dft_cos_sin(dg)
    hk = dg // 2
    cs_chan = np.concatenate([cc[:, :hk], sc[:, :hk]], axis=1).astype(np.float32)
    k = np.arange(seq // 2, dtype=np.int64)[:, None]
    parts = []
    for par in range(2):
        n = 2 * np.arange(seq // 2, dtype=np.int64)[None, :] + par
        ang = ((k * n) % seq).astype(np.float64) * (2.0 * np.pi / seq)
        parts.append(np.stack([np.cos(ang), np.sin(ang)], axis=0))
    cs_seq = np.stack(parts, axis=0).astype(np.float32)
    return cs_chan.astype(BF16), cs_seq.astype(BF16)


def _moe(x2, g2, sh, sc, gt, w_group, b_group, w_router, b_router, w_gate, w_up, w_down, layer,
         g_final, seq, final_norm):
    t, d = x2.shape
    w_all = jnp.concatenate([w_group, w_router], axis=1)
    w_all = jnp.pad(w_all, ((0, 0), (0, LANES - w_all.shape[1])))
    w_hi = w_all.astype(BF16)
    w_lo = (w_all - w_hi.astype(F32)).astype(BF16)
    whl = jnp.concatenate([w_hi, w_lo], axis=1)
    br = jnp.pad(jnp.concatenate([b_group, b_router]), (0, LANES - N_GROUPS - N_EXPERTS)).reshape(1, LANES)

    pk = d // (2 * LANES)
    hp, route, cnt, rt = _route(x2, g2, sh, sc, whl, br, seq)

    counts = cnt[0, :N_EXPERTS].astype(jnp.int32)
    padded = (counts + EXPERT_BLOCK - 1) // EXPERT_BLOCK * EXPERT_BLOCK
    pends = jnp.cumsum(padded)
    pstarts = pends - padded
    n_assign = t * TOP_K
    n_blocks = (n_assign + N_EXPERTS * (EXPERT_BLOCK - 1) + EXPERT_BLOCK - 1) // EXPERT_BLOCK
    n_rows = n_blocks * EXPERT_BLOCK
    er = rt.astype(jnp.int32)
    pstarts = pstarts.astype(jnp.int32)
    n_used = (pends[-1:] // EXPERT_BLOCK).astype(jnp.int32)

    buf = _scatter(er, pstarts, (pstarts + counts).astype(jnp.int32), (padded - counts).astype(jnp.int32), n_used,
                   hp, n_rows, pk)
    yb = _experts((pstarts // EXPERT_BLOCK).astype(jnp.int32), (padded // EXPERT_BLOCK).astype(jnp.int32), n_used,
                  buf, w_gate, w_up, w_down, layer, n_rows, pk)
    return _combine(er, pstarts, x2, route, gt, g_final, yb, seq, final_norm)


def kernel(x, c, g_norm1, g_norm2, w_ada, b_ada, fa_w_in, fa_w_out, sg_w_in, sg_b_in, sg_g_v, sg_w_s, sg_b_s, sg_w_out, w_group, b_group, w_router, b_router, w_gate, w_up, w_down, g_final):
    n_batch, seq, d = x.shape
    depth = w_ada.shape[0]
    t = n_batch * seq
    x2 = x.reshape(t, d)

    mod = _ada(c, w_ada, b_ada)
    cs_chan, cs_seq = _dft_constants(seq, d // F_GROUPS)
    gfin = g_final.reshape(1, d)

    for l in range(depth):
        parts = [mod[l, :n_batch, k * d:(k + 1) * d].reshape(n_batch, 1, d) for k in range(6)]
        sh1, sc1, gt1, sh2, sc2, gt2 = parts
        g1 = g_norm1[l].reshape(1, d)
        j = l // 2
        if l % 2 == 0:
            ab = _fnet_in(x2, g1, sh1, sc1, fa_w_in[j].astype(BF16), jnp.asarray(cs_chan), n_batch, seq)
            x2 = _fnet_seq(jnp.asarray(cs_seq), ab, _fold_w_out(fa_w_out[j]), x2, gt1, n_batch, seq)
        else:
            x2 = _sgu(x2, g1, sh1, sc1, gt1, sg_w_in[j].astype(BF16), sg_b_in[j].reshape(1, 2 * d),
                      sg_g_v[j].reshape(1, d), sg_w_s[j].astype(BF16), sg_b_s[j].T, sg_w_out[j].astype(BF16), seq)
        x2 = _moe(x2, g_norm2[l].reshape(1, d), sh2, sc2, gt2, w_group[l], b_group[l], w_router[l], b_router[l],
                  w_gate, w_up, w_down, l, gfin, seq, final_norm=(l == depth - 1))
    return x2.reshape(n_batch, seq, d)
```

```python
import functools

import numpy as np
import jax
import jax.numpy as jnp
from jax import lax
from jax.experimental import pallas as pl
from jax.experimental.pallas import tpu as pltpu

F32 = jnp.float32
BF16 = jnp.bfloat16

EPS = 1e-6
F_GROUPS = 4
SGU_HEADS = 8
CHUNK = 128
N_GROUPS = 4
E_PER_GROUP = 8
N_EXPERTS = N_GROUPS * E_PER_GROUP
TOP_K = 2
EXPERT_BLOCK = 256

LANES = 128
SUBLANES = 8
VMEM_LIMIT = 56 * 1024 * 1024
NEG_BIG = -1e30


def _cparams(sem, vmem=VMEM_LIMIT):
    return pltpu.CompilerParams(dimension_semantics=sem, vmem_limit_bytes=vmem)


def _rms(x, g):
    return x * lax.rsqrt(jnp.mean(x * x, axis=-1, keepdims=True) + EPS) * g


def _store_packed(ref, vals_bf16, n_tok):
    pk = vals_bf16.shape[1] // (2 * LANES)
    bits = pltpu.bitcast(vals_bf16.astype(F32), jnp.uint32)
    for s in range(pk):
        low = bits[:, s * LANES:(s + 1) * LANES]
        high = bits[:, (s + pk) * LANES:(s + pk + 1) * LANES]
        ref[pl.ds(s, n_tok, stride=pk), :] = (high & jnp.uint32(0xFFFF0000)) | (low >> jnp.uint32(16))


def _load_packed(ref, n_tok, pk, dtype):
    lows, highs = [], []
    for s in range(pk):
        w = ref[pl.ds(s, n_tok, stride=pk), :]
        lows.append(pltpu.bitcast(w << jnp.uint32(16), F32).astype(dtype))
        highs.append(pltpu.bitcast(w & jnp.uint32(0xFFFF0000), F32).astype(dtype))
    return jnp.concatenate(lows + highs, axis=1)


ADA_TN = 1024


def _ada_kernel(c_ref, w_ref, b_ref, o_ref, lhs_ref):
    @pl.when((pl.program_id(0) == 0) & (pl.program_id(1) == 0))
    def _():
        cv = c_ref[...]
        s = cv / (1.0 + jnp.exp(-cv))
        s_hi = s.astype(BF16).astype(F32)
        lhs_ref[...] = jnp.concatenate([s_hi, s - s_hi], axis=0).astype(BF16)

    w = w_ref[0]
    w_hi = w.astype(BF16)
    w_lo = (w - w_hi.astype(F32)).astype(BF16)
    lhs = lhs_ref[...]
    r = jnp.dot(lhs, w_hi, preferred_element_type=F32) + jnp.dot(lhs, w_lo, preferred_element_type=F32)
    o_ref[0] = r[:SUBLANES] + r[SUBLANES:] + b_ref[0]


def _ada(c, w_ada, b_ada):
    n_batch, d = c.shape
    depth, _, n6 = w_ada.shape
    assert n_batch <= SUBLANES
    c8 = jnp.pad(c, ((0, SUBLANES - n_batch), (0, 0)))
    return pl.pallas_call(
        _ada_kernel,
        grid=(depth, n6 // ADA_TN),
        in_specs=[
            pl.BlockSpec((SUBLANES, d), lambda l, j: (0, 0)),
            pl.BlockSpec((1, d, ADA_TN), lambda l, j: (l, 0, j)),
            pl.BlockSpec((1, 1, ADA_TN), lambda l, j: (l, 0, j)),
        ],
        out_specs=pl.BlockSpec((1, SUBLANES, ADA_TN), lambda l, j: (l, 0, j)),
        out_shape=jax.ShapeDtypeStruct((depth, SUBLANES, n6), F32),
        scratch_shapes=[pltpu.VMEM((2 * SUBLANES, d), BF16)],
        compiler_params=_cparams(("arbitrary", "arbitrary")),
        name="ada_mod",
    )(c8, w_ada, b_ada.reshape(depth, 1, n6))


FN_TM = 256


def _fnet_in_kernel(x_ref, g_ref, sh_ref, sc_ref, win_ref, cs_ref, ab_ref, xs_ref):
    tm, d = x_ref.shape
    dg = d // F_GROUPS
    half = tm // 2
    nch = d // LANES
    for c in range(nch):
        xs_ref[c] = x_ref[:, c * LANES:(c + 1) * LANES]
    x = jnp.concatenate(
        [jnp.concatenate([xs_ref[c, pl.ds(par, half, stride=2), :] for c in range(nch)], axis=1)
         for par in range(2)], axis=0)
    h = _rms(x, g_ref[...]) * (1.0 + sc_ref[0]) + sh_ref[0]
    z = jnp.dot(h.astype(BF16), win_ref[...], preferred_element_type=F32)
    hk = dg // 2
    pw = F_GROUPS * hk + LANES
    lane = lax.broadcasted_iota(jnp.int32, (1, dg), 1)
    alt = (1 - 2 * (lane & 1)).astype(F32)
    lane_n = lax.broadcasted_iota(jnp.int32, (tm, LANES), 1)
    nyq = jnp.zeros((tm, LANES), F32)
    for g in range(F_GROUPS):
        zg = z[:, g * dg:(g + 1) * dg]
        ab = jnp.dot(zg.astype(BF16), cs_ref[...], preferred_element_type=F32)
        nyq = jnp.where(lane_n == g, jnp.sum(zg * alt, axis=-1, keepdims=True), nyq)
        for par in range(2):
            rows = slice(par * half, (par + 1) * half)
            ab_ref[0, par, :, g * hk:(g + 1) * hk] = ab[rows, :hk].astype(BF16)
            ab_ref[0, par, :, pw + g * hk:pw + (g + 1) * hk] = ab[rows, hk:].astype(BF16)
    for par in range(2):
        ab_ref[0, par, :, F_GROUPS * hk:pw] = nyq[par * half:(par + 1) * half].astype(BF16)


def _fnet_in(x2, g, sh, sc, w_in_bf, cs_chan, n_batch, seq):
    t, d = x2.shape
    tpb = seq // FN_TM
    dg = d // F_GROUPS
    rw = d + LANES
    return pl.pallas_call(
        _fnet_in_kernel,
        grid=(t // FN_TM,),
        in_specs=[
            pl.BlockSpec((FN_TM, d), lambda i: (i, 0)),
            pl.BlockSpec((1, d), lambda i: (0, 0)),
            pl.BlockSpec((1, 1, d), lambda i: (i // tpb, 0, 0)),
            pl.BlockSpec((1, 1, d), lambda i: (i // tpb, 0, 0)),
            pl.BlockSpec((d, d), lambda i: (0, 0), pipeline_mode=pl.Buffered(1)),
            pl.BlockSpec((dg, dg), lambda i: (0, 0), pipeline_mode=pl.Buffered(1)),
        ],
        out_specs=pl.BlockSpec((1, 2, FN_TM // 2, rw), lambda i: (i // tpb, 0, i % tpb, 0)),
        out_shape=jax.ShapeDtypeStruct((n_batch, 2, seq // 2, rw), BF16),
        scratch_shapes=[pltpu.VMEM((d // LANES, FN_TM, LANES), F32)],
        compiler_params=_cparams(("arbitrary",)),
        name="fnet_in",
    )(x2, g, sh, sc, w_in_bf, cs_chan)


FS_TK = 256
FS_SUB = 128


def _fnet_seq_kernel(cs_ref, ab_ref, wout_ref, x_ref, gt_ref, o_ref, *, scale):
    tk = cs_ref.shape[2]
    rw = ab_ref.shape[3]
    pw = rw - wout_ref.shape[1] // 2
    sub = FS_SUB
    for c in range(tk // sub):
        rows = slice(c * sub, (c + 1) * sub)
        pe = jnp.dot(cs_ref[0, 0, rows, :], ab_ref[0, 0, :, :pw], preferred_element_type=F32)
        po = jnp.dot(cs_ref[1, 0, rows, :], ab_ref[0, 1, :, :pw], preferred_element_type=F32)
        qe = jnp.dot(cs_ref[0, 1, rows, :], ab_ref[0, 0, :, pw:], preferred_element_type=F32)
        qo = jnp.dot(cs_ref[1, 1, rows, :], ab_ref[0, 1, :, pw:], preferred_element_type=F32)
        lo = jnp.concatenate([pe + po, qe + qo], axis=1)
        hi = jnp.concatenate([pe - po, qe - qo], axis=1)
        f = (jnp.concatenate([lo, hi], axis=0) * scale).astype(BF16)
        y = jnp.dot(f, wout_ref[...], preferred_element_type=F32)
        o_ref[0, 0, rows, :] = x_ref[0, 0, rows, :] + gt_ref[0] * y[:sub]
        o_ref[0, 1, rows, :] = x_ref[0, 1, rows, :] + gt_ref[0] * y[sub:]


def _fold_w_out_kernel(w_ref, o_ref):
    d = w_ref.shape[1]
    dg = d // F_GROUPS
    hk = dg // 2
    nc = F_GROUPS * hk
    pw = nc + LANES
    r = lax.broadcasted_iota(jnp.int32, (dg, dg), 0)
    c = lax.broadcasted_iota(jnp.int32, (dg, dg), 1)
    mirror = jnp.where((r + c == dg) | ((r == 0) & (c == 0)), 1.0, 0.0).astype(BF16)
    first = lax.broadcasted_iota(jnp.int32, (hk, 1), 0) == 0
    row_n = lax.broadcasted_iota(jnp.int32, (LANES, 1), 0)
    nyq = jnp.zeros((LANES, d), F32)
    for g in range(F_GROUPS):
        w = w_ref[g * dg:(g + 1) * dg, :]
        wm = jnp.dot(mirror, w.astype(BF16), preferred_element_type=F32)
        o_ref[g * hk:(g + 1) * hk, :] = jnp.where(first, w[:hk], w[:hk] + wm[:hk]).astype(BF16)
        o_ref[pw + g * hk:pw + (g + 1) * hk, :] = jnp.where(first, 0.0, wm[:hk] - w[:hk]).astype(BF16)
        nyq = jnp.where(row_n == g, w[hk:hk + 1], nyq)
    o_ref[nc:pw, :] = nyq.astype(BF16)


def _fold_w_out(w_out):
    d = w_out.shape[0]
    return pl.pallas_call(
        _fold_w_out_kernel,
        grid=(1,),
        in_specs=[pl.BlockSpec((d, d), lambda i: (0, 0), pipeline_mode=pl.Buffered(1))],
        out_specs=pl.BlockSpec((d + LANES, d), lambda i: (0, 0)),
        out_shape=jax.ShapeDtypeStruct((d + LANES, d), BF16),
        compiler_params=_cparams(("arbitrary",)),
        name="fold_w_out",
    )(w_out)


def _fnet_seq(cs_seq, ab, w_fold_bf, x2, gt, n_batch, seq):
    t, d = x2.shape
    hs = seq // 2
    rw = ab.shape[-1]
    scale = 1.0 / float(np.sqrt(seq * (d // F_GROUPS)))
    x4 = x2.reshape(n_batch, 2, hs, d)
    out = pl.pallas_call(
        functools.partial(_fnet_seq_kernel, scale=scale),
        grid=(n_batch, hs // FS_TK),
        in_specs=[
            pl.BlockSpec((2, 2, FS_TK, hs), lambda b, k: (0, 0, k, 0)),
            pl.BlockSpec((1, 2, hs, rw), lambda b, k: (b, 0, 0, 0), pipeline_mode=pl.Buffered(1)),
            pl.BlockSpec((rw, d), lambda b, k: (0, 0), pipeline_mode=pl.Buffered(1)),
            pl.BlockSpec((1, 2, FS_TK, d), lambda b, k: (b, 0, k, 0)),
            pl.BlockSpec((1, 1, d), lambda b, k: (b, 0, 0)),
        ],
        out_specs=pl.BlockSpec((1, 2, FS_TK, d), lambda b, k: (b, 0, k, 0)),
        out_shape=jax.ShapeDtypeStruct((n_batch, 2, hs, d), F32),
        compiler_params=_cparams(("arbitrary", "arbitrary")),
        name="fnet_seq",
    )(cs_seq, ab, w_fold_bf, x4, gt)
    return out.reshape(t, d)


SG_TM = 256


def _gelu_tanh(x):
    c = float(np.sqrt(2.0 / np.pi))
    return x * (0.5 * (1.0 + jnp.tanh(c * (x + 0.044715 * (x * x * x)))))


def _sgu_kernel(x_ref, g_ref, sh_ref, sc_ref, gt_ref, win_ref, bin_ref, gv_ref, ws_ref, bs_ref,
                wout_ref, o_ref, gated_ref):
    tm, d = x_ref.shape
    dh = d // SGU_HEADS
    for c in range(tm // CHUNK):
        rows = slice(c * CHUNK, (c + 1) * CHUNK)
        x = x_ref[rows, :]
        h = _rms(x, g_ref[...]) * (1.0 + sc_ref[0]) + sh_ref[0]
        z = jnp.dot(h.astype(BF16), win_ref[...], preferred_element_type=F32) + bin_ref[...]
        z = _gelu_tanh(z)
        u = z[:, :d]
        v = _rms(z[:, d:], gv_ref[...])
        for hd in range(SGU_HEADS):
            cols = slice(hd * dh, (hd + 1) * dh)
            vc = v[:, cols].astype(BF16)
            m = jnp.dot(ws_ref[hd], vc, preferred_element_type=F32) + bs_ref[:, hd:hd + 1]
            gated_ref[rows, cols] = (u[:, cols] * m).astype(BF16)
        y = jnp.dot(gated_ref[rows, :], wout_ref[...], preferred_element_type=F32)
        o_ref[rows, :] = x + gt_ref[0] * y


def _sgu(x2, g, sh, sc, gt, w_in_bf, b_in, g_v, w_s_bf, b_s_t, w_out_bf, seq):
    t, d = x2.shape
    tpb = seq // SG_TM
    const2 = lambda i: (0, 0)
    perb = lambda i: (i // tpb, 0, 0)
    return pl.pallas_call(
        _sgu_kernel,
        grid=(t // SG_TM,),
        in_specs=[
            pl.BlockSpec((SG_TM, d), lambda i: (i, 0)),
            pl.BlockSpec((1, d), const2),
            pl.BlockSpec((1, 1, d), perb),
            pl.BlockSpec((1, 1, d), perb),
            pl.BlockSpec((1, 1, d), perb),
            pl.BlockSpec((d, 2 * d), const2, pipeline_mode=pl.Buffered(1)),
            pl.BlockSpec((1, 2 * d), const2),
            pl.BlockSpec((1, d), const2),
            pl.BlockSpec((SGU_HEADS, CHUNK, CHUNK), lambda i: (0, 0, 0)),
            pl.BlockSpec((CHUNK, SGU_HEADS), const2),
            pl.BlockSpec((d, d), const2, pipeline_mode=pl.Buffered(1)),
        ],
        out_specs=pl.BlockSpec((SG_TM, d), lambda i: (i, 0)),
        out_shape=jax.ShapeDtypeStruct((t, d), F32),
        scratch_shapes=[pltpu.VMEM((SG_TM, d), BF16)],
        compiler_params=_cparams(("arbitrary",)),
        name="sgu_mix",
    )(x2, g, sh, sc, gt, w_in_bf, b_in, g_v, w_s_bf, b_s_t, w_out_bf)


RT_TM = 256


def _route_kernel(x_ref, g_ref, sh_ref, sc_ref, whl_ref, br_ref, h_ref, route_ref, cnt_ref, rt_ref, carry_ref):
    tm = x_ref.shape[0]
    i = pl.program_id(0)

    @pl.when(i == 0)
    def _():
        carry_ref[...] = jnp.zeros_like(carry_ref)

    h = _rms(x_ref[...], g_ref[...]) * (1.0 + sc_ref[0]) + sh_ref[0]
    h_hi = h.astype(BF16)
    _store_packed(h_ref, h_hi, tm)
    h_lo = (h - h_hi.astype(F32)).astype(BF16)
    whl = whl_ref[...]
    both = jnp.dot(h_hi, whl, preferred_element_type=F32)
    lo = jnp.dot(h_lo, whl[:, :LANES], preferred_element_type=F32)
    lg = both[:, :LANES] + both[:, LANES:] + lo + br_ref[...]

    lane = lax.broadcasted_iota(jnp.int32, (tm, LANES), 1)
    lanef = lane.astype(F32)
    is_g = lane < N_GROUPS
    gl = jnp.where(is_g, lg, NEG_BIG)
    gmax = jnp.max(gl, axis=1, keepdims=True)
    gidx = jnp.min(jnp.where(gl == gmax, lanef, float(LANES)), axis=1, keepdims=True)
    gsum = jnp.sum(jnp.where(is_g, jnp.exp(gl - gmax), 0.0), axis=1, keepdims=True)
    g_w = 1.0 / gsum
    lo_lane = float(N_GROUPS) + gidx * float(E_PER_GROUP)
    in_grp = (lanef >= lo_lane) & (lanef < lo_lane + float(E_PER_GROUP))
    el = jnp.where(in_grp, lg, NEG_BIG)
    v1 = jnp.max(el, axis=1, keepdims=True)
    i1 = jnp.min(jnp.where(el == v1, lanef, float(LANES)), axis=1, keepdims=True)
    el2 = jnp.where(lanef == i1, NEG_BIG, el)
    v2 = jnp.max(el2, axis=1, keepdims=True)
    i2 = jnp.min(jnp.where(el2 == v2, lanef, float(LANES)), axis=1, keepdims=True)
    p = jnp.exp(v2 - v1)
    w0 = g_w / (1.0 + p)
    w1 = g_w * p / (1.0 + p)
    e0 = i1 - float(N_GROUPS)
    e1 = i2 - float(N_GROUPS)

    oh0 = jnp.where(lanef == e0, 1.0, 0.0)
    oh1 = jnp.where(lanef == e1, 1.0, 0.0)
    rr = lax.broadcasted_iota(jnp.int32, (tm, tm), 0)
    cc = lax.broadcasted_iota(jnp.int32, (tm, tm), 1)
    tri = jnp.where(rr > cc, 1.0, 0.0).astype(BF16)
    pre0 = jnp.dot(tri, oh0.astype(BF16), preferred_element_type=F32)
    pre1 = jnp.dot(tri, oh1.astype(BF16), preferred_element_type=F32)
    carry = carry_ref[0:1, :]
    cnt0 = jnp.sum(oh0, axis=0, keepdims=True)
    cnt1 = jnp.sum(oh1, axis=0, keepdims=True)
    rank0 = jnp.sum(oh0 * (pre0 + carry), axis=1, keepdims=True)
    rank1 = jnp.sum(oh1 * (pre1 + carry + cnt0), axis=1, keepdims=True)
    new_carry = carry + cnt0 + cnt1
    carry_ref[...] = jnp.broadcast_to(new_carry, carry_ref.shape)
    cnt_ref[...] = jnp.broadcast_to(new_carry, cnt_ref.shape)

    route = jnp.where(lane == 0, e0, 0.0)
    route = jnp.where(lane == 1, e1, route)
    route = jnp.where(lane == 2, w0, route)
    route = jnp.where(lane == 3, w1, route)
    route = jnp.where(lane == 4, rank0, route)
    route = jnp.where(lane == 5, rank1, route)
    route_ref[...] = route
    rt_ref[...] = jnp.transpose(route)[:SUBLANES, :]


def _route(x2, g, sh, sc, whl, br, seq):
    t, d = x2.shape
    tpb = seq // RT_TM
    pk = d // (2 * LANES)
    perb = lambda i: (i // tpb, 0, 0)
    return pl.pallas_call(
        _route_kernel,
        grid=(t // RT_TM,),
        in_specs=[
            pl.BlockSpec((RT_TM, d), lambda i: (i, 0)),
            pl.BlockSpec((1, d), lambda i: (0, 0)),
            pl.BlockSpec((1, 1, d), perb),
            pl.BlockSpec((1, 1, d), perb),
            pl.BlockSpec((d, 2 * LANES), lambda i: (0, 0)),
            pl.BlockSpec((1, LANES), lambda i: (0, 0)),
        ],
        out_specs=[
            pl.BlockSpec((RT_TM * pk, LANES), lambda i: (i, 0)),
            pl.BlockSpec((RT_TM, LANES), lambda i: (i, 0)),
            pl.BlockSpec((SUBLANES, LANES), lambda i: (0, 0)),
            pl.BlockSpec((SUBLANES, RT_TM), lambda i: (0, i)),
        ],
        out_shape=[
            jax.ShapeDtypeStruct((t * pk, LANES), jnp.uint32),
            jax.ShapeDtypeStruct((t, LANES), F32),
            jax.ShapeDtypeStruct((SUBLANES, LANES), F32),
            jax.ShapeDtypeStruct((SUBLANES, t), F32),
        ],
        scratch_shapes=[pltpu.VMEM((SUBLANES, LANES), F32)],
        compiler_params=_cparams(("arbitrary",)),
        name="moe_route",
    )(x2, g, sh, sc, whl, br)


SC_TM = 1024


SC_UNROLL = 8


RT_E, RT_RANK = 0, 4


def _scatter_kernel(dest_ref, pad_start_ref, pad_n_ref, nu_ref, h_ref, buf_ref, zrow_ref, sem, zsem, tsem, *, pk):
    tm = h_ref.shape[0] // pk
    i = pl.program_id(0)
    blk_rows = EXPERT_BLOCK * pk
    n_blocks = buf_ref.shape[0] // blk_rows

    def tok_rows(tok):
        return pl.ds(pl.multiple_of(tok * pk, pk), pk)

    def row_copy(r, k):
        dst = dest_ref[k, i * tm + r]
        return pltpu.make_async_copy(h_ref.at[tok_rows(r), :], buf_ref.at[tok_rows(dst), :], sem)

    def pad_fill(e, do):
        n = pad_n_ref[e]
        start = pad_start_ref[e]
        size = EXPERT_BLOCK // 2
        while size >= 1:
            has = (n & size) != 0
            cp = pltpu.make_async_copy(zrow_ref.at[pl.ds(0, size * pk), :],
                                       buf_ref.at[pl.ds(pl.multiple_of(start * pk, pk), size * pk), :], zsem)

            @pl.when(has)
            def _():
                do(cp)

            start = start + jnp.where(has, size, 0)
            size //= 2

    def tail_copy(blk):
        row0 = pl.multiple_of(blk * blk_rows, blk_rows)
        return pltpu.make_async_copy(zrow_ref, buf_ref.at[pl.ds(row0, blk_rows), :], tsem)

    @pl.when(i == 0)
    def _():
        zrow_ref[...] = jnp.zeros_like(zrow_ref)

        def per_expert(e, c):
            pad_fill(e, lambda cp: cp.start())
            return c
        lax.fori_loop(0, N_EXPERTS, per_expert, 0)
        lax.fori_loop(nu_ref[0], n_blocks, lambda blk, c: (tail_copy(blk).start(), c)[1], 0)

    def issue(q, c):
        for u in range(SC_UNROLL):
            for k in range(TOP_K):
                row_copy(q * SC_UNROLL + u, k).start(priority=k)
        return c
    lax.fori_loop(0, tm // SC_UNROLL, issue, 0)

    def drain(q, c):
        for u in range(SC_UNROLL):
            for k in range(TOP_K):
                row_copy(q * SC_UNROLL + u, k).wait()
        return c
    lax.fori_loop(0, tm // SC_UNROLL, drain, 0)

    @pl.when(i == 0)
    def _():
        def per_expert(e, c):
            pad_fill(e, lambda cp: cp.wait())
            return c
        lax.fori_loop(0, N_EXPERTS, per_expert, 0)
        lax.fori_loop(nu_ref[0], n_blocks, lambda blk, c: (tail_copy(blk).wait(), c)[1], 0)


def _scatter(dest2, pad_start, pad_n, n_used, hp, n_rows, pk):
    t = hp.shape[0] // pk
    grid_spec = pltpu.PrefetchScalarGridSpec(
        num_scalar_prefetch=4,
        grid=(t // SC_TM,),
        in_specs=[pl.BlockSpec((SC_TM * pk, LANES), lambda i, *_: (i, 0))],
        out_specs=pl.BlockSpec(memory_space=pl.ANY),
        scratch_shapes=[pltpu.VMEM((EXPERT_BLOCK * pk, LANES), hp.dtype), pltpu.SemaphoreType.DMA,
                        pltpu.SemaphoreType.DMA, pltpu.SemaphoreType.DMA],
    )
    return pl.pallas_call(
        functools.partial(_scatter_kernel, pk=pk),
        grid_spec=grid_spec,
        out_shape=jax.ShapeDtypeStruct((n_rows * pk, LANES), hp.dtype),
        compiler_params=_cparams(("arbitrary",)),
        name="moe_scatter",
    )(dest2, pad_start, pad_n, n_used, hp)


EX_RING = 4
W_PARTS = 2


def _expert_kernel(pstart_ref, nblk_ref, nu_ref, wg_hbm, wu_hbm, wd_hbm, buf_ref, yb_ref,
                   wg_st, wu_st, wd_st, wg_bf, wu_bf, wd_bf, xbuf, obuf, w_sem, in_sem, out_sem, tail_sem,
                   *, layer, pk):
    e = pl.program_id(0)
    ne = pl.num_programs(0)
    nb = nblk_ref[e]
    g0 = pstart_ref[e]
    nu = nu_ref[0]
    xrows = EXPERT_BLOCK * pk
    orows = EXPERT_BLOCK * pk
    n_blocks = yb_ref.shape[0] // orows
    cur = e % 2
    nxt = (e + 1) % 2

    def w_copies(ex, slot):
        return [pltpu.make_async_copy(hbm.at[layer, ex], st.at[slot], w_sem.at[slot])
                for hbm, st in ((wg_hbm, wg_st), (wu_hbm, wu_st), (wd_hbm, wd_st))]

    def cast_part(slot, q):
        for st, bf in ((wg_st, wg_bf), (wu_st, wu_bf), (wd_st, wd_bf)):
            n = st.shape[1] // W_PARTS
            r = pl.ds(pl.multiple_of(q * n, n), n)
            bf[slot, r, :] = st[slot, r, :].astype(BF16)

    @pl.when(e == 0)
    def _():
        for c in w_copies(0, 0):
            c.start()
        for c in w_copies(1, 1):
            c.start()
        for c in w_copies(0, 0):
            c.wait()
        for q in range(W_PARTS):
            cast_part(0, q)
        for c in w_copies(2, 0):
            c.start()

    @pl.when((e >= 1) & (e + 2 < ne))
    def _():
        for c in w_copies(e + 2, cur):
            c.start()

    @pl.when(e + 1 < ne)
    def _():
        for c in w_copies(e + 1, nxt):
            c.wait()

    def x_copy(g):
        slot = g % EX_RING
        src = buf_ref.at[pl.ds(pl.multiple_of(g * xrows, xrows), xrows), :]
        return pltpu.make_async_copy(src, xbuf.at[slot], in_sem.at[slot])

    def o_copy(g):
        slot = g % EX_RING
        dst = yb_ref.at[pl.ds(pl.multiple_of(g * orows, orows), orows), :]
        return pltpu.make_async_copy(obuf.at[slot], dst, out_sem.at[slot])

    @pl.when(e == 0)
    def _():
        for k in range(2):
            @pl.when(k < nu)
            def _():
                x_copy(k).start()

    def stage(g, n):
        for k in range(n):
            @pl.when(g + 2 + k < nu)
            def _():
                x_copy(g + 2 + k).start()
        for k in range(n):
            x_copy(g + k).wait()

            @pl.when(g + k >= EX_RING)
            def _():
                o_copy(g + k - EX_RING).wait()

    def compute(g):
        slot = g % EX_RING
        xb = _load_packed(xbuf.at[slot], EXPERT_BLOCK, pk, BF16)
        gte = jnp.dot(xb, wg_bf[cur], preferred_element_type=F32)
        up = jnp.dot(xb, wu_bf[cur], preferred_element_type=F32)
        act = gte * (1.0 / (1.0 + jnp.exp(-gte))) * up
        y = jnp.dot(act.astype(BF16), wd_bf[cur], preferred_element_type=F32)
        _store_packed(obuf.at[slot], y.astype(BF16), EXPERT_BLOCK)

    def pair_body(p, c):
        g = g0 + 2 * p
        stage(g, 2)
        compute(g)
        compute(g + 1)
        o_copy(g).start()
        o_copy(g + 1).start()
        cast_part(nxt, jnp.minimum(p, W_PARTS - 1))
        return c

    n_pairs = nb // 2
    lax.fori_loop(0, n_pairs, pair_body, 0)

    @pl.when(nb % 2 == 1)
    def _():
        g = g0 + nb - 1
        stage(g, 1)
        compute(g)
        o_copy(g).start()

    lax.fori_loop(jnp.minimum(n_pairs, W_PARTS), W_PARTS, lambda q, c: (cast_part(nxt, q), c)[1], 0)

    @pl.when(e == ne - 1)
    def _():
        lax.fori_loop(jnp.maximum(nu - EX_RING, 0), nu, lambda g, c: (o_copy(g).wait(), c)[1], 0)
        obuf[0] = jnp.zeros(obuf.shape[1:], obuf.dtype)

        def tail_copy(blk):
            r = pl.ds(pl.multiple_of(blk * orows, orows), orows)
            return pltpu.make_async_copy(obuf.at[0], yb_ref.at[r, :], tail_sem)

        lax.fori_loop(nu, n_blocks, lambda blk, c: (tail_copy(blk).start(), c)[1], 0)
        lax.fori_loop(nu, n_blocks, lambda blk, c: (tail_copy(blk).wait(), c)[1], 0)


def _experts(gstarts, nblk, n_used, buf, w_gate, w_up, w_down, layer, n_rows, pk):
    d, de = w_gate.shape[-2:]
    assert N_EXPERTS >= 3 and d % W_PARTS == 0 and de % W_PARTS == 0
    hbm = pl.BlockSpec(memory_space=pl.ANY)
    grid_spec = pltpu.PrefetchScalarGridSpec(
        num_scalar_prefetch=3,
        grid=(N_EXPERTS,),
        in_specs=[hbm, hbm, hbm, hbm],
        out_specs=hbm,
        scratch_shapes=[
            pltpu.VMEM((2, d, de), F32), pltpu.VMEM((2, d, de), F32), pltpu.VMEM((2, de, d), F32),
            pltpu.VMEM((2, d, de), BF16), pltpu.VMEM((2, d, de), BF16), pltpu.VMEM((2, de, d), BF16),
            pltpu.VMEM((EX_RING, EXPERT_BLOCK * pk, LANES), buf.dtype),
            pltpu.VMEM((EX_RING, EXPERT_BLOCK * pk, LANES), buf.dtype),
            pltpu.SemaphoreType.DMA((2,)), pltpu.SemaphoreType.DMA((EX_RING,)), pltpu.SemaphoreType.DMA((EX_RING,)),
            pltpu.SemaphoreType.DMA,
        ],
    )
    return pl.pallas_call(
        functools.partial(_expert_kernel, layer=layer, pk=pk),
        grid_spec=grid_spec,
        out_shape=jax.ShapeDtypeStruct((n_rows * pk, LANES), buf.dtype),
        compiler_params=_cparams(("arbitrary",)),
        name="moe_experts",
    )(gstarts, nblk, n_used, w_gate, w_up, w_down, buf)


CB_TM = 256


def _combine_kernel(dest_ref, x_ref, route_ref, gt_ref, gf_ref, yb_ref, o_ref, ybuf, sems, *, final_norm):
    tm, d = x_ref.shape
    fs = d // (2 * LANES)
    i = pl.program_id(0)
    n = pl.num_programs(0)

    def row_copy(tile, slot, r, k):
        src = dest_ref[k, tile * tm + r]
        return pltpu.make_async_copy(yb_ref.at[pl.ds(pl.multiple_of(src * fs, fs), fs), :],
                                     ybuf.at[slot, k, pl.ds(pl.multiple_of(r * fs, fs), fs), :], sems.at[slot])

    def issue(tile, slot):
        def body(q, c):
            for u in range(SC_UNROLL):
                for k in range(TOP_K):
                    row_copy(tile, slot, q * SC_UNROLL + u, k).start(priority=k)
            return c
        lax.fori_loop(0, tm // SC_UNROLL, body, 0)

    @pl.when(i == 0)
    def _():
        issue(0, 0)

    @pl.when(i + 1 < n)
    def _():
        issue(i + 1, (i + 1) % 2)

    slot = i % 2

    def drain(q, c):
        for u in range(SC_UNROLL):
            for k in range(TOP_K):
                row_copy(i, slot, q * SC_UNROLL + u, k).wait()
        return c
    lax.fori_loop(0, tm // SC_UNROLL, drain, 0)

    route = route_ref[...]
    ys = [_load_packed(ybuf.at[slot, k], tm, fs, F32) for k in range(TOP_K)]
    y = ys[0] * route[:, 2:3] + ys[1] * route[:, 3:4]
    xn = x_ref[...] + gt_ref[0] * y
    if final_norm:
        xn = _rms(xn, gf_ref[...])
    o_ref[...] = xn


def _combine(dest2, x2, route, gt, g_final, yb, seq, final_norm):
    t, d = x2.shape
    tpb = seq // CB_TM
    grid_spec = pltpu.PrefetchScalarGridSpec(
        num_scalar_prefetch=1,
        grid=(t // CB_TM,),
        in_specs=[
            pl.BlockSpec((CB_TM, d), lambda i, *_: (i, 0)),
            pl.BlockSpec((CB_TM, LANES), lambda i, *_: (i, 0)),
            pl.BlockSpec((1, 1, d), lambda i, *_: (i // tpb, 0, 0)),
            pl.BlockSpec((1, d), lambda i, *_: (0, 0)),
            pl.BlockSpec(memory_space=pl.ANY),
        ],
        out_specs=pl.BlockSpec((CB_TM, d), lambda i, *_: (i, 0)),
        scratch_shapes=[pltpu.VMEM((2, TOP_K, CB_TM * (d // (2 * LANES)), LANES), yb.dtype),
                        pltpu.SemaphoreType.DMA((2,))],
    )
    return pl.pallas_call(
        functools.partial(_combine_kernel, final_norm=final_norm),
        grid_spec=grid_spec,
        out_shape=jax.ShapeDtypeStruct((t, d), F32),
        compiler_params=_cparams(("arbitrary",)),
        name="moe_combine",
    )(dest2, x2, route, gt, g_final, yb)


def _dft_cos_sin(n):
    k = np.arange(n, dtype=np.int64)
    ang = (np.outer(k, k) % n).astype(np.float64) * (2.0 * np.pi / n)
    return np.cos(ang), np.sin(ang)


@functools.lru_cache(maxsize=None)
def _dft_constants(seq, dg):
    cc, sc = _dft_cos_sin(dg)
    hk = dg // 2
    cs_chan = np.concatenate([cc[:, :hk], sc[:, :hk]], axis=1).astype(np.float32)
    k = np.arange(seq // 2, dtype=np.int64)[:, None]
    parts = []
    for par in range(2):
        n = 2 * np.arange(seq // 2, dtype=np.int64)[None, :] + par
        ang = ((k * n) % seq).astype(np.float64) * (2.0 * np.pi / seq)
        parts.append(np.stack([np.cos(ang), np.sin(ang)], axis=0))
    cs_seq = np.stack(parts, axis=0).astype(np.float32)
    return cs_chan.astype(BF16), cs_seq.astype(BF16)


def _moe(x2, g2, sh, sc, gt, w_group, b_group, w_router, b_router, w_gate, w_up, w_down, layer,
         g_final, seq, final_norm):
    t, d = x2.shape
    w_all = jnp.concatenate([w_group, w_router], axis=1)
    w_all = jnp.pad(w_all, ((0, 0), (0, LANES - w_all.shape[1])))
    w_hi = w_all.astype(BF16)
    w_lo = (w_all - w_hi.astype(F32)).astype(BF16)
    whl = jnp.concatenate([w_hi, w_lo], axis=1)
    br = jnp.pad(jnp.concatenate([b_group, b_router]), (0, LANES - N_GROUPS - N_EXPERTS)).reshape(1, LANES)

    pk = d // (2 * LANES)
    hp, route, cnt, rt = _route(x2, g2, sh, sc, whl, br, seq)

    counts = cnt[0, :N_EXPERTS].astype(jnp.int32)
    padded = (counts + EXPERT_BLOCK - 1) // EXPERT_BLOCK * EXPERT_BLOCK
    pends = jnp.cumsum(padded)
    pstarts = pends - padded
    n_assign = t * TOP_K
    n_blocks = (n_assign + N_EXPERTS * (EXPERT_BLOCK - 1) + EXPERT_BLOCK - 1) // EXPERT_BLOCK
    n_rows = n_blocks * EXPERT_BLOCK
    er = rt.astype(jnp.int32)
    pstarts = pstarts.astype(jnp.int32)
    e_ids = er[RT_E:RT_E + TOP_K]
    is_e = e_ids[:, None, :] == jnp.arange(N_EXPERTS, dtype=jnp.int32)[None, :, None]
    dest2 = jnp.sum(jnp.where(is_e, pstarts[None, :, None], 0), axis=1) + er[RT_RANK:RT_RANK + TOP_K]
    n_used = (pends[-1:] // EXPERT_BLOCK).astype(jnp.int32)

    buf = _scatter(dest2, (pstarts + counts).astype(jnp.int32), (padded - counts).astype(jnp.int32), n_used,
                   hp, n_rows, pk)
    yb = _experts((pstarts // EXPERT_BLOCK).astype(jnp.int32), (padded // EXPERT_BLOCK).astype(jnp.int32), n_used,
                  buf, w_gate, w_up, w_down, layer, n_rows, pk)
    return _combine(dest2, x2, route, gt, g_final, yb, seq, final_norm)


def kernel(x, c, g_norm1, g_norm2, w_ada, b_ada, fa_w_in, fa_w_out, sg_w_in, sg_b_in, sg_g_v, sg_w_s, sg_b_s, sg_w_out, w_group, b_group, w_router, b_router, w_gate, w_up, w_down, g_final):
    n_batch, seq, d = x.shape
    depth = w_ada.shape[0]
    t = n_batch * seq
    x2 = x.reshape(t, d)

    mod = _ada(c, w_ada, b_ada)
    cs_chan, cs_seq = _dft_constants(seq, d // F_GROUPS)
    gfin = g_final.reshape(1, d)

    for l in range(depth):
        parts = [mod[l, :n_batch, k * d:(k + 1) * d].reshape(n_batch, 1, d) for k in range(6)]
        sh1, sc1, gt1, sh2, sc2, gt2 = parts
        g1 = g_norm1[l].reshape(1, d)
        j = l // 2
        if l % 2 == 0:
            ab = _fnet_in(x2, g1, sh1, sc1, fa_w_in[j].astype(BF16), jnp.asarray(cs_chan), n_batch, seq)
            x2 = _fnet_seq(jnp.asarray(cs_seq), ab, _fold_w_out(fa_w_out[j]), x2, gt1, n_batch, seq)
        else:
            x2 = _sgu(x2, g1, sh1, sc1, gt1, sg_w_in[j].astype(BF16), sg_b_in[j].reshape(1, 2 * d),
                      sg_g_v[j].reshape(1, d), sg_w_s[j].astype(BF16), sg_b_s[j].T, sg_w_out[j].astype(BF16), seq)
        x2 = _moe(x2, g_norm2[l].reshape(1, d), sh2, sc2, gt2, w_group[l], b_group[l], w_router[l], b_router[l],
                  w_gate, w_up, w_down, l, gfin, seq, final_norm=(l == depth - 1))
    return x2.reshape(n_batch, seq, d)
```

```python
import functools

import numpy as np
import jax
import jax.numpy as jnp
from jax import lax
from jax.experimental import pallas as pl
from jax.experimental.pallas import tpu as pltpu

F32 = jnp.float32
BF16 = jnp.bfloat16

EPS = 1e-6
F_GROUPS = 4
SGU_HEADS = 8
CHUNK = 128
N_GROUPS = 4
E_PER_GROUP = 8
N_EXPERTS = N_GROUPS * E_PER_GROUP
TOP_K = 2
EXPERT_BLOCK = 256

LANES = 128
SUBLANES = 8
VMEM_LIMIT = 56 * 1024 * 1024
NEG_BIG = -1e30


def _cparams(sem, vmem=VMEM_LIMIT):
    return pltpu.CompilerParams(dimension_semantics=sem, vmem_limit_bytes=vmem)


def _rms(x, g):
    return x * lax.rsqrt(jnp.mean(x * x, axis=-1, keepdims=True) + EPS) * g


def _store_packed(ref, vals_bf16, n_tok):
    pk = vals_bf16.shape[1] // (2 * LANES)
    bits = pltpu.bitcast(vals_bf16.astype(F32), jnp.uint32)
    for s in range(pk):
        low = bits[:, s * LANES:(s + 1) * LANES]
        high = bits[:, (s + pk) * LANES:(s + pk + 1) * LANES]
        ref[pl.ds(s, n_tok, stride=pk), :] = (high & jnp.uint32(0xFFFF0000)) | (low >> jnp.uint32(16))


def _load_packed(ref, n_tok, pk, dtype):
    lows, highs = [], []
    for s in range(pk):
        w = ref[pl.ds(s, n_tok, stride=pk), :]
        lows.append(pltpu.bitcast(w << jnp.uint32(16), F32).astype(dtype))
        highs.append(pltpu.bitcast(w & jnp.uint32(0xFFFF0000), F32).astype(dtype))
    return jnp.concatenate(lows + highs, axis=1)


ADA_TN = 1024


def _ada_kernel(c_ref, w_ref, b_ref, o_ref, lhs_ref):
    @pl.when((pl.program_id(0) == 0) & (pl.program_id(1) == 0))
    def _():
        cv = c_ref[...]
        s = cv / (1.0 + jnp.exp(-cv))
        s_hi = s.astype(BF16).astype(F32)
        lhs_ref[...] = jnp.concatenate([s_hi, s - s_hi], axis=0).astype(BF16)

    w = w_ref[0]
    w_hi = w.astype(BF16)
    w_lo = (w - w_hi.astype(F32)).astype(BF16)
    lhs = lhs_ref[...]
    r = jnp.dot(lhs, w_hi, preferred_element_type=F32) + jnp.dot(lhs, w_lo, preferred_element_type=F32)
    o_ref[0] = r[:SUBLANES] + r[SUBLANES:] + b_ref[0]


def _ada(c, w_ada, b_ada):
    n_batch, d = c.shape
    depth, _, n6 = w_ada.shape
    assert n_batch <= SUBLANES
    c8 = jnp.pad(c, ((0, SUBLANES - n_batch), (0, 0)))
    return pl.pallas_call(
        _ada_kernel,
        grid=(depth, n6 // ADA_TN),
        in_specs=[
            pl.BlockSpec((SUBLANES, d), lambda l, j: (0, 0)),
            pl.BlockSpec((1, d, ADA_TN), lambda l, j: (l, 0, j)),
            pl.BlockSpec((1, 1, ADA_TN), lambda l, j: (l, 0, j)),
        ],
        out_specs=pl.BlockSpec((1, SUBLANES, ADA_TN), lambda l, j: (l, 0, j)),
        out_shape=jax.ShapeDtypeStruct((depth, SUBLANES, n6), F32),
        scratch_shapes=[pltpu.VMEM((2 * SUBLANES, d), BF16)],
        compiler_params=_cparams(("arbitrary", "arbitrary")),
        name="ada_mod",
    )(c8, w_ada, b_ada.reshape(depth, 1, n6))


FN_TM = 256


def _fnet_in_kernel(x_ref, g_ref, sh_ref, sc_ref, win_ref, cs_ref, ab_ref, xs_ref):
    tm, d = x_ref.shape
    dg = d // F_GROUPS
    half = tm // 2
    nch = d // LANES
    for c in range(nch):
        xs_ref[c] = x_ref[:, c * LANES:(c + 1) * LANES]
    x = jnp.concatenate(
        [jnp.concatenate([xs_ref[c, pl.ds(par, half, stride=2), :] for c in range(nch)], axis=1)
         for par in range(2)], axis=0)
    h = _rms(x, g_ref[...]) * (1.0 + sc_ref[0]) + sh_ref[0]
    z = jnp.dot(h.astype(BF16), win_ref[...], preferred_element_type=F32)
    hk = dg // 2
    pw = F_GROUPS * hk + LANES
    lane = lax.broadcasted_iota(jnp.int32, (1, dg), 1)
    alt = (1 - 2 * (lane & 1)).astype(F32)
    lane_n = lax.broadcasted_iota(jnp.int32, (tm, LANES), 1)
    nyq = jnp.zeros((tm, LANES), F32)
    for g in range(F_GROUPS):
        zg = z[:, g * dg:(g + 1) * dg]
        ab = jnp.dot(zg.astype(BF16), cs_ref[...], preferred_element_type=F32)
        nyq = jnp.where(lane_n == g, jnp.sum(zg * alt, axis=-1, keepdims=True), nyq)
        for par in range(2):
            rows = slice(par * half, (par + 1) * half)
            ab_ref[0, par, :, g * hk:(g + 1) * hk] = ab[rows, :hk].astype(BF16)
            ab_ref[0, par, :, pw + g * hk:pw + (g + 1) * hk] = ab[rows, hk:].astype(BF16)
    for par in range(2):
        ab_ref[0, par, :, F_GROUPS * hk:pw] = nyq[par * half:(par + 1) * half].astype(BF16)


def _fnet_in(x2, g, sh, sc, w_in_bf, cs_chan, n_batch, seq):
    t, d = x2.shape
    tpb = seq // FN_TM
    dg = d // F_GROUPS
    rw = d + LANES
    return pl.pallas_call(
        _fnet_in_kernel,
        grid=(t // FN_TM,),
        in_specs=[
            pl.BlockSpec((FN_TM, d), lambda i: (i, 0)),
            pl.BlockSpec((1, d), lambda i: (0, 0)),
            pl.BlockSpec((1, 1, d), lambda i: (i // tpb, 0, 0)),
            pl.BlockSpec((1, 1, d), lambda i: (i // tpb, 0, 0)),
            pl.BlockSpec((d, d), lambda i: (0, 0), pipeline_mode=pl.Buffered(1)),
            pl.BlockSpec((dg, dg), lambda i: (0, 0), pipeline_mode=pl.Buffered(1)),
        ],
        out_specs=pl.BlockSpec((1, 2, FN_TM // 2, rw), lambda i: (i // tpb, 0, i % tpb, 0)),
        out_shape=jax.ShapeDtypeStruct((n_batch, 2, seq // 2, rw), BF16),
        scratch_shapes=[pltpu.VMEM((d // LANES, FN_TM, LANES), F32)],
        compiler_params=_cparams(("arbitrary",)),
        name="fnet_in",
    )(x2, g, sh, sc, w_in_bf, cs_chan)


FS_TK = 256
FS_SUB = 128


def _fnet_seq_kernel(cs_ref, ab_ref, wout_ref, x_ref, gt_ref, o_ref, *, scale):
    tk = cs_ref.shape[2]
    rw = ab_ref.shape[3]
    pw = rw - wout_ref.shape[1] // 2
    sub = FS_SUB
    for c in range(tk // sub):
        rows = slice(c * sub, (c + 1) * sub)
        pe = jnp.dot(cs_ref[0, 0, rows, :], ab_ref[0, 0, :, :pw], preferred_element_type=F32)
        po = jnp.dot(cs_ref[1, 0, rows, :], ab_ref[0, 1, :, :pw], preferred_element_type=F32)
        qe = jnp.dot(cs_ref[0, 1, rows, :], ab_ref[0, 0, :, pw:], preferred_element_type=F32)
        qo = jnp.dot(cs_ref[1, 1, rows, :], ab_ref[0, 1, :, pw:], preferred_element_type=F32)
        lo = jnp.concatenate([pe + po, qe + qo], axis=1)
        hi = jnp.concatenate([pe - po, qe - qo], axis=1)
        f = (jnp.concatenate([lo, hi], axis=0) * scale).astype(BF16)
        y = jnp.dot(f, wout_ref[...], preferred_element_type=F32)
        o_ref[0, 0, rows, :] = x_ref[0, 0, rows, :] + gt_ref[0] * y[:sub]
        o_ref[0, 1, rows, :] = x_ref[0, 1, rows, :] + gt_ref[0] * y[sub:]


def _fold_w_out_kernel(w_ref, o_ref):
    d = w_ref.shape[1]
    dg = d // F_GROUPS
    hk = dg // 2
    nc = F_GROUPS * hk
    pw = nc + LANES
    r = lax.broadcasted_iota(jnp.int32, (dg, dg), 0)
    c = lax.broadcasted_iota(jnp.int32, (dg, dg), 1)
    mirror = jnp.where((r + c == dg) | ((r == 0) & (c == 0)), 1.0, 0.0).astype(BF16)
    first = lax.broadcasted_iota(jnp.int32, (hk, 1), 0) == 0
    row_n = lax.broadcasted_iota(jnp.int32, (LANES, 1), 0)
    nyq = jnp.zeros((LANES, d), F32)
    for g in range(F_GROUPS):
        w = w_ref[g * dg:(g + 1) * dg, :]
        wm = jnp.dot(mirror, w.astype(BF16), preferred_element_type=F32)
        o_ref[g * hk:(g + 1) * hk, :] = jnp.where(first, w[:hk], w[:hk] + wm[:hk]).astype(BF16)
        o_ref[pw + g * hk:pw + (g + 1) * hk, :] = jnp.where(first, 0.0, wm[:hk] - w[:hk]).astype(BF16)
        nyq = jnp.where(row_n == g, w[hk:hk + 1], nyq)
    o_ref[nc:pw, :] = nyq.astype(BF16)


def _fold_w_out(w_out):
    d = w_out.shape[0]
    return pl.pallas_call(
        _fold_w_out_kernel,
        grid=(1,),
        in_specs=[pl.BlockSpec((d, d), lambda i: (0, 0), pipeline_mode=pl.Buffered(1))],
        out_specs=pl.BlockSpec((d + LANES, d), lambda i: (0, 0)),
        out_shape=jax.ShapeDtypeStruct((d + LANES, d), BF16),
        compiler_params=_cparams(("arbitrary",)),
        name="fold_w_out",
    )(w_out)


def _fnet_seq(cs_seq, ab, w_fold_bf, x2, gt, n_batch, seq):
    t, d = x2.shape
    hs = seq // 2
    rw = ab.shape[-1]
    scale = 1.0 / float(np.sqrt(seq * (d // F_GROUPS)))
    x4 = x2.reshape(n_batch, 2, hs, d)
    out = pl.pallas_call(
        functools.partial(_fnet_seq_kernel, scale=scale),
        grid=(n_batch, hs // FS_TK),
        in_specs=[
            pl.BlockSpec((2, 2, FS_TK, hs), lambda b, k: (0, 0, k, 0)),
            pl.BlockSpec((1, 2, hs, rw), lambda b, k: (b, 0, 0, 0), pipeline_mode=pl.Buffered(1)),
            pl.BlockSpec((rw, d), lambda b, k: (0, 0), pipeline_mode=pl.Buffered(1)),
            pl.BlockSpec((1, 2, FS_TK, d), lambda b, k: (b, 0, k, 0)),
            pl.BlockSpec((1, 1, d), lambda b, k: (b, 0, 0)),
        ],
        out_specs=pl.BlockSpec((1, 2, FS_TK, d), lambda b, k: (b, 0, k, 0)),
        out_shape=jax.ShapeDtypeStruct((n_batch, 2, hs, d), F32),
        compiler_params=_cparams(("arbitrary", "arbitrary")),
        name="fnet_seq",
    )(cs_seq, ab, w_fold_bf, x4, gt)
    return out.reshape(t, d)


SG_TM = 256


def _gelu_tanh(x):
    c = float(np.sqrt(2.0 / np.pi))
    return x * (0.5 * (1.0 + jnp.tanh(c * (x + 0.044715 * (x * x * x)))))


def _sgu_kernel(x_ref, g_ref, sh_ref, sc_ref, gt_ref, win_ref, bin_ref, gv_ref, ws_ref, bs_ref,
                wout_ref, o_ref, gated_ref):
    tm, d = x_ref.shape
    dh = d // SGU_HEADS
    for c in range(tm // CHUNK):
        rows = slice(c * CHUNK, (c + 1) * CHUNK)
        x = x_ref[rows, :]
        h = _rms(x, g_ref[...]) * (1.0 + sc_ref[0]) + sh_ref[0]
        z = jnp.dot(h.astype(BF16), win_ref[...], preferred_element_type=F32) + bin_ref[...]
        z = _gelu_tanh(z)
        u = z[:, :d]
        v = _rms(z[:, d:], gv_ref[...])
        for hd in range(SGU_HEADS):
            cols = slice(hd * dh, (hd + 1) * dh)
            vc = v[:, cols].astype(BF16)
            m = jnp.dot(ws_ref[hd], vc, preferred_element_type=F32) + bs_ref[:, hd:hd + 1]
            gated_ref[rows, cols] = (u[:, cols] * m).astype(BF16)
        y = jnp.dot(gated_ref[rows, :], wout_ref[...], preferred_element_type=F32)
        o_ref[rows, :] = x + gt_ref[0] * y


def _sgu(x2, g, sh, sc, gt, w_in_bf, b_in, g_v, w_s_bf, b_s_t, w_out_bf, seq):
    t, d = x2.shape
    tpb = seq // SG_TM
    const2 = lambda i: (0, 0)
    perb = lambda i: (i // tpb, 0, 0)
    return pl.pallas_call(
        _sgu_kernel,
        grid=(t // SG_TM,),
        in_specs=[
            pl.BlockSpec((SG_TM, d), lambda i: (i, 0)),
            pl.BlockSpec((1, d), const2),
            pl.BlockSpec((1, 1, d), perb),
            pl.BlockSpec((1, 1, d), perb),
            pl.BlockSpec((1, 1, d), perb),
            pl.BlockSpec((d, 2 * d), const2, pipeline_mode=pl.Buffered(1)),
            pl.BlockSpec((1, 2 * d), const2),
            pl.BlockSpec((1, d), const2),
            pl.BlockSpec((SGU_HEADS, CHUNK, CHUNK), lambda i: (0, 0, 0)),
            pl.BlockSpec((CHUNK, SGU_HEADS), const2),
            pl.BlockSpec((d, d), const2, pipeline_mode=pl.Buffered(1)),
        ],
        out_specs=pl.BlockSpec((SG_TM, d), lambda i: (i, 0)),
        out_shape=jax.ShapeDtypeStruct((t, d), F32),
        scratch_shapes=[pltpu.VMEM((SG_TM, d), BF16)],
        compiler_params=_cparams(("arbitrary",)),
        name="sgu_mix",
    )(x2, g, sh, sc, gt, w_in_bf, b_in, g_v, w_s_bf, b_s_t, w_out_bf)


RT_TM = 256


def _route_kernel(x_ref, g_ref, sh_ref, sc_ref, whl_ref, br_ref, h_ref, route_ref, cnt_ref, rt_ref, carry_ref):
    tm = x_ref.shape[0]
    i = pl.program_id(0)

    @pl.when(i == 0)
    def _():
        carry_ref[...] = jnp.zeros_like(carry_ref)

    h = _rms(x_ref[...], g_ref[...]) * (1.0 + sc_ref[0]) + sh_ref[0]
    h_hi = h.astype(BF16)
    _store_packed(h_ref, h_hi, tm)
    h_lo = (h - h_hi.astype(F32)).astype(BF16)
    whl = whl_ref[...]
    both = jnp.dot(h_hi, whl, preferred_element_type=F32)
    lo = jnp.dot(h_lo, whl[:, :LANES], preferred_element_type=F32)
    lg = both[:, :LANES] + both[:, LANES:] + lo + br_ref[...]

    lane = lax.broadcasted_iota(jnp.int32, (tm, LANES), 1)
    lanef = lane.astype(F32)
    is_g = lane < N_GROUPS
    gl = jnp.where(is_g, lg, NEG_BIG)
    gmax = jnp.max(gl, axis=1, keepdims=True)
    gidx = jnp.min(jnp.where(gl == gmax, lanef, float(LANES)), axis=1, keepdims=True)
    gsum = jnp.sum(jnp.where(is_g, jnp.exp(gl - gmax), 0.0), axis=1, keepdims=True)
    g_w = 1.0 / gsum
    lo_lane = float(N_GROUPS) + gidx * float(E_PER_GROUP)
    in_grp = (lanef >= lo_lane) & (lanef < lo_lane + float(E_PER_GROUP))
    el = jnp.where(in_grp, lg, NEG_BIG)
    v1 = jnp.max(el, axis=1, keepdims=True)
    i1 = jnp.min(jnp.where(el == v1, lanef, float(LANES)), axis=1, keepdims=True)
    el2 = jnp.where(lanef == i1, NEG_BIG, el)
    v2 = jnp.max(el2, axis=1, keepdims=True)
    i2 = jnp.min(jnp.where(el2 == v2, lanef, float(LANES)), axis=1, keepdims=True)
    p = jnp.exp(v2 - v1)
    w0 = g_w / (1.0 + p)
    w1 = g_w * p / (1.0 + p)
    e0 = i1 - float(N_GROUPS)
    e1 = i2 - float(N_GROUPS)

    oh0 = jnp.where(lanef == e0, 1.0, 0.0)
    oh1 = jnp.where(lanef == e1, 1.0, 0.0)
    rr = lax.broadcasted_iota(jnp.int32, (tm, tm), 0)
    cc = lax.broadcasted_iota(jnp.int32, (tm, tm), 1)
    tri = jnp.where(rr > cc, 1.0, 0.0).astype(BF16)
    pre0 = jnp.dot(tri, oh0.astype(BF16), preferred_element_type=F32)
    pre1 = jnp.dot(tri, oh1.astype(BF16), preferred_element_type=F32)
    carry = carry_ref[0:1, :]
    cnt0 = jnp.sum(oh0, axis=0, keepdims=True)
    cnt1 = jnp.sum(oh1, axis=0, keepdims=True)
    rank0 = jnp.sum(oh0 * (pre0 + carry), axis=1, keepdims=True)
    rank1 = jnp.sum(oh1 * (pre1 + carry + cnt0), axis=1, keepdims=True)
    new_carry = carry + cnt0 + cnt1
    carry_ref[...] = jnp.broadcast_to(new_carry, carry_ref.shape)
    cnt_ref[...] = jnp.broadcast_to(new_carry, cnt_ref.shape)

    route = jnp.where(lane == 0, e0, 0.0)
    route = jnp.where(lane == 1, e1, route)
    route = jnp.where(lane == 2, w0, route)
    route = jnp.where(lane == 3, w1, route)
    route = jnp.where(lane == 4, rank0, route)
    route = jnp.where(lane == 5, rank1, route)
    route_ref[...] = route
    rt_ref[...] = jnp.transpose(route)[:SUBLANES, :]


def _route(x2, g, sh, sc, whl, br, seq):
    t, d = x2.shape
    tpb = seq // RT_TM
    pk = d // (2 * LANES)
    perb = lambda i: (i // tpb, 0, 0)
    return pl.pallas_call(
        _route_kernel,
        grid=(t // RT_TM,),
        in_specs=[
            pl.BlockSpec((RT_TM, d), lambda i: (i, 0)),
            pl.BlockSpec((1, d), lambda i: (0, 0)),
            pl.BlockSpec((1, 1, d), perb),
            pl.BlockSpec((1, 1, d), perb),
            pl.BlockSpec((d, 2 * LANES), lambda i: (0, 0)),
            pl.BlockSpec((1, LANES), lambda i: (0, 0)),
        ],
        out_specs=[
            pl.BlockSpec((RT_TM * pk, LANES), lambda i: (i, 0)),
            pl.BlockSpec((RT_TM, LANES), lambda i: (i, 0)),
            pl.BlockSpec((SUBLANES, LANES), lambda i: (0, 0)),
            pl.BlockSpec((SUBLANES, RT_TM), lambda i: (0, i)),
        ],
        out_shape=[
            jax.ShapeDtypeStruct((t * pk, LANES), jnp.uint32),
            jax.ShapeDtypeStruct((t, LANES), F32),
            jax.ShapeDtypeStruct((SUBLANES, LANES), F32),
            jax.ShapeDtypeStruct((SUBLANES, t), F32),
        ],
        scratch_shapes=[pltpu.VMEM((SUBLANES, LANES), F32)],
        compiler_params=_cparams(("arbitrary",)),
        name="moe_route",
    )(x2, g, sh, sc, whl, br)


SC_TM = 1024


SC_UNROLL = 8


RT_E, RT_RANK = 0, 4


def _scatter_kernel(dest_ref, pad_start_ref, pad_n_ref, nu_ref, h_ref, buf_ref, zrow_ref, sem, zsem, tsem, *, pk):
    tm = h_ref.shape[0] // pk
    i = pl.program_id(0)
    blk_rows = EXPERT_BLOCK * pk
    n_blocks = buf_ref.shape[0] // blk_rows

    def tok_rows(tok):
        return pl.ds(pl.multiple_of(tok * pk, pk), pk)

    def row_copy(r, k):
        dst = dest_ref[k * (dest_ref.shape[0] // TOP_K) + i * tm + r]
        return pltpu.make_async_copy(h_ref.at[tok_rows(r), :], buf_ref.at[tok_rows(dst), :], sem)

    def pad_fill(e, do):
        n = pad_n_ref[e]
        start = pad_start_ref[e]
        size = EXPERT_BLOCK // 2
        while size >= 1:
            has = (n & size) != 0
            cp = pltpu.make_async_copy(zrow_ref.at[pl.ds(0, size * pk), :],
                                       buf_ref.at[pl.ds(pl.multiple_of(start * pk, pk), size * pk), :], zsem)

            @pl.when(has)
            def _():
                do(cp)

            start = start + jnp.where(has, size, 0)
            size //= 2

    def tail_copy(blk):
        row0 = pl.multiple_of(blk * blk_rows, blk_rows)
        return pltpu.make_async_copy(zrow_ref, buf_ref.at[pl.ds(row0, blk_rows), :], tsem)

    @pl.when(i == 0)
    def _():
        zrow_ref[...] = jnp.zeros_like(zrow_ref)

        def per_expert(e, c):
            pad_fill(e, lambda cp: cp.start())
            return c
        lax.fori_loop(0, N_EXPERTS, per_expert, 0)
        lax.fori_loop(nu_ref[0], n_blocks, lambda blk, c: (tail_copy(blk).start(), c)[1], 0)

    def issue(q, c):
        for u in range(SC_UNROLL):
            for k in range(TOP_K):
                row_copy(q * SC_UNROLL + u, k).start(priority=k)
        return c
    lax.fori_loop(0, tm // SC_UNROLL, issue, 0)

    def drain(q, c):
        for u in range(SC_UNROLL):
            for k in range(TOP_K):
                row_copy(q * SC_UNROLL + u, k).wait()
        return c
    lax.fori_loop(0, tm // SC_UNROLL, drain, 0)

    @pl.when(i == 0)
    def _():
        def per_expert(e, c):
            pad_fill(e, lambda cp: cp.wait())
            return c
        lax.fori_loop(0, N_EXPERTS, per_expert, 0)
        lax.fori_loop(nu_ref[0], n_blocks, lambda blk, c: (tail_copy(blk).wait(), c)[1], 0)


def _scatter(dest2, pad_start, pad_n, n_used, hp, n_rows, pk):
    t = hp.shape[0] // pk
    grid_spec = pltpu.PrefetchScalarGridSpec(
        num_scalar_prefetch=4,
        grid=(t // SC_TM,),
        in_specs=[pl.BlockSpec((SC_TM * pk, LANES), lambda i, *_: (i, 0))],
        out_specs=pl.BlockSpec(memory_space=pl.ANY),
        scratch_shapes=[pltpu.VMEM((EXPERT_BLOCK * pk, LANES), hp.dtype), pltpu.SemaphoreType.DMA,
                        pltpu.SemaphoreType.DMA, pltpu.SemaphoreType.DMA],
    )
    return pl.pallas_call(
        functools.partial(_scatter_kernel, pk=pk),
        grid_spec=grid_spec,
        out_shape=jax.ShapeDtypeStruct((n_rows * pk, LANES), hp.dtype),
        compiler_params=_cparams(("arbitrary",)),
        name="moe_scatter",
    )(dest2, pad_start, pad_n, n_used, hp)


EX_RING = 4
W_PARTS = 2


def _expert_kernel(pstart_ref, nblk_ref, nu_ref, wg_hbm, wu_hbm, wd_hbm, buf_ref, yb_ref,
                   wg_st, wu_st, wd_st, wg_bf, wu_bf, wd_bf, xbuf, obuf, w_sem, in_sem, out_sem, tail_sem,
                   *, layer, pk):
    e = pl.program_id(0)
    ne = pl.num_programs(0)
    nb = nblk_ref[e]
    g0 = pstart_ref[e]
    nu = nu_ref[0]
    xrows = EXPERT_BLOCK * pk
    orows = EXPERT_BLOCK * pk
    n_blocks = yb_ref.shape[0] // orows
    cur = e % 2
    nxt = (e + 1) % 2

    def w_copies(ex, slot):
        return [pltpu.make_async_copy(hbm.at[layer, ex], st.at[slot], w_sem.at[slot])
                for hbm, st in ((wg_hbm, wg_st), (wu_hbm, wu_st), (wd_hbm, wd_st))]

    def cast_part(slot, q):
        for st, bf in ((wg_st, wg_bf), (wu_st, wu_bf), (wd_st, wd_bf)):
            n = st.shape[1] // W_PARTS
            r = pl.ds(pl.multiple_of(q * n, n), n)
            bf[slot, r, :] = st[slot, r, :].astype(BF16)

    @pl.when(e == 0)
    def _():
        for c in w_copies(0, 0):
            c.start()
        for c in w_copies(1, 1):
            c.start()
        for c in w_copies(0, 0):
            c.wait()
        for q in range(W_PARTS):
            cast_part(0, q)
        for c in w_copies(2, 0):
            c.start()

    @pl.when((e >= 1) & (e + 2 < ne))
    def _():
        for c in w_copies(e + 2, cur):
            c.start()

    @pl.when(e + 1 < ne)
    def _():
        for c in w_copies(e + 1, nxt):
            c.wait()

    def x_copy(g):
        slot = g % EX_RING
        src = buf_ref.at[pl.ds(pl.multiple_of(g * xrows, xrows), xrows), :]
        return pltpu.make_async_copy(src, xbuf.at[slot], in_sem.at[slot])

    def o_copy(g):
        slot = g % EX_RING
        dst = yb_ref.at[pl.ds(pl.multiple_of(g * orows, orows), orows), :]
        return pltpu.make_async_copy(obuf.at[slot], dst, out_sem.at[slot])

    @pl.when(e == 0)
    def _():
        for k in range(2):
            @pl.when(k < nu)
            def _():
                x_copy(k).start()

    def stage(g, n):
        for k in range(n):
            @pl.when(g + 2 + k < nu)
            def _():
                x_copy(g + 2 + k).start()
        for k in range(n):
            x_copy(g + k).wait()

            @pl.when(g + k >= EX_RING)
            def _():
                o_copy(g + k - EX_RING).wait()

    def compute(g):
        slot = g % EX_RING
        xb = _load_packed(xbuf.at[slot], EXPERT_BLOCK, pk, BF16)
        gte = jnp.dot(xb, wg_bf[cur], preferred_element_type=F32)
        up = jnp.dot(xb, wu_bf[cur], preferred_element_type=F32)
        act = gte * (1.0 / (1.0 + jnp.exp(-gte))) * up
        y = jnp.dot(act.astype(BF16), wd_bf[cur], preferred_element_type=F32)
        _store_packed(obuf.at[slot], y.astype(BF16), EXPERT_BLOCK)

    def pair_body(p, c):
        g = g0 + 2 * p
        stage(g, 2)
        compute(g)
        compute(g + 1)
        o_copy(g).start()
        o_copy(g + 1).start()
        cast_part(nxt, jnp.minimum(p, W_PARTS - 1))
        return c

    n_pairs = nb // 2
    lax.fori_loop(0, n_pairs, pair_body, 0)

    @pl.when(nb % 2 == 1)
    def _():
        g = g0 + nb - 1
        stage(g, 1)
        compute(g)
        o_copy(g).start()

    lax.fori_loop(jnp.minimum(n_pairs, W_PARTS), W_PARTS, lambda q, c: (cast_part(nxt, q), c)[1], 0)

    @pl.when(e == ne - 1)
    def _():
        lax.fori_loop(jnp.maximum(nu - EX_RING, 0), nu, lambda g, c: (o_copy(g).wait(), c)[1], 0)
        obuf[0] = jnp.zeros(obuf.shape[1:], obuf.dtype)

        def tail_copy(blk):
            r = pl.ds(pl.multiple_of(blk * orows, orows), orows)
            return pltpu.make_async_copy(obuf.at[0], yb_ref.at[r, :], tail_sem)

        lax.fori_loop(nu, n_blocks, lambda blk, c: (tail_copy(blk).start(), c)[1], 0)
        lax.fori_loop(nu, n_blocks, lambda blk, c: (tail_copy(blk).wait(), c)[1], 0)


def _experts(gstarts, nblk, n_used, buf, w_gate, w_up, w_down, layer, n_rows, pk):
    d, de = w_gate.shape[-2:]
    assert N_EXPERTS >= 3 and d % W_PARTS == 0 and de % W_PARTS == 0
    hbm = pl.BlockSpec(memory_space=pl.ANY)
    grid_spec = pltpu.PrefetchScalarGridSpec(
        num_scalar_prefetch=3,
        grid=(N_EXPERTS,),
        in_specs=[hbm, hbm, hbm, hbm],
        out_specs=hbm,
        scratch_shapes=[
            pltpu.VMEM((2, d, de), F32), pltpu.VMEM((2, d, de), F32), pltpu.VMEM((2, de, d), F32),
            pltpu.VMEM((2, d, de), BF16), pltpu.VMEM((2, d, de), BF16), pltpu.VMEM((2, de, d), BF16),
            pltpu.VMEM((EX_RING, EXPERT_BLOCK * pk, LANES), buf.dtype),
            pltpu.VMEM((EX_RING, EXPERT_BLOCK * pk, LANES), buf.dtype),
            pltpu.SemaphoreType.DMA((2,)), pltpu.SemaphoreType.DMA((EX_RING,)), pltpu.SemaphoreType.DMA((EX_RING,)),
            pltpu.SemaphoreType.DMA,
        ],
    )
    return pl.pallas_call(
        functools.partial(_expert_kernel, layer=layer, pk=pk),
        grid_spec=grid_spec,
        out_shape=jax.ShapeDtypeStruct((n_rows * pk, LANES), buf.dtype),
        compiler_params=_cparams(("arbitrary",)),
        name="moe_experts",
    )(gstarts, nblk, n_used, w_gate, w_up, w_down, buf)


CB_TM = 256


def _combine_kernel(dest_ref, x_ref, route_ref, gt_ref, gf_ref, yb_ref, o_ref, ybuf, sems, *, final_norm):
    tm, d = x_ref.shape
    fs = d // (2 * LANES)
    i = pl.program_id(0)
    n = pl.num_programs(0)

    def row_copy(tile, slot, r, k):
        src = dest_ref[k * (dest_ref.shape[0] // TOP_K) + tile * tm + r]
        return pltpu.make_async_copy(yb_ref.at[pl.ds(pl.multiple_of(src * fs, fs), fs), :],
                                     ybuf.at[slot, k, pl.ds(pl.multiple_of(r * fs, fs), fs), :], sems.at[slot])

    def issue(tile, slot):
        def body(q, c):
            for u in range(SC_UNROLL):
                for k in range(TOP_K):
                    row_copy(tile, slot, q * SC_UNROLL + u, k).start(priority=k)
            return c
        lax.fori_loop(0, tm // SC_UNROLL, body, 0)

    @pl.when(i == 0)
    def _():
        issue(0, 0)

    @pl.when(i + 1 < n)
    def _():
        issue(i + 1, (i + 1) % 2)

    slot = i % 2

    def drain(q, c):
        for u in range(SC_UNROLL):
            for k in range(TOP_K):
                row_copy(i, slot, q * SC_UNROLL + u, k).wait()
        return c
    lax.fori_loop(0, tm // SC_UNROLL, drain, 0)

    route = route_ref[...]
    ys = [_load_packed(ybuf.at[slot, k], tm, fs, F32) for k in range(TOP_K)]
    y = ys[0] * route[:, 2:3] + ys[1] * route[:, 3:4]
    xn = x_ref[...] + gt_ref[0] * y
    if final_norm:
        xn = _rms(xn, gf_ref[...])
    o_ref[...] = xn


def _combine(dest2, x2, route, gt, g_final, yb, seq, final_norm):
    t, d = x2.shape
    tpb = seq // CB_TM
    grid_spec = pltpu.PrefetchScalarGridSpec(
        num_scalar_prefetch=1,
        grid=(t // CB_TM,),
        in_specs=[
            pl.BlockSpec((CB_TM, d), lambda i, *_: (i, 0)),
            pl.BlockSpec((CB_TM, LANES), lambda i, *_: (i, 0)),
            pl.BlockSpec((1, 1, d), lambda i, *_: (i // tpb, 0, 0)),
            pl.BlockSpec((1, d), lambda i, *_: (0, 0)),
            pl.BlockSpec(memory_space=pl.ANY),
        ],
        out_specs=pl.BlockSpec((CB_TM, d), lambda i, *_: (i, 0)),
        scratch_shapes=[pltpu.VMEM((2, TOP_K, CB_TM * (d // (2 * LANES)), LANES), yb.dtype),
                        pltpu.SemaphoreType.DMA((2,))],
    )
    return pl.pallas_call(
        functools.partial(_combine_kernel, final_norm=final_norm),
        grid_spec=grid_spec,
        out_shape=jax.ShapeDtypeStruct((t, d), F32),
        compiler_params=_cparams(("arbitrary",)),
        name="moe_combine",
    )(dest2, x2, route, gt, g_final, yb)


def _dft_cos_sin(n):
    k = np.arange(n, dtype=np.int64)
    ang = (np.outer(k, k) % n).astype(np.float64) * (2.0 * np.pi / n)
    return np.cos(ang), np.sin(ang)


@functools.lru_cache(maxsize=None)
def _dft_constants(seq, dg):
    cc, sc = _dft_cos_sin(dg)
    hk = dg // 2
    cs_chan = np.concatenate([cc[:, :hk], sc[:, :hk]], axis=1).astype(np.float32)
    k = np.arange(seq // 2, dtype=np.int64)[:, None]
    parts = []
    for par in range(2):
        n = 2 * np.arange(seq // 2, dtype=np.int64)[None, :] + par
        ang = ((k * n) % seq).astype(np.float64) * (2.0 * np.pi / seq)
        parts.append(np.stack([np.cos(ang), np.sin(ang)], axis=0))
    cs_seq = np.stack(parts, axis=0).astype(np.float32)
    return cs_chan.astype(BF16), cs_seq.astype(BF16)


def _moe(x2, g2, sh, sc, gt, w_group, b_group, w_router, b_router, w_gate, w_up, w_down, layer,
         g_final, seq, final_norm):
    t, d = x2.shape
    w_all = jnp.concatenate([w_group, w_router], axis=1)
    w_all = jnp.pad(w_all, ((0, 0), (0, LANES - w_all.shape[1])))
    w_hi = w_all.astype(BF16)
    w_lo = (w_all - w_hi.astype(F32)).astype(BF16)
    whl = jnp.concatenate([w_hi, w_lo], axis=1)
    br = jnp.pad(jnp.concatenate([b_group, b_router]), (0, LANES - N_GROUPS - N_EXPERTS)).reshape(1, LANES)

    pk = d // (2 * LANES)
    hp, route, cnt, rt = _route(x2, g2, sh, sc, whl, br, seq)

    counts = cnt[0, :N_EXPERTS].astype(jnp.int32)
    padded = (counts + EXPERT_BLOCK - 1) // EXPERT_BLOCK * EXPERT_BLOCK
    pends = jnp.cumsum(padded)
    pstarts = pends - padded
    n_assign = t * TOP_K
    n_blocks = (n_assign + N_EXPERTS * (EXPERT_BLOCK - 1) + EXPERT_BLOCK - 1) // EXPERT_BLOCK
    n_rows = n_blocks * EXPERT_BLOCK
    er = rt.astype(jnp.int32)
    pstarts = pstarts.astype(jnp.int32)
    e_ids = er[RT_E:RT_E + TOP_K]
    is_e = e_ids[:, None, :] == jnp.arange(N_EXPERTS, dtype=jnp.int32)[None, :, None]
    dest2 = jnp.sum(jnp.where(is_e, pstarts[None, :, None], 0), axis=1) + er[RT_RANK:RT_RANK + TOP_K]
    dest2 = dest2.reshape(TOP_K * t)
    n_used = (pends[-1:] // EXPERT_BLOCK).astype(jnp.int32)

    buf = _scatter(dest2, (pstarts + counts).astype(jnp.int32), (padded - counts).astype(jnp.int32), n_used,
                   hp, n_rows, pk)
    yb = _experts((pstarts // EXPERT_BLOCK).astype(jnp.int32), (padded // EXPERT_BLOCK).astype(jnp.int32), n_used,
                  buf, w_gate, w_up, w_down, layer, n_rows, pk)
    return _combine(dest2, x2, route, gt, g_final, yb, seq, final_norm)


def kernel(x, c, g_norm1, g_norm2, w_ada, b_ada, fa_w_in, fa_w_out, sg_w_in, sg_b_in, sg_g_v, sg_w_s, sg_b_s, sg_w_out, w_group, b_group, w_router, b_router, w_gate, w_up, w_down, g_final):
    n_batch, seq, d = x.shape
    depth = w_ada.shape[0]
    t = n_batch * seq
    x2 = x.reshape(t, d)

    mod = _ada(c, w_ada, b_ada)
    cs_chan, cs_seq = _dft_constants(seq, d // F_GROUPS)
    gfin = g_final.reshape(1, d)

    for l in range(depth):
        parts = [mod[l, :n_batch, k * d:(k + 1) * d].reshape(n_batch, 1, d) for k in range(6)]
        sh1, sc1, gt1, sh2, sc2, gt2 = parts
        g1 = g_norm1[l].reshape(1, d)
        j = l // 2
        if l % 2 == 0:
            ab = _fnet_in(x2, g1, sh1, sc1, fa_w_in[j].astype(BF16), jnp.asarray(cs_chan), n_batch, seq)
            x2 = _fnet_seq(jnp.asarray(cs_seq), ab, _fold_w_out(fa_w_out[j]), x2, gt1, n_batch, seq)
        else:
            x2 = _sgu(x2, g1, sh1, sc1, gt1, sg_w_in[j].astype(BF16), sg_b_in[j].reshape(1, 2 * d),
                      sg_g_v[j].reshape(1, d), sg_w_s[j].astype(BF16), sg_b_s[j].T, sg_w_out[j].astype(BF16), seq)
        x2 = _moe(x2, g_norm2[l].reshape(1, d), sh2, sc2, gt2, w_group[l], b_group[l], w_router[l], b_router[l],
                  w_gate, w_up, w_down, l, gfin, seq, final_norm=(l == depth - 1))
    return x2.reshape(n_batch, seq, d)
```
